```python
import math, functools
import jax, jax.numpy as jnp
from jax import lax
import numpy as np

D_MODEL = 1024
BATCH = 8
SEQ = 2048
DEPTH = 2
DEC_BATCH = 128
DEC_SEQ = 1
PAST_LEN = 2048
PAGE_SIZE = 128

N_HEADS = 4
HEAD_DIM = 64
V_DIM = 2 * HEAD_DIM
QK_W = N_HEADS * 2 * HEAD_DIM
ATTN_W = N_HEADS * V_DIM
POOL_WINDOWS = (2, 4, 8, 16)
N_POOL_GROUPS = len(POOL_WINDOWS)
POOL_W = D_MODEL // 2
POOL_GROUP_W = POOL_W // N_POOL_GROUPS
POOL_BUF = max(POOL_WINDOWS) - 1
N_BRANCH = 2
IN_COLS = 2 * QK_W + ATTN_W + POOL_W + N_BRANCH * D_MODEL
N_BUCKETS = 32
MAX_DISTANCE = 128
D_FF = 2816
N_EXPERTS = 8
TOP_K = 2
D_EXPERT = 3584
N_DENSE = (DEPTH + 1) // 2
N_MOE = DEPTH // 2
Q_BLOCK = 128
EPS = 1e-6

kernel_name = "hybrid_diffattn_pool_gated_decoder_step"

F32 = jnp.float32


def rms_norm(x, gain):
    xf = x.astype(F32)
    y = xf * lax.rsqrt(jnp.mean(xf * xf, axis=-1, keepdims=True) + EPS)
    return (y * gain.astype(F32)).astype(x.dtype)


def lambda_init(layer):
    return 0.8 - 0.6 * math.exp(-0.3 * layer)


def diff_lambda_value(lp, layer):
    lp = lp.astype(F32)
    return (jnp.exp(jnp.sum(lp[0] * lp[1])) - jnp.exp(jnp.sum(lp[2] * lp[3]))
            + lambda_init(layer))


def t5_bucket(rel):
    n = jnp.maximum(rel, 0)
    max_exact = N_BUCKETS // 2
    nf = jnp.maximum(n, 1).astype(F32)
    large = max_exact + (jnp.log(nf / max_exact) / math.log(MAX_DISTANCE / max_exact)
                         * (N_BUCKETS - max_exact)).astype(jnp.int32)
    large = jnp.minimum(large, N_BUCKETS - 1)
    return jnp.where(n < max_exact, n, large)


def rel_bias_for(rel, rel_bias):
    return jnp.transpose(rel_bias.astype(F32)[t5_bucket(rel)], (2, 0, 1))


def in_proj(h, w_in_l, q_gain, k_gain):
    B, T = h.shape[:2]
    z = h @ w_in_l
    q, k, v, u, g = jnp.split(
        z, [QK_W, 2 * QK_W, 2 * QK_W + ATTN_W, 2 * QK_W + ATTN_W + POOL_W], axis=-1)
    q = rms_norm(q.reshape(B, T, N_HEADS, 2, HEAD_DIM), q_gain)
    k = rms_norm(k.reshape(B, T, N_HEADS, 2, HEAD_DIM), k_gain)
    v = v.reshape(B, T, N_HEADS, V_DIM)
    g = jax.nn.sigmoid(g.astype(F32)).astype(h.dtype).reshape(B, T, N_BRANCH, D_MODEL)
    return q, k, v, u, g


def attn_prompt(q, k, v, lam, rel_bias):
    B, S = q.shape[:2]
    nb = S // Q_BLOCK
    scale = HEAD_DIM ** -0.5
    kpos = jnp.arange(S)
    qb = jnp.moveaxis(q.reshape(B, nb, Q_BLOCK, N_HEADS, 2, HEAD_DIM), 1, 0)

    def block(args):
        i, qi = args
        qpos = i * Q_BLOCK + jnp.arange(Q_BLOCK)
        rel = qpos[:, None] - kpos[None, :]
        s = jnp.einsum('bqhmd,bkhmd->bmhqk', qi, k, preferred_element_type=F32) * scale
        s = jnp.where(rel >= 0, s + rel_bias_for(rel, rel_bias), -jnp.inf)
        p = jax.nn.softmax(s, axis=-1)
        a = (p[:, 0] - lam * p[:, 1]).astype(v.dtype)
        return jnp.einsum('bhqk,bkhe->bqhe', a, v)

    o = lax.map(block, (jnp.arange(nb), qb))
    return jnp.moveaxis(o, 0, 1).reshape(B, S, N_HEADS, V_DIM)


def attn_sample(q, k, v, lam, k_past, v_past, rel_bias):
    T = q.shape[1]
    P = k_past.shape[1]
    scale = HEAD_DIM ** -0.5
    qpos = P + jnp.arange(T)
    rel_past = qpos[:, None] - jnp.arange(P)[None, :]
    rel_new = qpos[:, None] - qpos[None, :]
    s_past = (jnp.einsum('bqhmd,bkhmd->bmhqk', q, k_past, preferred_element_type=F32) * scale
              + rel_bias_for(rel_past, rel_bias))
    s_new = jnp.einsum('bqhmd,bkhmd->bmhqk', q, k, preferred_element_type=F32) * scale
    s_new = jnp.where(rel_new >= 0, s_new + rel_bias_for(rel_new, rel_bias), -jnp.inf)
    p = jax.nn.softmax(jnp.concatenate([s_past, s_new], axis=-1), axis=-1)
    a = (p[:, 0] - lam * p[:, 1]).astype(v.dtype)
    return (jnp.einsum('bhqk,bkhe->bqhe', a[..., :P], v_past)
            + jnp.einsum('bhqk,bkhe->bqhe', a[..., P:], v))


def pool_mix(u, buf, pos0, w_pool_l, pool_scale_l):
    B, T = u.shape[:2]
    uf = u.astype(F32)
    ext = jnp.concatenate([buf.astype(F32), uf], axis=1)
    cs = jnp.concatenate([jnp.zeros((B, 1, POOL_W), F32), jnp.cumsum(ext, axis=1)], axis=1)
    pos = pos0 + jnp.arange(T)
    outs = []
    for gi, w in enumerate(POOL_WINDOWS):
        ch = slice(gi * POOL_GROUP_W, (gi + 1) * POOL_GROUP_W)
        win_sum = (cs[:, POOL_BUF + 1:POOL_BUF + 1 + T, ch]
                   - cs[:, POOL_BUF + 1 - w:POOL_BUF + 1 - w + T, ch])
        cnt = jnp.minimum(pos + 1, w).astype(F32)[None, :, None]
        outs.append(win_sum / cnt - uf[:, :, ch])
    d = jnp.stack(outs, axis=2)
    y = jnp.einsum('btgc,gce->btge', d, w_pool_l.astype(F32)).reshape(B, T, POOL_W)
    y = (y * pool_scale_l.astype(F32)).astype(u.dtype)
    return y, ext[:, -POOL_BUF:].astype(u.dtype)


def swiglu(h, wg, wu, wd):
    return (jax.nn.silu(h @ wg) * (h @ wu)) @ wd


def moe_swiglu(h, w_router, wg, wu, wd):
    logits = (h @ w_router).astype(F32)
    top_v, top_i = lax.top_k(logits, TOP_K)
    probs = jax.nn.softmax(top_v, axis=-1)
    comb = jnp.sum(jax.nn.one_hot(top_i, N_EXPERTS, dtype=F32) * probs[..., None], axis=-2)
    comb = comb.astype(h.dtype)
    out = jnp.zeros_like(h)
    for e in range(N_EXPERTS):
        out = out + comb[..., e:e + 1] * swiglu(h, wg[e], wu[e], wd[e])
    return out


def layer_forward(x, c, pool_buf, pos0, attend, layer, w_ada_l, b_ada_l, g_mix_l, g_ffn_l,
                  w_in_l, q_norm_l, k_norm_l, diff_lambda_l, subln_l, w_pool_l, pool_scale_l,
                  w_branch_l, w_out_l, ffn_w):
    mod = jax.nn.silu(c) @ w_ada_l + b_ada_l
    sh1, sc1, gt1, sh2, sc2, gt2 = [t[:, None, :] for t in jnp.split(mod, 6, axis=-1)]
    h = rms_norm(x, g_mix_l) * (1 + sc1) + sh1
    q, k, v, u, g = in_proj(h, w_in_l, q_norm_l, k_norm_l)
    lam = diff_lambda_value(diff_lambda_l, layer)
    o = attend(q, k, v, lam)
    o = rms_norm(o, subln_l) * (1.0 - lambda_init(layer))
    o = o.reshape(o.shape[0], o.shape[1], ATTN_W)
    y_pool, new_buf = pool_mix(u, pool_buf, pos0, w_pool_l, pool_scale_l)
    pa = o @ w_branch_l[:ATTN_W]
    pb = y_pool @ w_branch_l[ATTN_W:]
    merged = g[:, :, 0] * pa + g[:, :, 1] * pb
    x = x + gt1 * (merged @ w_out_l)
    h2 = rms_norm(x, g_ffn_l) * (1 + sc2) + sh2
    if layer % 2 == 0:
        f = swiglu(h2, *ffn_w)
    else:
        f = moe_swiglu(h2, *ffn_w)
    x = x + gt2 * f
    return x, k, v, new_buf


def setup_inputs(seed: int = 0) -> dict:
    key = jax.random.key(seed)
    ks = iter(jax.random.split(key, 40))

    def nrm(shape, scale):
        return jax.random.normal(next(ks), shape, F32) * scale

    n_pages = PAST_LEN // PAGE_SIZE
    n_used = DEC_BATCH * n_pages
    n_phys = n_used + max(1, n_used // 4)
    perm = jax.random.permutation(next(ks), n_phys)
    page_table = perm[:n_used].reshape(DEC_BATCH, n_pages).astype(jnp.int32)
    D = D_MODEL
    return {
        "x_prompt": nrm((BATCH, SEQ, D), 1.0),
        "x_sample": nrm((DEC_BATCH, DEC_SEQ, D), 1.0),
        "c_prompt": nrm((BATCH, D), 1.0),
        "c_sample": nrm((DEC_BATCH, D), 1.0),
        "cache_k": nrm((DEPTH, n_phys, PAGE_SIZE, N_HEADS, 2, HEAD_DIM), 1.0),
        "cache_v": nrm((DEPTH, n_phys, PAGE_SIZE, N_HEADS, V_DIM), 1.0),
        "state_pool": nrm((DEPTH, DEC_BATCH, POOL_BUF, POOL_W), 1.0),
        "page_table": page_table,
        "rel_bias": nrm((N_BUCKETS, N_HEADS), 0.5),
        "w_ada": nrm((DEPTH, D, 6 * D), 0.5 * D ** -0.5),
        "b_ada": nrm((DEPTH, 6 * D), 0.02),
        "g_mix": 1.0 + nrm((DEPTH, D), 0.02),
        "g_ffn": 1.0 + nrm((DEPTH, D), 0.02),
        "w_in": nrm((DEPTH, D, IN_COLS), D ** -0.5),
        "q_norm": 1.0 + nrm((DEPTH, 2, HEAD_DIM), 0.02),
        "k_norm": 1.0 + nrm((DEPTH, 2, HEAD_DIM), 0.02),
        "diff_lambda": nrm((DEPTH, 4, HEAD_DIM), 0.1),
        "subln": 1.0 + nrm((DEPTH, V_DIM), 0.02),
        "w_pool": nrm((DEPTH, N_POOL_GROUPS, POOL_GROUP_W, POOL_GROUP_W), POOL_GROUP_W ** -0.5),
        "pool_scale": 1.0 + nrm((DEPTH, POOL_W), 0.1),
        "w_branch": nrm((DEPTH, ATTN_W + POOL_W, D), (ATTN_W) ** -0.5),
        "w_out": nrm((DEPTH, D, D), D ** -0.5),
        "w_ff_gate": nrm((N_DENSE, D, D_FF), D ** -0.5),
        "w_ff_up": nrm((N_DENSE, D, D_FF), D ** -0.5),
        "w_ff_down": nrm((N_DENSE, D_FF, D), D_FF ** -0.5),
        "w_router": nrm((N_MOE, D, N_EXPERTS), D ** -0.5),
        "w_exp_gate": nrm((N_MOE, N_EXPERTS, D, D_EXPERT), D ** -0.5),
        "w_exp_up": nrm((N_MOE, N_EXPERTS, D, D_EXPERT), D ** -0.5),
        "w_exp_down": nrm((N_MOE, N_EXPERTS, D_EXPERT, D), D_EXPERT ** -0.5),
    }


def reference(x_prompt, x_sample, c_prompt, c_sample, cache_k, cache_v, state_pool, page_table,
              rel_bias, w_ada, b_ada, g_mix, g_ffn, w_in, q_norm, k_norm, diff_lambda, subln,
              w_pool, pool_scale, w_branch, w_out, w_ff_gate, w_ff_up, w_ff_down, w_router,
              w_exp_gate, w_exp_up, w_exp_down):
    B = x_prompt.shape[0]
    DB = x_sample.shape[0]
    past = page_table.shape[1] * PAGE_SIZE
    zero_buf = jnp.zeros((B, POOL_BUF, POOL_W), x_prompt.dtype)
    attend_p = functools.partial(attn_prompt, rel_bias=rel_bias)
    xp, xs = x_prompt, x_sample
    kp_l, vp_l, pp_l, ks_l, vs_l, ps_l = [], [], [], [], [], []
    for l in range(DEPTH):
        j = l // 2
        if l % 2 == 0:
            ffn_w = (w_ff_gate[j], w_ff_up[j], w_ff_down[j])
        else:
            ffn_w = (w_router[j], w_exp_gate[j], w_exp_up[j], w_exp_down[j])
        lw = (w_ada[l], b_ada[l], g_mix[l], g_ffn[l], w_in[l], q_norm[l], k_norm[l],
              diff_lambda[l], subln[l], w_pool[l], pool_scale[l], w_branch[l], w_out[l], ffn_w)
        xp, kp, vp, bp = layer_forward(xp, c_prompt, zero_buf, 0, attend_p, l, *lw)
        k_past = cache_k[l, page_table].reshape(DB, past, N_HEADS, 2, HEAD_DIM)
        v_past = cache_v[l, page_table].reshape(DB, past, N_HEADS, V_DIM)
        attend_s = functools.partial(attn_sample, k_past=k_past, v_past=v_past,
                                     rel_bias=rel_bias)
        xs, ks_, vs_, bs = layer_forward(xs, c_sample, state_pool[l], past, attend_s, l, *lw)
        kp_l.append(kp); vp_l.append(vp); pp_l.append(bp)
        ks_l.append(ks_); vs_l.append(vs_); ps_l.append(bs)
    new_k_prompt = jnp.stack(kp_l, axis=0)
    new_v_prompt = jnp.stack(vp_l, axis=0)
    new_pool_prompt = jnp.stack(pp_l, axis=0)
    new_k_sample = jnp.stack(ks_l, axis=0)
    new_v_sample = jnp.stack(vs_l, axis=0)
    new_pool_sample = jnp.stack(ps_l, axis=0)
    return (xp, xs, new_k_prompt, new_v_prompt, new_pool_prompt,
            new_k_sample, new_v_sample, new_pool_sample)
```

```python
import functools
import math

import jax
import jax.numpy as jnp
from jax import lax
from jax.experimental import pallas as pl
from jax.experimental.pallas import tpu as pltpu

F32 = jnp.float32
BF16 = jnp.bfloat16

D_MODEL = 1024
N_HEADS = 4
HEAD_DIM = 64
V_DIM = 2 * HEAD_DIM
QK_W = N_HEADS * 2 * HEAD_DIM
ATTN_W = N_HEADS * V_DIM
POOL_WINDOWS = (2, 4, 8, 16)
POOL_W = D_MODEL // 2
POOL_GROUP_W = POOL_W // len(POOL_WINDOWS)
POOL_BUF = max(POOL_WINDOWS) - 1
POOL_HALO = max(POOL_WINDOWS)
IN_COLS = 2 * QK_W + ATTN_W + POOL_W + 2 * D_MODEL
N_BUCKETS = 32
MAX_DISTANCE = 128
N_EXPERTS = 8
TOP_K = 2
EPS = 1e-6
LANES = 128
V7X_VMEM_BYTES = 64 * 1024 * 1024
VMEM_LIMIT = V7X_VMEM_BYTES * 7 // 8

ATTN_TILE = 512
ROW_TILE = 512
FFN_ROW_TILE = 1024
MOE_ROW_TILE = 512
MOE_FF_TILE = 896
MOE_DMA_CHUNK = 256

_NT = (((1,), (1,)), ((), ()))


def _lambda_init(layer):
    return 0.8 - 0.6 * math.exp(-0.3 * layer)


def _params(n_axes, vmem=VMEM_LIMIT):
    return pltpu.CompilerParams(dimension_semantics=("arbitrary",) * n_axes,
                                vmem_limit_bytes=vmem)


def _dot(a, b):
    return jnp.dot(a, b, preferred_element_type=F32)


def _rms(x):
    return x * lax.rsqrt(jnp.mean(x * x, axis=-1, keepdims=True) + EPS)


def _silu(x):
    return x * jax.nn.sigmoid(x)


def _const_spec(a):
    nd = a.ndim
    return pl.BlockSpec(a.shape, lambda *_: (0,) * nd)


def _row_spec(tm, width):
    return pl.BlockSpec((1, tm, width), lambda b, i, *_: (b, i, 0))


def _mod_spec(mod, tm, chunk):
    if mod.shape[1] == 1:
        return pl.BlockSpec((1, 1, D_MODEL), lambda b, i, *_: (b, 0, chunk))
    return pl.BlockSpec((1, tm, D_MODEL), lambda b, i, *_: (b, i, chunk))


def _ada_kernel(c_ref, w_ref, b_ref, o_ref):
    a = _silu(c_ref[...]).astype(BF16)
    o_ref[0] = _dot(a, w_ref[0].astype(BF16)) + b_ref[0]


def _ada_mod(c_all, w_ada, b_ada):
    rows = c_all.shape[0]
    depth, _, cols = w_ada.shape
    tn = cols // 4
    return pl.pallas_call(
        _ada_kernel,
        grid=(depth, cols // tn),
        in_specs=[pl.BlockSpec((rows, D_MODEL), lambda l, n: (0, 0)),
                  pl.BlockSpec((1, D_MODEL, tn), lambda l, n: (l, 0, n)),
                  pl.BlockSpec((1, 1, tn), lambda l, n: (l, 0, n))],
        out_specs=pl.BlockSpec((1, rows, tn), lambda l, n: (l, 0, n)),
        out_shape=jax.ShapeDtypeStruct((depth, rows, cols), F32),
        compiler_params=_params(2),
        name="ada_mod",
    )(c_all, w_ada, b_ada.reshape(depth, 1, cols))


def _t5_bucket(rel):
    n = jnp.maximum(rel, 0)
    max_exact = N_BUCKETS // 2
    nf = jnp.maximum(n, 1).astype(F32)
    large = max_exact + (jnp.log(nf / max_exact) / math.log(MAX_DISTANCE / max_exact)
                         * (N_BUCKETS - max_exact)).astype(jnp.int32)
    large = jnp.minimum(large, N_BUCKETS - 1)
    return jnp.where(n < max_exact, n, large)


def _bias_lookup(bucket, rb_ref, h):
    acc = jnp.zeros(bucket.shape, F32)
    for b in range(N_BUCKETS):
        acc = jnp.where(bucket == b, rb_ref[b, h], acc)
    return acc


def _bias_tile_kernel(rb_ref, o_ref, *, t):
    h = pl.program_id(0)
    d = pl.program_id(1)
    r = lax.broadcasted_iota(jnp.int32, (t, t), 0)
    c = lax.broadcasted_iota(jnp.int32, (t, t), 1)
    rel = d * t + r - c
    val = _bias_lookup(_t5_bucket(rel), rb_ref, h)
    o_ref[0, 0] = jnp.where(rel >= 0, val, -jnp.inf)


def _bias_tiles(rel_bias, t):
    return pl.pallas_call(
        functools.partial(_bias_tile_kernel, t=t),
        grid=(N_HEADS, 2),
        in_specs=[pl.BlockSpec(memory_space=pltpu.SMEM)],
        out_specs=pl.BlockSpec((1, 1, t, t), lambda h, d: (h, d, 0, 0)),
        out_shape=jax.ShapeDtypeStruct((N_HEADS, 2, t, t), F32),
        compiler_params=_params(2),
        name="bias_tiles",
    )(rel_bias)


def _bias_row_kernel(rb_ref, o_ref, *, past):
    width = o_ref.shape[1]
    c = lax.broadcasted_iota(jnp.int32, (1, width), 1)
    bucket = _t5_bucket(jnp.maximum(past - c, 0))
    for h in range(N_HEADS):
        o_ref[h:h + 1, :] = _bias_lookup(bucket, rb_ref, h)


def _bias_rows(rel_bias, past):
    return pl.pallas_call(
        functools.partial(_bias_row_kernel, past=past),
        in_specs=[pl.BlockSpec(memory_space=pltpu.SMEM)],
        out_shape=jax.ShapeDtypeStruct((N_HEADS, past + LANES), F32),
        name="bias_rows",
    )(rel_bias)


def _window_means(ext, u, pos):
    tm = u.shape[0]
    outs = []
    for gi, w in enumerate(POOL_WINDOWS):
        ch = slice(gi * POOL_GROUP_W, (gi + 1) * POOL_GROUP_W)
        s = ext[:, ch]
        span = 1
        while span < w:
            s = s[span:] + s[:-span]
            span *= 2
        off = POOL_HALO - (w - 1)
        win = s[off:off + tm]
        cnt = jnp.minimum(pos + 1, w).astype(F32)
        outs.append(win / cnt - u[:, ch])
    return jnp.concatenate(outs, axis=-1)


def _in_proj_kernel(x_ref, sh_ref, sc_ref, gmix_ref, w_ref, wkt_ref, qg_ref, kg_ref, bd_ref,
                    q_ref, kt_ref, v_ref, u_ref, g_ref, *rest, tm, seq_pool):
    h = _rms(x_ref[0]) * gmix_ref[...]
    h = h * (1.0 + sc_ref[0]) + sh_ref[0]
    hb = h.astype(BF16)

    def proj(c0, c1):
        return _dot(hb, w_ref[:, c0:c1])

    zq = proj(0, QK_W)
    msq = _dot((zq * zq).astype(BF16), bd_ref[...])
    q_ref[0] = (zq * lax.rsqrt(msq + EPS) * qg_ref[...]).astype(BF16)

    zk = lax.dot_general(wkt_ref[...], hb, _NT, preferred_element_type=F32)
    zk = zk.reshape(N_HEADS * 2, HEAD_DIM, tm)
    msk = jnp.mean(zk * zk, axis=1, keepdims=True)
    kt_ref[0] = (zk * lax.rsqrt(msk + EPS) * kg_ref[...]).reshape(QK_W, tm)

    c0 = 2 * QK_W
    v_ref[0] = proj(c0, c0 + ATTN_W)
    c0 += ATTN_W
    u = proj(c0, c0 + POOL_W)
    u_ref[0] = u
    c0 += POOL_W
    for c in range(0, 2 * D_MODEL, 512):
        g_ref[0, :, c:c + 512] = jax.nn.sigmoid(proj(c0 + c, c0 + c + 512)).astype(BF16)

    if seq_pool:
        d_ref, carry_ref = rest
        i = pl.program_id(1)

        @pl.when(i == 0)
        def _():
            carry_ref[...] = jnp.zeros_like(carry_ref)

        ext = jnp.concatenate([carry_ref[...], u], axis=0)
        carry_ref[...] = u[tm - POOL_HALO:, :]
        pos = i * tm + lax.broadcasted_iota(jnp.int32, (tm, 1), 0)
        d_ref[0] = _window_means(ext, u, pos).astype(BF16)


def _in_proj(x, mod, gmix, w_bf, wkt_bf, qg, kg, bd, *, tm, seq_pool):
    B, T, _ = x.shape
    nt = T // tm
    out_shape = [jax.ShapeDtypeStruct((B, T, QK_W), BF16),
                 jax.ShapeDtypeStruct((B, QK_W, T), F32),
                 jax.ShapeDtypeStruct((B, T, ATTN_W), F32),
                 jax.ShapeDtypeStruct((B, T, POOL_W), F32),
                 jax.ShapeDtypeStruct((B, T, 2 * D_MODEL), BF16)]
    out_specs = [_row_spec(tm, QK_W),
                 pl.BlockSpec((1, QK_W, tm), lambda b, i: (b, 0, i)),
                 _row_spec(tm, ATTN_W), _row_spec(tm, POOL_W), _row_spec(tm, 2 * D_MODEL)]
    scratch = []
    if seq_pool:
        out_shape.append(jax.ShapeDtypeStruct((B, T, POOL_W), BF16))
        out_specs.append(_row_spec(tm, POOL_W))
        scratch.append(pltpu.VMEM((POOL_HALO, POOL_W), F32))
    return pl.pallas_call(
        functools.partial(_in_proj_kernel, tm=tm, seq_pool=seq_pool),
        grid=(B, nt),
        in_specs=[_row_spec(tm, D_MODEL), _mod_spec(mod, tm, 0), _mod_spec(mod, tm, 1),
                  _const_spec(gmix), _const_spec(w_bf), _const_spec(wkt_bf),
                  _const_spec(qg), _const_spec(kg), _const_spec(bd)],
        out_specs=out_specs,
        out_shape=out_shape,
        scratch_shapes=scratch,
        compiler_params=_params(2),
        name="in_proj",
    )(x, mod, mod, gmix, w_bf, wkt_bf, qg, kg, bd)


def _diff_lambda(lp, layer):
    a = jnp.sum(lp[0:1] * lp[1:2], axis=-1, keepdims=True)
    b = jnp.sum(lp[2:3] * lp[3:4], axis=-1, keepdims=True)
    return jnp.exp(a) - jnp.exp(b) + _lambda_init(layer)


def _attn_kernel(qi_ref, kj_ref, q_ref, kt_ref, v_ref, bias_ref, rb_ref, lp_ref, subln_ref,
                 o_ref, m_ref, l_ref, acc_ref, *, t, layer):
    s_idx = pl.program_id(1)
    i = qi_ref[s_idx]
    j = kj_ref[s_idx]

    @pl.when(j == 0)
    def _():
        m_ref[...] = jnp.full_like(m_ref, -jnp.inf)
        l_ref[...] = jnp.zeros_like(l_ref)
        acc_ref[...] = jnp.zeros_like(acc_ref)

    lane = lax.broadcasted_iota(jnp.int32, (1, V_DIM), 1)
    map0 = jnp.where(lane < HEAD_DIM, 1.0, 0.0).astype(BF16)
    map1 = jnp.where(lane >= HEAD_DIM, 1.0, 0.0).astype(BF16)
    far = (i - j) >= 2
    for h in range(N_HEADS):
        hs = slice(h * V_DIM, (h + 1) * V_DIM)
        q = q_ref[0, :, hs]
        q2 = jnp.concatenate([q * map0, q * map1], axis=0)
        s = _dot(q2, kt_ref[0, hs, :].astype(BF16))
        bias = jnp.where(far, rb_ref[N_BUCKETS - 1, h], bias_ref[h, 0])
        s = (s.reshape(2, t, t) + bias[None]).reshape(2 * t, t)
        m_prev = m_ref[h]
        m_new = jnp.maximum(m_prev, jnp.max(s, axis=-1, keepdims=True))
        alpha = jnp.exp(m_prev - m_new)
        p = jnp.exp(s - m_new)
        l_ref[h] = alpha * l_ref[h] + jnp.sum(p, axis=-1, keepdims=True)
        acc_ref[h] = alpha * acc_ref[h] + _dot(p.astype(BF16), v_ref[0, :, hs].astype(BF16))
        m_ref[h] = m_new

    @pl.when(j == i)
    def _():
        lam = _diff_lambda(lp_ref[...], layer)
        for h in range(N_HEADS):
            o2 = acc_ref[h] / l_ref[h]
            o = o2[:t] - lam * o2[t:]
            o = _rms(o) * subln_ref[...] * (1.0 - _lambda_init(layer))
            o_ref[0, :, h * V_DIM:(h + 1) * V_DIM] = o.astype(BF16)


def _attn_prompt(q, kt, v, bias_tiles, rel_bias, lp, subln, *, layer, t):
    B, T, _ = q.shape
    n = T // t
    pairs = [(i, j) for i in range(n) for j in range(i + 1)]
    qi = jnp.asarray([p[0] for p in pairs], jnp.int32)
    kj = jnp.asarray([p[1] for p in pairs], jnp.int32)
    grid_spec = pltpu.PrefetchScalarGridSpec(
        num_scalar_prefetch=2,
        grid=(B, len(pairs)),
        in_specs=[pl.BlockSpec((1, t, QK_W), lambda b, s, qi, kj: (b, qi[s], 0)),
                  pl.BlockSpec((1, QK_W, t), lambda b, s, qi, kj: (b, 0, kj[s])),
                  pl.BlockSpec((1, t, ATTN_W), lambda b, s, qi, kj: (b, kj[s], 0)),
                  pl.BlockSpec((N_HEADS, 1, t, t),
                               lambda b, s, qi, kj: (0, jnp.minimum(qi[s] - kj[s], 1), 0, 0)),
                  pl.BlockSpec(memory_space=pltpu.SMEM),
                  pl.BlockSpec(lp.shape, lambda *_: (0, 0)),
                  pl.BlockSpec(subln.shape, lambda *_: (0, 0))],
        out_specs=pl.BlockSpec((1, t, ATTN_W), lambda b, s, qi, kj: (b, qi[s], 0)),
        scratch_shapes=[pltpu.VMEM((N_HEADS, 2 * t, 1), F32),
                        pltpu.VMEM((N_HEADS, 2 * t, 1), F32),
                        pltpu.VMEM((N_HEADS, 2 * t, V_DIM), F32)])
    return pl.pallas_call(
        functools.partial(_attn_kernel, t=t, layer=layer),
        grid_spec=grid_spec,
        out_shape=jax.ShapeDtypeStruct((B, T, ATTN_W), BF16),
        compiler_params=_params(2),
        name="attn_prompt",
    )(qi, kj, q, kt, v, bias_tiles, rel_bias, lp, subln)


def _attn_sample_kernel(pt_ref, q_ref, kn_ref, vn_ref, *refs, n_pages, page, layer):
    k_refs = refs[:n_pages]
    v_refs = refs[n_pages:2 * n_pages]
    bias_ref, lp_ref, subln_ref, o_ref = refs[2 * n_pages:]
    rows = 2 * N_HEADS
    row = lax.broadcasted_iota(jnp.int32, (rows, QK_W), 0)
    seg = lax.broadcasted_iota(jnp.int32, (rows, QK_W), 1) // HEAD_DIM
    q = jnp.broadcast_to(q_ref[0].astype(F32), (rows, QK_W))
    qbd_f = jnp.where(seg == (row % N_HEADS) * 2 + row // N_HEADS, q, 0.0)
    qbd = qbd_f.astype(BF16)

    s = jnp.concatenate([_dot(qbd, k_refs[p][0, 0].astype(BF16)) for p in range(n_pages)], axis=1)
    past = n_pages * page
    bias = bias_ref[...]
    bias = jnp.concatenate([bias, bias], axis=0)
    s = s + bias[:, :past]
    s_new = jnp.sum(qbd_f * kn_ref[0], axis=-1, keepdims=True) + bias[:, past:past + 1]
    m = jnp.maximum(jnp.max(s, axis=-1, keepdims=True), s_new)
    p = jnp.exp(s - m)
    p_new = jnp.exp(s_new - m)
    denom = jnp.sum(p, axis=-1, keepdims=True) + p_new
    p = p / denom
    p_new = p_new / denom

    lam = _diff_lambda(lp_ref[...], layer)
    first = lax.broadcasted_iota(jnp.int32, (rows, 1), 0) < N_HEADS
    a = jnp.where(first, p - lam * pltpu.roll(p, N_HEADS, 0), 0.0).astype(BF16)
    a_new = p_new[:N_HEADS] - lam * p_new[N_HEADS:]

    out = jnp.zeros((rows, ATTN_W), F32)
    for pg in range(n_pages):
        vp = jnp.concatenate([v_refs[pg][0, 0, pl.ds(h, page, stride=N_HEADS), :]
                              for h in range(N_HEADS)], axis=1)
        out = out + _dot(a[:, pg * page:(pg + 1) * page], vp.astype(BF16))
    out = out[:N_HEADS] + a_new * vn_ref[0]
    rowh = lax.broadcasted_iota(jnp.int32, (N_HEADS, ATTN_W), 0)
    head = lax.broadcasted_iota(jnp.int32, (N_HEADS, ATTN_W), 1) // V_DIM
    om = jnp.where(rowh == head, out, 0.0)
    ms = jnp.sum(om * om, axis=-1, keepdims=True) / V_DIM
    on = om * lax.rsqrt(ms + EPS)
    o = jnp.sum(on, axis=0, keepdims=True) * subln_ref[...] * (1.0 - _lambda_init(layer))
    o_ref[0] = o.astype(BF16)


def _attn_sample(q, k_new, v_new, cache_kt, cache_vr, page_table, bias_rows, lp, subln4, *, layer):
    DB = q.shape[0]
    n_pages = page_table.shape[1]
    page = cache_kt.shape[3]

    def page_spec(p):
        return pl.BlockSpec((1, 1, QK_W, page), lambda b, pt: (layer, pt[b, p], 0, 0))

    def vpage_spec(p):
        return pl.BlockSpec((1, 1, page * N_HEADS, V_DIM), lambda b, pt: (layer, pt[b, p], 0, 0))

    vec = pl.BlockSpec((1, 1, QK_W), lambda b, pt: (b, 0, 0))
    grid_spec = pltpu.PrefetchScalarGridSpec(
        num_scalar_prefetch=1,
        grid=(DB,),
        in_specs=([vec, vec, vec] + [page_spec(p) for p in range(n_pages)]
                  + [vpage_spec(p) for p in range(n_pages)]
                  + [pl.BlockSpec(bias_rows.shape, lambda *_: (0, 0)),
                     pl.BlockSpec(lp.shape, lambda *_: (0, 0)),
                     pl.BlockSpec(subln4.shape, lambda *_: (0, 0))]),
        out_specs=vec)
    return pl.pallas_call(
        functools.partial(_attn_sample_kernel, n_pages=n_pages, page=page, layer=layer),
        grid_spec=grid_spec,
        out_shape=jax.ShapeDtypeStruct((DB, 1, ATTN_W), BF16),
        compiler_params=_params(1),
        name="attn_sample",
    )(page_table, q, k_new, v_new, *([cache_kt] * n_pages), *([cache_vr] * n_pages),
      bias_rows, lp, subln4)


def _pool_sample_kernel(state_ref, u_ref, d_ref):
    u = u_ref[...]
    outs = []
    for gi, w in enumerate(POOL_WINDOWS):
        ch = slice(gi * POOL_GROUP_W, (gi + 1) * POOL_GROUP_W)
        s = u[:, ch]
        for r in range(POOL_BUF - (w - 1), POOL_BUF):
            s = s + state_ref[r, :, ch]
        outs.append(s / float(w) - u[:, ch])
    d_ref[...] = jnp.concatenate(outs, axis=-1).astype(BF16)


def _pool_sample(state_t, u):
    return pl.pallas_call(
        _pool_sample_kernel,
        out_shape=jax.ShapeDtypeStruct(u.shape, BF16),
        name="pool_sample",
    )(state_t, u)


def _mix_kernel(o_ref, d_ref, g_ref, x_ref, gt_ref, sh_ref, sc_ref, wp_ref, ps_ref, wb_ref, wo_ref,
                gffn_ref, *rest, router):
    d = d_ref[0]
    y = jnp.concatenate([_dot(d[:, g * POOL_GROUP_W:(g + 1) * POOL_GROUP_W], wp_ref[g])
                         for g in range(len(POOL_WINDOWS))], axis=-1)
    y = (y * ps_ref[...]).astype(BF16)
    pa = _dot(o_ref[0], wb_ref[:ATTN_W])
    pb = _dot(y, wb_ref[ATTN_W:])
    g = g_ref[0]
    merged = g[:, :D_MODEL].astype(F32) * pa + g[:, D_MODEL:].astype(F32) * pb
    xn = x_ref[0] + gt_ref[0] * _dot(merged.astype(BF16), wo_ref[...])
    h2 = _rms(xn) * gffn_ref[...]
    h2 = h2 * (1.0 + sc_ref[0]) + sh_ref[0]
    if not router:
        xn_ref, h2_ref = rest
        xn_ref[0] = xn
        h2_ref[0] = h2.astype(h2_ref.dtype)
        return

    wr_hi_ref, wr_lo_ref, xn_ref, h2_ref, ri_ref, rp_ref = rest
    xn_ref[0] = xn
    h2_ref[0] = h2
    hi = h2.astype(BF16)
    lo = (h2 - hi.astype(F32)).astype(BF16)
    logits = _dot(hi, wr_hi_ref[...]) + _dot(lo, wr_hi_ref[...]) + _dot(hi, wr_lo_ref[...])
    lane = lax.broadcasted_iota(jnp.int32, logits.shape, 1)
    lg = jnp.where(lane < N_EXPERTS, logits, -jnp.inf)
    m1 = jnp.max(lg, axis=-1, keepdims=True)
    i1 = jnp.min(jnp.where(lg == m1, lane, LANES), axis=-1, keepdims=True)
    lg = jnp.where(lane == i1, -jnp.inf, lg)
    m2 = jnp.max(lg, axis=-1, keepdims=True)
    i2 = jnp.min(jnp.where(lg == m2, lane, LANES), axis=-1, keepdims=True)
    e = jnp.exp(m2 - m1)
    ri_ref[0] = jnp.concatenate([i1, i2], axis=-1)
    rp_ref[0] = jnp.concatenate([1.0 / (1.0 + e), e / (1.0 + e)], axis=-1)


def _mix(o, d, g, x, mod, wp_bf, pscale, wb_bf, wo_bf, gffn, router_w, *, tm):
    B, T, _ = x.shape
    router = router_w is not None
    ins = [o, d, g, x, mod, mod, mod, wp_bf, pscale, wb_bf, wo_bf, gffn]
    in_specs = [_row_spec(tm, ATTN_W), _row_spec(tm, POOL_W), _row_spec(tm, 2 * D_MODEL),
                _row_spec(tm, D_MODEL), _mod_spec(mod, tm, 2), _mod_spec(mod, tm, 3),
                _mod_spec(mod, tm, 4)] + [_const_spec(a) for a in ins[7:]]
    out_shape = [jax.ShapeDtypeStruct((B, T, D_MODEL), F32),
                 jax.ShapeDtypeStruct((B, T, D_MODEL), F32 if router else BF16)]
    out_specs = [_row_spec(tm, D_MODEL), _row_spec(tm, D_MODEL)]
    if router:
        ins += list(router_w)
        in_specs += [_const_spec(a) for a in router_w]
        out_shape += [jax.ShapeDtypeStruct((B, T, TOP_K), jnp.int32),
                      jax.ShapeDtypeStruct((B, T, TOP_K), F32)]
        out_specs += [_row_spec(tm, TOP_K), _row_spec(tm, TOP_K)]
    return pl.pallas_call(
        functools.partial(_mix_kernel, router=router),
        grid=(B, T // tm),
        in_specs=in_specs,
        out_specs=out_specs,
        out_shape=out_shape,
        compiler_params=_params(2),
        name="mix",
    )(*ins)


def _ffn_kernel(h_ref, x_ref, gt_ref, wg_ref, wu_ref, wd_ref, y_ref, acc_ref):
    f = pl.program_id(2)

    @pl.when(f == 0)
    def _():
        acc_ref[...] = jnp.zeros_like(acc_ref)

    h = h_ref[0]
    a = (_silu(_dot(h, wg_ref[...])) * _dot(h, wu_ref[...])).astype(BF16)
    acc_ref[...] += _dot(a, wd_ref[...])

    @pl.when(f == pl.num_programs(2) - 1)
    def _():
        y_ref[0] = x_ref[0] + gt_ref[0] * acc_ref[...]


def _ffn_dense(h2, x, mod, wg_bf, wu_bf, wd_bf, *, tm):
    B, T, _ = x.shape
    d_ff = wg_bf.shape[1]
    tf = d_ff // 2
    return pl.pallas_call(
        _ffn_kernel,
        grid=(B, T // tm, d_ff // tf),
        in_specs=[_row_spec(tm, D_MODEL), _row_spec(tm, D_MODEL), _mod_spec(mod, tm, 5),
                  pl.BlockSpec((D_MODEL, tf), lambda b, i, f: (0, f)),
                  pl.BlockSpec((D_MODEL, tf), lambda b, i, f: (0, f)),
                  pl.BlockSpec((tf, D_MODEL), lambda b, i, f: (f, 0))],
        out_specs=_row_spec(tm, D_MODEL),
        out_shape=jax.ShapeDtypeStruct((B, T, D_MODEL), F32),
        scratch_shapes=[pltpu.VMEM((tm, D_MODEL), F32)],
        compiler_params=_params(3),
        name="ffn_dense",
    )(h2, x, mod, wg_bf, wu_bf, wd_bf)


def _row_copy(src_hbm, src_row, dst, dst_row, sem):
    return pltpu.make_async_copy(src_hbm.at[pl.ds(src_row, 1), :], dst.at[pl.ds(dst_row, 1), :], sem)


def _dispatch_kernel(pos_ref, h_hbm, xs_in_hbm, xs_hbm, sem, *, tc, rows_per_batch):
    del xs_in_hbm
    base = pl.program_id(0) * tc

    def issue(t, carry):
        for k in range(TOP_K):
            _row_copy(h_hbm, base + t, xs_hbm, pos_ref[0, 0, TOP_K * t + k], sem).start()
        return carry

    lax.fori_loop(0, tc, issue, 0)

    def drain(t, carry):
        for k in range(TOP_K):
            _row_copy(h_hbm, 0, xs_hbm, 0, sem).wait()
        return carry

    lax.fori_loop(0, tc, drain, 0)


def _dispatch(h2_rows, pos, xs, *, tc):
    M = h2_rows.shape[0]
    nchunk = M // tc
    return pl.pallas_call(
        functools.partial(_dispatch_kernel, tc=tc, rows_per_batch=M),
        grid=(nchunk,),
        in_specs=[pl.BlockSpec((1, 1, TOP_K * tc), lambda c: (c, 0, 0), memory_space=pltpu.SMEM),
                  pl.BlockSpec(memory_space=pl.ANY),
                  pl.BlockSpec(memory_space=pl.ANY)],
        out_specs=pl.BlockSpec(memory_space=pl.ANY),
        out_shape=jax.ShapeDtypeStruct(xs.shape, xs.dtype),
        scratch_shapes=[pltpu.SemaphoreType.DMA],
        input_output_aliases={2: 0},
        compiler_params=_params(1),
        name="moe_dispatch",
    )(pos.reshape(nchunk, 1, TOP_K * tc), h2_rows, xs)


def _moe_ffn_kernel(te_ref, nu_ref, xs_ref, wg_ref, wu_ref, wd_ref, y_ref, acc_ref):
    i = pl.program_id(0)
    f = pl.program_id(1)

    @pl.when(i < nu_ref[0])
    def _():
        @pl.when(f == 0)
        def _():
            acc_ref[...] = jnp.zeros_like(acc_ref)

        xb = xs_ref[...].astype(BF16)
        a = (_silu(_dot(xb, wg_ref[0].astype(BF16))) * _dot(xb, wu_ref[0].astype(BF16))).astype(BF16)
        acc_ref[...] += _dot(a, wd_ref[0].astype(BF16))

        @pl.when(f == pl.num_programs(1) - 1)
        def _():
            y_ref[...] = acc_ref[...]

    @pl.when(i >= nu_ref[0])
    def _():
        y_ref[...] = jnp.zeros_like(y_ref)


def _moe_ffn(xs, tile_expert, n_used, wg, wu, wd, *, tm, tf):
    slots = xs.shape[0]
    d_ff = wg.shape[2]
    nf = d_ff // tf

    def tile(i, nu):
        return jnp.minimum(i, nu[0] - 1)

    def ff(i, f, nu):
        return jnp.where(i < nu[0], f, nf - 1)

    grid_spec = pltpu.PrefetchScalarGridSpec(
        num_scalar_prefetch=2,
        grid=(slots // tm, nf),
        in_specs=[pl.BlockSpec((tm, D_MODEL), lambda i, f, te, nu: (tile(i, nu), 0)),
                  pl.BlockSpec((1, D_MODEL, tf), lambda i, f, te, nu: (te[tile(i, nu)], 0, ff(i, f, nu))),
                  pl.BlockSpec((1, D_MODEL, tf), lambda i, f, te, nu: (te[tile(i, nu)], 0, ff(i, f, nu))),
                  pl.BlockSpec((1, tf, D_MODEL), lambda i, f, te, nu: (te[tile(i, nu)], ff(i, f, nu), 0))],
        out_specs=pl.BlockSpec((tm, D_MODEL), lambda i, f, te, nu: (i, 0)),
        scratch_shapes=[pltpu.VMEM((tm, D_MODEL), F32)])
    return pl.pallas_call(
        _moe_ffn_kernel,
        grid_spec=grid_spec,
        out_shape=jax.ShapeDtypeStruct(xs.shape, F32),
        compiler_params=_params(2),
        name="moe_ffn",
    )(tile_expert, n_used, xs, wg, wu, wd)


def _combine_kernel(pos_ref, y_hbm, x_ref, gt_ref, p_ref, o_ref, buf, sem, *, tc):
    def issue(t, carry):
        for k in range(TOP_K):
            _row_copy(y_hbm, pos_ref[0, 0, TOP_K * t + k], buf.at[k], t, sem).start()
        return carry

    lax.fori_loop(0, tc, issue, 0)

    def drain(t, carry):
        for k in range(TOP_K):
            _row_copy(y_hbm, 0, buf.at[0], 0, sem).wait()
        return carry

    lax.fori_loop(0, tc, drain, 0)
    p = p_ref[0]
    f = p[:, 0:1] * buf[0] + p[:, 1:2] * buf[1]
    o_ref[0] = x_ref[0] + gt_ref[0] * f


def _combine(y, pos, probs, x, mod, *, tc):
    B, T, _ = x.shape
    nt = T // tc
    return pl.pallas_call(
        functools.partial(_combine_kernel, tc=tc),
        grid=(B, nt),
        in_specs=[pl.BlockSpec((1, 1, TOP_K * tc), lambda b, i: (b * nt + i, 0, 0),
                               memory_space=pltpu.SMEM),
                  pl.BlockSpec(memory_space=pl.ANY),
                  _row_spec(tc, D_MODEL), _mod_spec(mod, tc, 5), _row_spec(tc, TOP_K)],
        out_specs=_row_spec(tc, D_MODEL),
        out_shape=jax.ShapeDtypeStruct(x.shape, F32),
        scratch_shapes=[pltpu.VMEM((TOP_K, tc, D_MODEL), F32), pltpu.SemaphoreType.DMA],
        compiler_params=_params(2),
        name="moe_combine",
    )(pos.reshape(B * nt, 1, TOP_K * tc), y, x, mod, probs)


def _route_plan(top_i, tm, n_tiles):
    e_flat = top_i.reshape(-1)
    onehot = (e_flat[:, None] == jnp.arange(N_EXPERTS, dtype=jnp.int32)[None, :]).astype(jnp.int32)
    csum = jnp.cumsum(onehot, axis=0)
    cnt = csum[-1]
    gsz = (cnt + tm - 1) // tm * tm
    gend = jnp.cumsum(gsz)
    gstart = gend - gsz
    pos = jnp.sum(onehot * (gstart[None, :] + csum - 1), axis=1)
    n_used = (gend[-1] // tm).astype(jnp.int32).reshape(1)
    tile_start = jnp.arange(n_tiles, dtype=jnp.int32) * tm
    tile_expert = jnp.sum((tile_start[:, None] >= gend[None, :]).astype(jnp.int32), axis=1)
    tile_expert = jnp.minimum(tile_expert, N_EXPERTS - 1).astype(jnp.int32)
    return pos.astype(jnp.int32).reshape(top_i.shape), tile_expert, n_used


def _moe(groups, wg, wu, wd):
    tm = MOE_ROW_TILE
    counts = [g[0].shape[0] * g[0].shape[1] for g in groups]
    total = sum(counts) * TOP_K
    n_tiles = (total + N_EXPERTS * (tm - 1)) // tm + 1
    top_all = jnp.concatenate([g[3].reshape(-1, TOP_K) for g in groups], axis=0)
    pos_all, tile_expert, n_used = _route_plan(top_all, tm, n_tiles)
    xs = jnp.zeros((n_tiles * tm, D_MODEL), F32)
    offs = 0
    pos_groups = []
    for (h2, x, mod, top_i, probs), cnt in zip(groups, counts):
        pos = pos_all[offs:offs + cnt]
        offs += cnt
        pos_groups.append(pos)
        xs = _dispatch(h2.reshape(cnt, D_MODEL), pos, xs, tc=min(MOE_DMA_CHUNK, cnt))
    y = _moe_ffn(xs, tile_expert, n_used, wg, wu, wd, tm=tm, tf=MOE_FF_TILE)
    outs = []
    for (h2, x, mod, top_i, probs), pos in zip(groups, pos_groups):
        T = x.shape[1]
        outs.append(_combine(y, pos.reshape(top_i.shape), probs, x, mod, tc=min(MOE_DMA_CHUNK, T)))
    return outs


def kernel(x_prompt, x_sample, c_prompt, c_sample, cache_k, cache_v, state_pool, page_table,
           rel_bias, w_ada, b_ada, g_mix, g_ffn, w_in, q_norm, k_norm, diff_lambda, subln,
           w_pool, pool_scale, w_branch, w_out, w_ff_gate, w_ff_up, w_ff_down, w_router,
           w_exp_gate, w_exp_up, w_exp_down):
    B, S, D = x_prompt.shape
    DB = x_sample.shape[0]
    depth = w_in.shape[0]
    n_phys, page = cache_k.shape[1], cache_k.shape[2]
    past = page_table.shape[1] * page

    mod_all = _ada_mod(jnp.concatenate([c_prompt, c_sample], axis=0), w_ada, b_ada)
    bias_tiles = _bias_tiles(rel_bias, ATTN_TILE)
    bias_rows = _bias_rows(rel_bias, past)

    cache_kt = jnp.transpose(cache_k, (0, 1, 3, 4, 5, 2)).reshape(depth, n_phys, QK_W, page)
    cache_vr = cache_v.reshape(depth, n_phys, page * N_HEADS, V_DIM)
    state_t = jnp.transpose(state_pool, (0, 2, 1, 3))

    bd = jnp.kron(jnp.eye(QK_W // HEAD_DIM, dtype=F32),
                  jnp.full((HEAD_DIM, HEAD_DIM), 1.0 / HEAD_DIM, F32)).astype(BF16)

    xp = x_prompt
    xs = x_sample.reshape(1, DB, D)
    outs = {k: [] for k in ("kp", "vp", "pp", "ks", "vs", "ps")}
    for l in range(depth):
        mod_p = mod_all[l, :B].reshape(B, 1, 6 * D)
        mod_s = mod_all[l, B:].reshape(1, DB, 6 * D)
        w_bf = w_in[l].astype(BF16)
        wkt_bf = w_in[l][:, QK_W:2 * QK_W].T.astype(BF16)
        qg = jnp.tile(q_norm[l].reshape(1, 2 * HEAD_DIM), (1, N_HEADS)) * (HEAD_DIM ** -0.5)
        kg = jnp.tile(k_norm[l], (N_HEADS, 1)).reshape(N_HEADS * 2, HEAD_DIM, 1)
        gmix = g_mix[l].reshape(1, D)
        gffn = g_ffn[l].reshape(1, D)
        lp = diff_lambda[l]
        sub1 = subln[l].reshape(1, V_DIM)
        sub4 = jnp.tile(sub1, (1, N_HEADS))
        wp_bf = w_pool[l].astype(BF16)
        pscale = pool_scale[l].reshape(1, POOL_W)
        wb_bf = w_branch[l].astype(BF16)
        wo_bf = w_out[l].astype(BF16)
        moe = l % 2 == 1
        j = l // 2
        router_w = None
        if moe:
            wr = jnp.pad(w_router[j], ((0, 0), (0, LANES - N_EXPERTS)))
            wr_hi = wr.astype(BF16)
            router_w = (wr_hi, (wr - wr_hi.astype(F32)).astype(BF16))

        qp, ktp, vp, up, gp, dp = _in_proj(xp, mod_p, gmix, w_bf, wkt_bf, qg, kg, bd,
                                           tm=ROW_TILE, seq_pool=True)
        op = _attn_prompt(qp, ktp, vp, bias_tiles, rel_bias, lp, sub1, layer=l, t=ATTN_TILE)
        mix_p = _mix(op, dp, gp, xp, mod_p, wp_bf, pscale, wb_bf, wo_bf, gffn, router_w, tm=ROW_TILE)

        qs, kts, vs, us, gs = _in_proj(xs, mod_s, gmix, w_bf, wkt_bf, qg, kg, bd,
                                       tm=DB, seq_pool=False)
        ks_rows = jnp.transpose(kts, (0, 2, 1)).reshape(DB, 1, QK_W)
        osamp = _attn_sample(qs.reshape(DB, 1, QK_W), ks_rows, vs.reshape(DB, 1, ATTN_W),
                             cache_kt, cache_vr, page_table, bias_rows, lp, sub4, layer=l)
        ds = _pool_sample(state_t[l], us[0])
        mix_s = _mix(osamp.reshape(1, DB, ATTN_W), ds.reshape(1, DB, POOL_W), gs, xs, mod_s,
                     wp_bf, pscale, wb_bf, wo_bf, gffn, router_w, tm=DB)

        if moe:
            xp, xs = _moe([(mix_p[1], mix_p[0], mod_p, mix_p[2], mix_p[3]),
                           (mix_s[1], mix_s[0], mod_s, mix_s[2], mix_s[3])],
                          w_exp_gate[j], w_exp_up[j], w_exp_down[j])
        else:
            wg_bf = w_ff_gate[j].astype(BF16)
            wu_bf = w_ff_up[j].astype(BF16)
            wd_bf = w_ff_down[j].astype(BF16)
            xp = _ffn_dense(mix_p[1], mix_p[0], mod_p, wg_bf, wu_bf, wd_bf, tm=FFN_ROW_TILE)
            xs = _ffn_dense(mix_s[1], mix_s[0], mod_s, wg_bf, wu_bf, wd_bf, tm=DB)

        outs["kp"].append(jnp.transpose(ktp.reshape(B, N_HEADS, 2, HEAD_DIM, S), (0, 4, 1, 2, 3)))
        outs["vp"].append(vp.reshape(B, S, N_HEADS, V_DIM))
        outs["pp"].append(up[:, S - POOL_BUF:, :])
        outs["ks"].append(ks_rows.reshape(DB, 1, N_HEADS, 2, HEAD_DIM))
        outs["vs"].append(vs.reshape(DB, 1, N_HEADS, V_DIM))
        outs["ps"].append(jnp.concatenate([state_pool[l][:, 1:], us.reshape(DB, 1, POOL_W)], axis=1))

    return (xp, xs.reshape(DB, 1, D),
            jnp.stack(outs["kp"]), jnp.stack(outs["vp"]), jnp.stack(outs["pp"]),
            jnp.stack(outs["ks"]), jnp.stack(outs["vs"]), jnp.stack(outs["ps"]))
```

```python
import functools
import math

import jax
import jax.numpy as jnp
from jax import lax
from jax.experimental import pallas as pl
from jax.experimental.pallas import tpu as pltpu

F32 = jnp.float32
BF16 = jnp.bfloat16

D_MODEL = 1024
N_HEADS = 4
HEAD_DIM = 64
V_DIM = 2 * HEAD_DIM
QK_W = N_HEADS * 2 * HEAD_DIM
ATTN_W = N_HEADS * V_DIM
POOL_WINDOWS = (2, 4, 8, 16)
POOL_W = D_MODEL // 2
POOL_GROUP_W = POOL_W // len(POOL_WINDOWS)
POOL_BUF = max(POOL_WINDOWS) - 1
POOL_HALO = max(POOL_WINDOWS)
IN_COLS = 2 * QK_W + ATTN_W + POOL_W + 2 * D_MODEL
N_BUCKETS = 32
MAX_DISTANCE = 128
N_EXPERTS = 8
TOP_K = 2
EPS = 1e-6
LANES = 128
V7X_VMEM_BYTES = 64 * 1024 * 1024
VMEM_LIMIT = V7X_VMEM_BYTES * 7 // 8

ATTN_TILE = 512
ROW_TILE = 512
FFN_ROW_TILE = 1024
MOE_ROW_TILE = 512
MOE_FF_TILE = 896
MOE_DMA_CHUNK = 256

_NT = (((1,), (1,)), ((), ()))


def _lambda_init(layer):
    return 0.8 - 0.6 * math.exp(-0.3 * layer)


def _params(n_axes, vmem=VMEM_LIMIT):
    return pltpu.CompilerParams(dimension_semantics=("arbitrary",) * n_axes,
                                vmem_limit_bytes=vmem)


def _dot(a, b):
    return jnp.dot(a, b, preferred_element_type=F32)


def _rms(x):
    return x * lax.rsqrt(jnp.mean(x * x, axis=-1, keepdims=True) + EPS)


def _silu(x):
    return x * jax.nn.sigmoid(x)


def _const_spec(a):
    nd = a.ndim
    return pl.BlockSpec(a.shape, lambda *_: (0,) * nd)


def _row_spec(tm, width):
    return pl.BlockSpec((1, tm, width), lambda b, i, *_: (b, i, 0))


def _mod_spec(mod, tm, chunk):
    if mod.shape[1] == 1:
        return pl.BlockSpec((1, 1, D_MODEL), lambda b, i, *_: (b, 0, chunk))
    return pl.BlockSpec((1, tm, D_MODEL), lambda b, i, *_: (b, i, chunk))


def _ada_kernel(c_ref, w_ref, b_ref, o_ref):
    a = _silu(c_ref[...]).astype(BF16)
    o_ref[0] = _dot(a, w_ref[0].astype(BF16)) + b_ref[0]


def _ada_mod(c_all, w_ada, b_ada):
    rows = c_all.shape[0]
    depth, _, cols = w_ada.shape
    tn = cols // 4
    return pl.pallas_call(
        _ada_kernel,
        grid=(depth, cols // tn),
        in_specs=[pl.BlockSpec((rows, D_MODEL), lambda l, n: (0, 0)),
                  pl.BlockSpec((1, D_MODEL, tn), lambda l, n: (l, 0, n)),
                  pl.BlockSpec((1, 1, tn), lambda l, n: (l, 0, n))],
        out_specs=pl.BlockSpec((1, rows, tn), lambda l, n: (l, 0, n)),
        out_shape=jax.ShapeDtypeStruct((depth, rows, cols), F32),
        compiler_params=_params(2),
        name="ada_mod",
    )(c_all, w_ada, b_ada.reshape(depth, 1, cols))


def _t5_bucket(rel):
    n = jnp.maximum(rel, 0)
    max_exact = N_BUCKETS // 2
    nf = jnp.maximum(n, 1).astype(F32)
    large = max_exact + (jnp.log(nf / max_exact) / math.log(MAX_DISTANCE / max_exact)
                         * (N_BUCKETS - max_exact)).astype(jnp.int32)
    large = jnp.minimum(large, N_BUCKETS - 1)
    return jnp.where(n < max_exact, n, large)


def _bias_lookup(bucket, rb_ref, h):
    acc = jnp.zeros(bucket.shape, F32)
    for b in range(N_BUCKETS):
        acc = jnp.where(bucket == b, rb_ref[b, h], acc)
    return acc


def _bias_tile_kernel(rb_ref, o_ref, *, t):
    h = pl.program_id(0)
    d = pl.program_id(1)
    r = lax.broadcasted_iota(jnp.int32, (t, t), 0)
    c = lax.broadcasted_iota(jnp.int32, (t, t), 1)
    rel = d * t + r - c
    val = _bias_lookup(_t5_bucket(rel), rb_ref, h)
    o_ref[0, 0] = jnp.where(rel >= 0, val, -jnp.inf)


def _bias_tiles(rel_bias, t):
    return pl.pallas_call(
        functools.partial(_bias_tile_kernel, t=t),
        grid=(N_HEADS, 2),
        in_specs=[pl.BlockSpec(memory_space=pltpu.SMEM)],
        out_specs=pl.BlockSpec((1, 1, t, t), lambda h, d: (h, d, 0, 0)),
        out_shape=jax.ShapeDtypeStruct((N_HEADS, 2, t, t), F32),
        compiler_params=_params(2),
        name="bias_tiles",
    )(rel_bias)


def _bias_row_kernel(rb_ref, o_ref, *, past):
    width = o_ref.shape[1]
    c = lax.broadcasted_iota(jnp.int32, (1, width), 1)
    bucket = _t5_bucket(jnp.maximum(past - c, 0))
    for h in range(N_HEADS):
        o_ref[h:h + 1, :] = _bias_lookup(bucket, rb_ref, h)


def _bias_rows(rel_bias, past):
    return pl.pallas_call(
        functools.partial(_bias_row_kernel, past=past),
        in_specs=[pl.BlockSpec(memory_space=pltpu.SMEM)],
        out_shape=jax.ShapeDtypeStruct((N_HEADS, past + LANES), F32),
        name="bias_rows",
    )(rel_bias)


def _window_means(ext, u, pos):
    tm = u.shape[0]
    outs = []
    for gi, w in enumerate(POOL_WINDOWS):
        ch = slice(gi * POOL_GROUP_W, (gi + 1) * POOL_GROUP_W)
        s = ext[:, ch]
        span = 1
        while span < w:
            s = s[span:] + s[:-span]
            span *= 2
        off = POOL_HALO - (w - 1)
        win = s[off:off + tm]
        cnt = jnp.minimum(pos + 1, w).astype(F32)
        outs.append(win / cnt - u[:, ch])
    return jnp.concatenate(outs, axis=-1)


def _in_proj_kernel(x_ref, sh_ref, sc_ref, gmix_ref, w_ref, wkt_ref, qg_ref, kg_ref, bd_ref,
                    q_ref, kt_ref, v_ref, u_ref, g_ref, *rest, tm, seq_pool):
    h = _rms(x_ref[0]) * gmix_ref[...]
    h = h * (1.0 + sc_ref[0]) + sh_ref[0]
    hb = h.astype(BF16)

    def proj(c0, c1):
        return _dot(hb, w_ref[:, c0:c1])

    zq = proj(0, QK_W)
    msq = _dot((zq * zq).astype(BF16), bd_ref[...])
    q_ref[0] = (zq * lax.rsqrt(msq + EPS) * qg_ref[...]).astype(BF16)

    zk = lax.dot_general(wkt_ref[...], hb, _NT, preferred_element_type=F32)
    zk = zk.reshape(N_HEADS * 2, HEAD_DIM, tm)
    msk = jnp.mean(zk * zk, axis=1, keepdims=True)
    kt_ref[0] = (zk * lax.rsqrt(msk + EPS) * kg_ref[...]).reshape(QK_W, tm)

    c0 = 2 * QK_W
    v_ref[0] = proj(c0, c0 + ATTN_W)
    c0 += ATTN_W
    u = proj(c0, c0 + POOL_W)
    u_ref[0] = u
    c0 += POOL_W
    for c in range(0, 2 * D_MODEL, 512):
        g_ref[0, :, c:c + 512] = jax.nn.sigmoid(proj(c0 + c, c0 + c + 512)).astype(BF16)

    if seq_pool:
        d_ref, carry_ref = rest
        i = pl.program_id(1)

        @pl.when(i == 0)
        def _():
            carry_ref[...] = jnp.zeros_like(carry_ref)

        ext = jnp.concatenate([carry_ref[...], u], axis=0)
        carry_ref[...] = u[tm - POOL_HALO:, :]
        pos = i * tm + lax.broadcasted_iota(jnp.int32, (tm, 1), 0)
        d_ref[0] = _window_means(ext, u, pos).astype(BF16)


def _in_proj(x, mod, gmix, w_bf, wkt_bf, qg, kg, bd, *, tm, seq_pool):
    B, T, _ = x.shape
    nt = T // tm
    out_shape = [jax.ShapeDtypeStruct((B, T, QK_W), BF16),
                 jax.ShapeDtypeStruct((B, QK_W, T), F32),
                 jax.ShapeDtypeStruct((B, T, ATTN_W), F32),
                 jax.ShapeDtypeStruct((B, T, POOL_W), F32),
                 jax.ShapeDtypeStruct((B, T, 2 * D_MODEL), BF16)]
    out_specs = [_row_spec(tm, QK_W),
                 pl.BlockSpec((1, QK_W, tm), lambda b, i: (b, 0, i)),
                 _row_spec(tm, ATTN_W), _row_spec(tm, POOL_W), _row_spec(tm, 2 * D_MODEL)]
    scratch = []
    if seq_pool:
        out_shape.append(jax.ShapeDtypeStruct((B, T, POOL_W), BF16))
        out_specs.append(_row_spec(tm, POOL_W))
        scratch.append(pltpu.VMEM((POOL_HALO, POOL_W), F32))
    return pl.pallas_call(
        functools.partial(_in_proj_kernel, tm=tm, seq_pool=seq_pool),
        grid=(B, nt),
        in_specs=[_row_spec(tm, D_MODEL), _mod_spec(mod, tm, 0), _mod_spec(mod, tm, 1),
                  _const_spec(gmix), _const_spec(w_bf), _const_spec(wkt_bf),
                  _const_spec(qg), _const_spec(kg), _const_spec(bd)],
        out_specs=out_specs,
        out_shape=out_shape,
        scratch_shapes=scratch,
        compiler_params=_params(2),
        name="in_proj",
    )(x, mod, mod, gmix, w_bf, wkt_bf, qg, kg, bd)


def _diff_lambda(lp, layer):
    a = jnp.sum(lp[0:1] * lp[1:2], axis=-1, keepdims=True)
    b = jnp.sum(lp[2:3] * lp[3:4], axis=-1, keepdims=True)
    return jnp.exp(a) - jnp.exp(b) + _lambda_init(layer)


def _attn_kernel(qi_ref, kj_ref, q_ref, kt_ref, v_ref, bias_ref, rb_ref, lp_ref, subln_ref,
                 o_ref, m_ref, l_ref, acc_ref, *, t, layer):
    s_idx = pl.program_id(1)
    i = qi_ref[s_idx]
    j = kj_ref[s_idx]

    @pl.when(j == 0)
    def _():
        m_ref[...] = jnp.full_like(m_ref, -jnp.inf)
        l_ref[...] = jnp.zeros_like(l_ref)
        acc_ref[...] = jnp.zeros_like(acc_ref)

    lane = lax.broadcasted_iota(jnp.int32, (1, V_DIM), 1)
    map0 = jnp.where(lane < HEAD_DIM, 1.0, 0.0).astype(BF16)
    map1 = jnp.where(lane >= HEAD_DIM, 1.0, 0.0).astype(BF16)
    far = (i - j) >= 2
    for h in range(N_HEADS):
        hs = slice(h * V_DIM, (h + 1) * V_DIM)
        q = q_ref[0, :, hs]
        q2 = jnp.concatenate([q * map0, q * map1], axis=0)
        s = _dot(q2, kt_ref[0, hs, :].astype(BF16))
        bias = jnp.where(far, rb_ref[N_BUCKETS - 1, h], bias_ref[h, 0])
        s = (s.reshape(2, t, t) + bias[None]).reshape(2 * t, t)
        m_prev = m_ref[h]
        m_new = jnp.maximum(m_prev, jnp.max(s, axis=-1, keepdims=True))
        alpha = jnp.exp(m_prev - m_new)
        p = jnp.exp(s - m_new)
        l_ref[h] = alpha * l_ref[h] + jnp.sum(p, axis=-1, keepdims=True)
        acc_ref[h] = alpha * acc_ref[h] + _dot(p.astype(BF16), v_ref[0, :, hs].astype(BF16))
        m_ref[h] = m_new

    @pl.when(j == i)
    def _():
        lam = _diff_lambda(lp_ref[...], layer)
        for h in range(N_HEADS):
            o2 = acc_ref[h] / l_ref[h]
            o = o2[:t] - lam * o2[t:]
            o = _rms(o) * subln_ref[...] * (1.0 - _lambda_init(layer))
            o_ref[0, :, h * V_DIM:(h + 1) * V_DIM] = o.astype(BF16)


def _attn_prompt(q, kt, v, bias_tiles, rel_bias, lp, subln, *, layer, t):
    B, T, _ = q.shape
    n = T // t
    pairs = [(i, j) for i in range(n) for j in range(i + 1)]
    qi = jnp.asarray([p[0] for p in pairs], jnp.int32)
    kj = jnp.asarray([p[1] for p in pairs], jnp.int32)
    grid_spec = pltpu.PrefetchScalarGridSpec(
        num_scalar_prefetch=2,
        grid=(B, len(pairs)),
        in_specs=[pl.BlockSpec((1, t, QK_W), lambda b, s, qi, kj: (b, qi[s], 0)),
                  pl.BlockSpec((1, QK_W, t), lambda b, s, qi, kj: (b, 0, kj[s])),
                  pl.BlockSpec((1, t, ATTN_W), lambda b, s, qi, kj: (b, kj[s], 0)),
                  pl.BlockSpec((N_HEADS, 1, t, t),
                               lambda b, s, qi, kj: (0, jnp.minimum(qi[s] - kj[s], 1), 0, 0)),
                  pl.BlockSpec(memory_space=pltpu.SMEM),
                  pl.BlockSpec(lp.shape, lambda *_: (0, 0)),
                  pl.BlockSpec(subln.shape, lambda *_: (0, 0))],
        out_specs=pl.BlockSpec((1, t, ATTN_W), lambda b, s, qi, kj: (b, qi[s], 0)),
        scratch_shapes=[pltpu.VMEM((N_HEADS, 2 * t, 1), F32),
                        pltpu.VMEM((N_HEADS, 2 * t, 1), F32),
                        pltpu.VMEM((N_HEADS, 2 * t, V_DIM), F32)])
    return pl.pallas_call(
        functools.partial(_attn_kernel, t=t, layer=layer),
        grid_spec=grid_spec,
        out_shape=jax.ShapeDtypeStruct((B, T, ATTN_W), BF16),
        compiler_params=_params(2),
        name="attn_prompt",
    )(qi, kj, q, kt, v, bias_tiles, rel_bias, lp, subln)


def _attn_sample_kernel(pt_ref, q_ref, kn_ref, vn_ref, *refs, n_pages, page, layer):
    k_refs = refs[:n_pages]
    v_refs = refs[n_pages:2 * n_pages]
    bias_ref, lp_ref, subln_ref, o_ref = refs[2 * n_pages:]
    rows = 2 * N_HEADS
    row = lax.broadcasted_iota(jnp.int32, (rows, QK_W), 0)
    seg = lax.broadcasted_iota(jnp.int32, (rows, QK_W), 1) // HEAD_DIM
    q = jnp.broadcast_to(q_ref[0].astype(F32), (rows, QK_W))
    qbd_f = jnp.where(seg == (row % N_HEADS) * 2 + row // N_HEADS, q, 0.0)
    qbd = qbd_f.astype(BF16)

    s = jnp.concatenate([_dot(qbd, k_refs[p][0, 0].astype(BF16)) for p in range(n_pages)], axis=1)
    past = n_pages * page
    bias = bias_ref[...]
    bias = jnp.concatenate([bias, bias], axis=0)
    s = s + bias[:, :past]
    s_new = jnp.sum(qbd_f * kn_ref[0], axis=-1, keepdims=True) + bias[:, past:past + 1]
    m = jnp.maximum(jnp.max(s, axis=-1, keepdims=True), s_new)
    p = jnp.exp(s - m)
    p_new = jnp.exp(s_new - m)
    denom = jnp.sum(p, axis=-1, keepdims=True) + p_new
    p = p / denom
    p_new = p_new / denom

    lam = _diff_lambda(lp_ref[...], layer)
    first = lax.broadcasted_iota(jnp.int32, (rows, 1), 0) < N_HEADS
    a = jnp.where(first, p - lam * pltpu.roll(p, N_HEADS, 0), 0.0).astype(BF16)
    a_new = p_new[:N_HEADS] - lam * p_new[N_HEADS:]

    out = jnp.zeros((rows, ATTN_W), F32)
    for pg in range(n_pages):
        vp = jnp.concatenate([v_refs[pg][0, 0, pl.ds(h, page, stride=N_HEADS), :]
                              for h in range(N_HEADS)], axis=1)
        out = out + _dot(a[:, pg * page:(pg + 1) * page], vp.astype(BF16))
    out = out[:N_HEADS] + a_new * vn_ref[0]
    rowh = lax.broadcasted_iota(jnp.int32, (N_HEADS, ATTN_W), 0)
    head = lax.broadcasted_iota(jnp.int32, (N_HEADS, ATTN_W), 1) // V_DIM
    om = jnp.where(rowh == head, out, 0.0)
    ms = jnp.sum(om * om, axis=-1, keepdims=True) / V_DIM
    on = om * lax.rsqrt(ms + EPS)
    o = jnp.sum(on, axis=0, keepdims=True) * subln_ref[...] * (1.0 - _lambda_init(layer))
    o_ref[0] = o.astype(BF16)


def _attn_sample(q, k_new, v_new, cache_kt, cache_vr, page_table, bias_rows, lp, subln4, *, layer):
    DB = q.shape[0]
    n_pages = page_table.shape[1]
    page = cache_kt.shape[3]

    def page_spec(p):
        return pl.BlockSpec((1, 1, QK_W, page), lambda b, pt: (layer, pt[b, p], 0, 0))

    def vpage_spec(p):
        return pl.BlockSpec((1, 1, page * N_HEADS, V_DIM), lambda b, pt: (layer, pt[b, p], 0, 0))

    vec = pl.BlockSpec((1, 1, QK_W), lambda b, pt: (b, 0, 0))
    grid_spec = pltpu.PrefetchScalarGridSpec(
        num_scalar_prefetch=1,
        grid=(DB,),
        in_specs=([vec, vec, vec] + [page_spec(p) for p in range(n_pages)]
                  + [vpage_spec(p) for p in range(n_pages)]
                  + [pl.BlockSpec(bias_rows.shape, lambda *_: (0, 0)),
                     pl.BlockSpec(lp.shape, lambda *_: (0, 0)),
                     pl.BlockSpec(subln4.shape, lambda *_: (0, 0))]),
        out_specs=vec)
    return pl.pallas_call(
        functools.partial(_attn_sample_kernel, n_pages=n_pages, page=page, layer=layer),
        grid_spec=grid_spec,
        out_shape=jax.ShapeDtypeStruct((DB, 1, ATTN_W), BF16),
        compiler_params=_params(1),
        name="attn_sample",
    )(page_table, q, k_new, v_new, *([cache_kt] * n_pages), *([cache_vr] * n_pages),
      bias_rows, lp, subln4)


def _pool_sample_kernel(state_ref, u_ref, d_ref):
    u = u_ref[...]
    outs = []
    for gi, w in enumerate(POOL_WINDOWS):
        ch = slice(gi * POOL_GROUP_W, (gi + 1) * POOL_GROUP_W)
        s = u[:, ch]
        for r in range(POOL_BUF - (w - 1), POOL_BUF):
            s = s + state_ref[r, :, ch]
        outs.append(s / float(w) - u[:, ch])
    d_ref[...] = jnp.concatenate(outs, axis=-1).astype(BF16)


def _pool_sample(state_t, u):
    return pl.pallas_call(
        _pool_sample_kernel,
        out_shape=jax.ShapeDtypeStruct(u.shape, BF16),
        name="pool_sample",
    )(state_t, u)


def _mix_kernel(o_ref, d_ref, g_ref, x_ref, gt_ref, sh_ref, sc_ref, wp_ref, ps_ref, wb_ref, wo_ref,
                gffn_ref, *rest, router):
    d = d_ref[0]
    y = jnp.concatenate([_dot(d[:, g * POOL_GROUP_W:(g + 1) * POOL_GROUP_W], wp_ref[g])
                         for g in range(len(POOL_WINDOWS))], axis=-1)
    y = (y * ps_ref[...]).astype(BF16)
    pa = _dot(o_ref[0], wb_ref[:ATTN_W])
    pb = _dot(y, wb_ref[ATTN_W:])
    g = g_ref[0]
    merged = g[:, :D_MODEL].astype(F32) * pa + g[:, D_MODEL:].astype(F32) * pb
    xn = x_ref[0] + gt_ref[0] * _dot(merged.astype(BF16), wo_ref[...])
    h2 = _rms(xn) * gffn_ref[...]
    h2 = h2 * (1.0 + sc_ref[0]) + sh_ref[0]
    if not router:
        xn_ref, h2_ref = rest
        xn_ref[0] = xn
        h2_ref[0] = h2.astype(h2_ref.dtype)
        return

    wr_hi_ref, wr_lo_ref, xn_ref, h2_ref, ri_ref, rp_ref = rest
    xn_ref[0] = xn
    h2_ref[0] = h2
    hi = h2.astype(BF16)
    lo = (h2 - hi.astype(F32)).astype(BF16)
    logits = _dot(hi, wr_hi_ref[...]) + _dot(lo, wr_hi_ref[...]) + _dot(hi, wr_lo_ref[...])
    lane = lax.broadcasted_iota(jnp.int32, logits.shape, 1)
    lg = jnp.where(lane < N_EXPERTS, logits, -jnp.inf)
    m1 = jnp.max(lg, axis=-1, keepdims=True)
    i1 = jnp.min(jnp.where(lg == m1, lane, LANES), axis=-1, keepdims=True)
    lg = jnp.where(lane == i1, -jnp.inf, lg)
    m2 = jnp.max(lg, axis=-1, keepdims=True)
    i2 = jnp.min(jnp.where(lg == m2, lane, LANES), axis=-1, keepdims=True)
    e = jnp.exp(m2 - m1)
    ri_ref[0] = jnp.concatenate([i1, i2], axis=-1)
    rp_ref[0] = jnp.concatenate([1.0 / (1.0 + e), e / (1.0 + e)], axis=-1)


def _mix(o, d, g, x, mod, wp_bf, pscale, wb_bf, wo_bf, gffn, router_w, *, tm):
    B, T, _ = x.shape
    router = router_w is not None
    ins = [o, d, g, x, mod, mod, mod, wp_bf, pscale, wb_bf, wo_bf, gffn]
    in_specs = [_row_spec(tm, ATTN_W), _row_spec(tm, POOL_W), _row_spec(tm, 2 * D_MODEL),
                _row_spec(tm, D_MODEL), _mod_spec(mod, tm, 2), _mod_spec(mod, tm, 3),
                _mod_spec(mod, tm, 4)] + [_const_spec(a) for a in ins[7:]]
    out_shape = [jax.ShapeDtypeStruct((B, T, D_MODEL), F32),
                 jax.ShapeDtypeStruct((B, T, D_MODEL), F32 if router else BF16)]
    out_specs = [_row_spec(tm, D_MODEL), _row_spec(tm, D_MODEL)]
    if router:
        ins += list(router_w)
        in_specs += [_const_spec(a) for a in router_w]
        out_shape += [jax.ShapeDtypeStruct((B, T, TOP_K), jnp.int32),
                      jax.ShapeDtypeStruct((B, T, TOP_K), F32)]
        out_specs += [_row_spec(tm, TOP_K), _row_spec(tm, TOP_K)]
    return pl.pallas_call(
        functools.partial(_mix_kernel, router=router),
        grid=(B, T // tm),
        in_specs=in_specs,
        out_specs=out_specs,
        out_shape=out_shape,
        compiler_params=_params(2),
        name="mix",
    )(*ins)


def _ffn_kernel(h_ref, x_ref, gt_ref, wg_ref, wu_ref, wd_ref, y_ref, acc_ref):
    f = pl.program_id(2)

    @pl.when(f == 0)
    def _():
        acc_ref[...] = jnp.zeros_like(acc_ref)

    h = h_ref[0]
    a = (_silu(_dot(h, wg_ref[...])) * _dot(h, wu_ref[...])).astype(BF16)
    acc_ref[...] += _dot(a, wd_ref[...])

    @pl.when(f == pl.num_programs(2) - 1)
    def _():
        y_ref[0] = x_ref[0] + gt_ref[0] * acc_ref[...]


def _ffn_dense(h2, x, mod, wg_bf, wu_bf, wd_bf, *, tm):
    B, T, _ = x.shape
    d_ff = wg_bf.shape[1]
    tf = d_ff // 2
    return pl.pallas_call(
        _ffn_kernel,
        grid=(B, T // tm, d_ff // tf),
        in_specs=[_row_spec(tm, D_MODEL), _row_spec(tm, D_MODEL), _mod_spec(mod, tm, 5),
                  pl.BlockSpec((D_MODEL, tf), lambda b, i, f: (0, f)),
                  pl.BlockSpec((D_MODEL, tf), lambda b, i, f: (0, f)),
                  pl.BlockSpec((tf, D_MODEL), lambda b, i, f: (f, 0))],
        out_specs=_row_spec(tm, D_MODEL),
        out_shape=jax.ShapeDtypeStruct((B, T, D_MODEL), F32),
        scratch_shapes=[pltpu.VMEM((tm, D_MODEL), F32)],
        compiler_params=_params(3),
        name="ffn_dense",
    )(h2, x, mod, wg_bf, wu_bf, wd_bf)


def _row_copy(src_hbm, src_row, dst, dst_row, sem):
    return pltpu.make_async_copy(src_hbm.at[pl.ds(src_row, 1), :], dst.at[pl.ds(dst_row, 1), :], sem)


def _moe_ffn_kernel(te_ref, nu_ref, src_ref, src_next_ref, h_hbm, wg_ref, wu_ref, wd_ref, y_ref,
                    xbuf, acc_ref, sem, *, tm):
    i = pl.program_id(0)
    f = pl.program_id(1)
    nu = nu_ref[0]
    slot = i % 2

    def issue(idx_ref, dst):
        def body(r, carry):
            _row_copy(h_hbm, idx_ref[0, 0, r], xbuf.at[dst], r, sem.at[dst]).start()
            return carry
        lax.fori_loop(0, tm, body, 0, unroll=8)

    def drain(dst):
        def body(r, carry):
            _row_copy(h_hbm, 0, xbuf.at[dst], 0, sem.at[dst]).wait()
            return carry
        lax.fori_loop(0, tm, body, 0, unroll=8)

    @pl.when(i < nu)
    def _():
        @pl.when(f == 0)
        def _():
            @pl.when(i == 0)
            def _():
                issue(src_ref, 0)

            drain(slot)

            @pl.when(i + 1 < nu)
            def _():
                issue(src_next_ref, 1 - slot)

            acc_ref[...] = jnp.zeros_like(acc_ref)

        xb = xbuf[slot].astype(BF16)
        a = (_silu(_dot(xb, wg_ref[0].astype(BF16))) * _dot(xb, wu_ref[0].astype(BF16))).astype(BF16)
        acc_ref[...] += _dot(a, wd_ref[0].astype(BF16))

        @pl.when(f == pl.num_programs(1) - 1)
        def _():
            y_ref[...] = acc_ref[...]

    @pl.when(i >= nu)
    def _():
        y_ref[...] = jnp.zeros_like(y_ref)


def _moe_ffn(h_rows, src, tile_expert, n_used, wg, wu, wd, *, tm, tf):
    n_tiles = src.shape[0]
    d_ff = wg.shape[2]
    nf = d_ff // tf

    def tile(i, nu):
        return jnp.minimum(i, nu[0] - 1)

    def ff(i, f, nu):
        return jnp.where(i < nu[0], f, nf - 1)

    grid_spec = pltpu.PrefetchScalarGridSpec(
        num_scalar_prefetch=2,
        grid=(n_tiles, nf),
        in_specs=[pl.BlockSpec((1, 1, tm), lambda i, f, te, nu: (i, 0, 0), memory_space=pltpu.SMEM),
                  pl.BlockSpec((1, 1, tm), lambda i, f, te, nu: (jnp.minimum(i + 1, n_tiles - 1), 0, 0),
                               memory_space=pltpu.SMEM),
                  pl.BlockSpec(memory_space=pl.ANY),
                  pl.BlockSpec((1, D_MODEL, tf), lambda i, f, te, nu: (te[tile(i, nu)], 0, ff(i, f, nu))),
                  pl.BlockSpec((1, D_MODEL, tf), lambda i, f, te, nu: (te[tile(i, nu)], 0, ff(i, f, nu))),
                  pl.BlockSpec((1, tf, D_MODEL), lambda i, f, te, nu: (te[tile(i, nu)], ff(i, f, nu), 0))],
        out_specs=pl.BlockSpec((tm, D_MODEL), lambda i, f, te, nu: (i, 0)),
        scratch_shapes=[pltpu.VMEM((2, tm, D_MODEL), F32), pltpu.VMEM((tm, D_MODEL), F32),
                        pltpu.SemaphoreType.DMA((2,))])
    return pl.pallas_call(
        functools.partial(_moe_ffn_kernel, tm=tm),
        grid_spec=grid_spec,
        out_shape=jax.ShapeDtypeStruct((n_tiles * tm, D_MODEL), F32),
        compiler_params=_params(2),
        name="moe_ffn",
    )(tile_expert, n_used, src, src, h_rows, wg, wu, wd)


def _combine_kernel(pos_ref, y_hbm, x_ref, gt_ref, p_ref, o_ref, buf, sem, *, tc):
    def issue(t, carry):
        for k in range(TOP_K):
            _row_copy(y_hbm, pos_ref[0, 0, TOP_K * t + k], buf.at[k], t, sem).start()
        return carry

    lax.fori_loop(0, tc, issue, 0)

    def drain(t, carry):
        for k in range(TOP_K):
            _row_copy(y_hbm, 0, buf.at[0], 0, sem).wait()
        return carry

    lax.fori_loop(0, tc, drain, 0)
    p = p_ref[0]
    f = p[:, 0:1] * buf[0] + p[:, 1:2] * buf[1]
    o_ref[0] = x_ref[0] + gt_ref[0] * f


def _combine(y, pos, probs, x, mod, *, tc):
    B, T, _ = x.shape
    nt = T // tc
    return pl.pallas_call(
        functools.partial(_combine_kernel, tc=tc),
        grid=(B, nt),
        in_specs=[pl.BlockSpec((1, 1, TOP_K * tc), lambda b, i: (b * nt + i, 0, 0),
                               memory_space=pltpu.SMEM),
                  pl.BlockSpec(memory_space=pl.ANY),
                  _row_spec(tc, D_MODEL), _mod_spec(mod, tc, 5), _row_spec(tc, TOP_K)],
        out_specs=_row_spec(tc, D_MODEL),
        out_shape=jax.ShapeDtypeStruct(x.shape, F32),
        scratch_shapes=[pltpu.VMEM((TOP_K, tc, D_MODEL), F32), pltpu.SemaphoreType.DMA],
        compiler_params=_params(2),
        name="moe_combine",
    )(pos.reshape(B * nt, 1, TOP_K * tc), y, x, mod, probs)


def _route_plan(top_i, tm, n_tiles):
    e_flat = top_i.reshape(-1)
    onehot = (e_flat[:, None] == jnp.arange(N_EXPERTS, dtype=jnp.int32)[None, :]).astype(jnp.int32)
    csum = jnp.cumsum(onehot, axis=0)
    cnt = csum[-1]
    gsz = (cnt + tm - 1) // tm * tm
    gend = jnp.cumsum(gsz)
    gstart = gend - gsz
    pos = jnp.sum(onehot * (gstart[None, :] + csum - 1), axis=1).astype(jnp.int32)
    n_used = (gend[-1] // tm).astype(jnp.int32).reshape(1)
    tile_start = jnp.arange(n_tiles, dtype=jnp.int32) * tm
    tile_expert = jnp.sum((tile_start[:, None] >= gend[None, :]).astype(jnp.int32), axis=1)
    tile_expert = jnp.minimum(tile_expert, N_EXPERTS - 1).astype(jnp.int32)
    token = jnp.arange(e_flat.shape[0], dtype=jnp.int32) // TOP_K
    src = jnp.zeros((n_tiles * tm,), jnp.int32).at[pos].set(token, unique_indices=True)
    return pos.reshape(top_i.shape), src.reshape(n_tiles, 1, tm), tile_expert, n_used


def _moe(groups, wg, wu, wd):
    tm = MOE_ROW_TILE
    counts = [g[0].shape[0] * g[0].shape[1] for g in groups]
    total = sum(counts) * TOP_K
    n_tiles = (total + N_EXPERTS * (tm - 1)) // tm + 1
    top_all = jnp.concatenate([g[3].reshape(-1, TOP_K) for g in groups], axis=0)
    h_all = jnp.concatenate([g[0].reshape(-1, D_MODEL) for g in groups], axis=0)
    pos_all, src, tile_expert, n_used = _route_plan(top_all, tm, n_tiles)
    y = _moe_ffn(h_all, src, tile_expert, n_used, wg, wu, wd, tm=tm, tf=MOE_FF_TILE)
    outs = []
    offs = 0
    for (h2, x, mod, top_i, probs), cnt in zip(groups, counts):
        pos = pos_all[offs:offs + cnt].reshape(top_i.shape)
        offs += cnt
        outs.append(_combine(y, pos, probs, x, mod, tc=min(MOE_DMA_CHUNK, x.shape[1])))
    return outs


def kernel(x_prompt, x_sample, c_prompt, c_sample, cache_k, cache_v, state_pool, page_table,
           rel_bias, w_ada, b_ada, g_mix, g_ffn, w_in, q_norm, k_norm, diff_lambda, subln,
           w_pool, pool_scale, w_branch, w_out, w_ff_gate, w_ff_up, w_ff_down, w_router,
           w_exp_gate, w_exp_up, w_exp_down):
    B, S, D = x_prompt.shape
    DB = x_sample.shape[0]
    depth = w_in.shape[0]
    n_phys, page = cache_k.shape[1], cache_k.shape[2]
    past = page_table.shape[1] * page

    mod_all = _ada_mod(jnp.concatenate([c_prompt, c_sample], axis=0), w_ada, b_ada)
    bias_tiles = _bias_tiles(rel_bias, ATTN_TILE)
    bias_rows = _bias_rows(rel_bias, past)

    cache_kt = jnp.transpose(cache_k, (0, 1, 3, 4, 5, 2)).reshape(depth, n_phys, QK_W, page)
    cache_vr = cache_v.reshape(depth, n_phys, page * N_HEADS, V_DIM)
    state_t = jnp.transpose(state_pool, (0, 2, 1, 3))

    bd = jnp.kron(jnp.eye(QK_W // HEAD_DIM, dtype=F32),
                  jnp.full((HEAD_DIM, HEAD_DIM), 1.0 / HEAD_DIM, F32)).astype(BF16)

    xp = x_prompt
    xs = x_sample.reshape(1, DB, D)
    outs = {k: [] for k in ("kp", "vp", "pp", "ks", "vs", "ps")}
    for l in range(depth):
        mod_p = mod_all[l, :B].reshape(B, 1, 6 * D)
        mod_s = mod_all[l, B:].reshape(1, DB, 6 * D)
        w_bf = w_in[l].astype(BF16)
        wkt_bf = w_in[l][:, QK_W:2 * QK_W].T.astype(BF16)
        qg = jnp.tile(q_norm[l].reshape(1, 2 * HEAD_DIM), (1, N_HEADS)) * (HEAD_DIM ** -0.5)
        kg = jnp.tile(k_norm[l], (N_HEADS, 1)).reshape(N_HEADS * 2, HEAD_DIM, 1)
        gmix = g_mix[l].reshape(1, D)
        gffn = g_ffn[l].reshape(1, D)
        lp = diff_lambda[l]
        sub1 = subln[l].reshape(1, V_DIM)
        sub4 = jnp.tile(sub1, (1, N_HEADS))
        wp_bf = w_pool[l].astype(BF16)
        pscale = pool_scale[l].reshape(1, POOL_W)
        wb_bf = w_branch[l].astype(BF16)
        wo_bf = w_out[l].astype(BF16)
        moe = l % 2 == 1
        j = l // 2
        router_w = None
        if moe:
            wr = jnp.pad(w_router[j], ((0, 0), (0, LANES - N_EXPERTS)))
            wr_hi = wr.astype(BF16)
            router_w = (wr_hi, (wr - wr_hi.astype(F32)).astype(BF16))

        qp, ktp, vp, up, gp, dp = _in_proj(xp, mod_p, gmix, w_bf, wkt_bf, qg, kg, bd,
                                           tm=ROW_TILE, seq_pool=True)
        op = _attn_prompt(qp, ktp, vp, bias_tiles, rel_bias, lp, sub1, layer=l, t=ATTN_TILE)
        mix_p = _mix(op, dp, gp, xp, mod_p, wp_bf, pscale, wb_bf, wo_bf, gffn, router_w, tm=ROW_TILE)

        qs, kts, vs, us, gs = _in_proj(xs, mod_s, gmix, w_bf, wkt_bf, qg, kg, bd,
                                       tm=DB, seq_pool=False)
        ks_rows = jnp.transpose(kts, (0, 2, 1)).reshape(DB, 1, QK_W)
        osamp = _attn_sample(qs.reshape(DB, 1, QK_W), ks_rows, vs.reshape(DB, 1, ATTN_W),
                             cache_kt, cache_vr, page_table, bias_rows, lp, sub4, layer=l)
        ds = _pool_sample(state_t[l], us[0])
        mix_s = _mix(osamp.reshape(1, DB, ATTN_W), ds.reshape(1, DB, POOL_W), gs, xs, mod_s,
                     wp_bf, pscale, wb_bf, wo_bf, gffn, router_w, tm=DB)

        if moe:
            xp, xs = _moe([(mix_p[1], mix_p[0], mod_p, mix_p[2], mix_p[3]),
                           (mix_s[1], mix_s[0], mod_s, mix_s[2], mix_s[3])],
                          w_exp_gate[j], w_exp_up[j], w_exp_down[j])
        else:
            wg_bf = w_ff_gate[j].astype(BF16)
            wu_bf = w_ff_up[j].astype(BF16)
            wd_bf = w_ff_down[j].astype(BF16)
            xp = _ffn_dense(mix_p[1], mix_p[0], mod_p, wg_bf, wu_bf, wd_bf, tm=FFN_ROW_TILE)
            xs = _ffn_dense(mix_s[1], mix_s[0], mod_s, wg_bf, wu_bf, wd_bf, tm=DB)

        outs["kp"].append(jnp.transpose(ktp.reshape(B, N_HEADS, 2, HEAD_DIM, S), (0, 4, 1, 2, 3)))
        outs["vp"].append(vp.reshape(B, S, N_HEADS, V_DIM))
        outs["pp"].append(up[:, S - POOL_BUF:, :])
        outs["ks"].append(ks_rows.reshape(DB, 1, N_HEADS, 2, HEAD_DIM))
        outs["vs"].append(vs.reshape(DB, 1, N_HEADS, V_DIM))
        outs["ps"].append(jnp.concatenate([state_pool[l][:, 1:], us.reshape(DB, 1, POOL_W)], axis=1))

    return (xp, xs.reshape(DB, 1, D),
            jnp.stack(outs["kp"]), jnp.stack(outs["vp"]), jnp.stack(outs["pp"]),
            jnp.stack(outs["ks"]), jnp.stack(outs["vs"]), jnp.stack(outs["ps"]))
```

```python
import functools
import math

import jax
import jax.numpy as jnp
from jax import lax
from jax.experimental import pallas as pl
from jax.experimental.pallas import tpu as pltpu

F32 = jnp.float32
BF16 = jnp.bfloat16

D_MODEL = 1024
N_HEADS = 4
HEAD_DIM = 64
V_DIM = 2 * HEAD_DIM
QK_W = N_HEADS * 2 * HEAD_DIM
ATTN_W = N_HEADS * V_DIM
POOL_WINDOWS = (2, 4, 8, 16)
POOL_W = D_MODEL // 2
POOL_GROUP_W = POOL_W // len(POOL_WINDOWS)
POOL_BUF = max(POOL_WINDOWS) - 1
POOL_HALO = max(POOL_WINDOWS)
IN_COLS = 2 * QK_W + ATTN_W + POOL_W + 2 * D_MODEL
N_BUCKETS = 32
MAX_DISTANCE = 128
N_EXPERTS = 8
TOP_K = 2
EPS = 1e-6
LOG2E = math.log2(math.e)
LANES = 128
V7X_VMEM_BYTES = 64 * 1024 * 1024
VMEM_LIMIT = V7X_VMEM_BYTES * 7 // 8

ATTN_TILE = 512
ATTN_ROWS = 256
ATTN_SUB = 32
ATTN_UNROLL = ATTN_ROWS // ATTN_SUB
ROW_TILE = 512
FFN_ROW_TILE = 1024
MOE_ROW_TILE = 512
MOE_FF_TILE = 896
MOE_DMA_CHUNK = 256

_NT = (((1,), (1,)), ((), ()))


def _lambda_init(layer):
    return 0.8 - 0.6 * math.exp(-0.3 * layer)


def _params(n_axes, vmem=VMEM_LIMIT):
    return pltpu.CompilerParams(dimension_semantics=("arbitrary",) * n_axes,
                                vmem_limit_bytes=vmem)


def _dot(a, b):
    return jnp.dot(a, b, preferred_element_type=F32)


def _rms(x):
    return x * lax.rsqrt(jnp.mean(x * x, axis=-1, keepdims=True) + EPS)


def _silu(x):
    return x * jax.nn.sigmoid(x)


def _const_spec(a):
    nd = a.ndim
    return pl.BlockSpec(a.shape, lambda *_: (0,) * nd)


def _row_spec(tm, width):
    return pl.BlockSpec((1, tm, width), lambda b, i, *_: (b, i, 0))


def _mod_spec(mod, tm, chunk):
    if mod.shape[1] == 1:
        return pl.BlockSpec((1, 1, D_MODEL), lambda b, i, *_: (b, 0, chunk))
    return pl.BlockSpec((1, tm, D_MODEL), lambda b, i, *_: (b, i, chunk))


def _ada_kernel(c_ref, w_ref, b_ref, o_ref):
    a = _silu(c_ref[...]).astype(BF16)
    o_ref[0] = _dot(a, w_ref[0].astype(BF16)) + b_ref[0]


def _ada_mod(c_all, w_ada, b_ada):
    rows = c_all.shape[0]
    depth, _, cols = w_ada.shape
    tn = cols // 4
    return pl.pallas_call(
        _ada_kernel,
        grid=(depth, cols // tn),
        in_specs=[pl.BlockSpec((rows, D_MODEL), lambda l, n: (0, 0)),
                  pl.BlockSpec((1, D_MODEL, tn), lambda l, n: (l, 0, n)),
                  pl.BlockSpec((1, 1, tn), lambda l, n: (l, 0, n))],
        out_specs=pl.BlockSpec((1, rows, tn), lambda l, n: (l, 0, n)),
        out_shape=jax.ShapeDtypeStruct((depth, rows, cols), F32),
        compiler_params=_params(2),
        name="ada_mod",
    )(c_all, w_ada, b_ada.reshape(depth, 1, cols))


def _t5_bucket(rel):
    n = jnp.maximum(rel, 0)
    max_exact = N_BUCKETS // 2
    nf = jnp.maximum(n, 1).astype(F32)
    large = max_exact + (jnp.log(nf / max_exact) / math.log(MAX_DISTANCE / max_exact)
                         * (N_BUCKETS - max_exact)).astype(jnp.int32)
    large = jnp.minimum(large, N_BUCKETS - 1)
    return jnp.where(n < max_exact, n, large)


def _bias_lookup(bucket, rb_ref, h):
    acc = jnp.zeros(bucket.shape, F32)
    for b in range(N_BUCKETS):
        acc = jnp.where(bucket == b, rb_ref[b, h], acc)
    return acc


def _bias_tile_kernel(rb_ref, o_ref, *, t):
    h = pl.program_id(0)
    d = pl.program_id(1)
    r = lax.broadcasted_iota(jnp.int32, (t, t), 0)
    c = lax.broadcasted_iota(jnp.int32, (t, t), 1)
    rel = d * t + r - c
    val = _bias_lookup(_t5_bucket(rel), rb_ref, h)
    o_ref[0, 0] = jnp.where(rel >= 0, val * LOG2E, -jnp.inf)


def _bias_tiles(rel_bias, t):
    return pl.pallas_call(
        functools.partial(_bias_tile_kernel, t=t),
        grid=(N_HEADS, 2),
        in_specs=[pl.BlockSpec(memory_space=pltpu.SMEM)],
        out_specs=pl.BlockSpec((1, 1, t, t), lambda h, d: (h, d, 0, 0)),
        out_shape=jax.ShapeDtypeStruct((N_HEADS, 2, t, t), F32),
        compiler_params=_params(2),
        name="bias_tiles",
    )(rel_bias)


def _bias_row_kernel(rb_ref, o_ref, *, past):
    width = o_ref.shape[1]
    c = lax.broadcasted_iota(jnp.int32, (1, width), 1)
    bucket = _t5_bucket(jnp.maximum(past - c, 0))
    for h in range(N_HEADS):
        o_ref[h:h + 1, :] = _bias_lookup(bucket, rb_ref, h) * LOG2E


def _bias_rows(rel_bias, past):
    return pl.pallas_call(
        functools.partial(_bias_row_kernel, past=past),
        in_specs=[pl.BlockSpec(memory_space=pltpu.SMEM)],
        out_shape=jax.ShapeDtypeStruct((N_HEADS, past + LANES), F32),
        name="bias_rows",
    )(rel_bias)


def _window_means(ext, u, pos):
    tm = u.shape[0]
    outs = []
    for gi, w in enumerate(POOL_WINDOWS):
        ch = slice(gi * POOL_GROUP_W, (gi + 1) * POOL_GROUP_W)
        s = ext[:, ch]
        span = 1
        while span < w:
            s = s[span:] + s[:-span]
            span *= 2
        off = POOL_HALO - (w - 1)
        win = s[off:off + tm]
        cnt = jnp.minimum(pos + 1, w).astype(F32)
        outs.append(win / cnt - u[:, ch])
    return jnp.concatenate(outs, axis=-1)


def _in_proj_kernel(x_ref, sh_ref, sc_ref, gmix_ref, w_ref, wkt_ref, qg_ref, kg_ref, bd_ref,
                    q_ref, kt_ref, v_ref, u_ref, g_ref, *rest, tm, seq_pool):
    h = _rms(x_ref[0]) * gmix_ref[...]
    h = h * (1.0 + sc_ref[0]) + sh_ref[0]
    hb = h.astype(BF16)

    def proj(c0, c1):
        return _dot(hb, w_ref[:, c0:c1])

    zq = proj(0, QK_W)
    msq = _dot((zq * zq).astype(BF16), bd_ref[...])
    q_ref[0] = (zq * lax.rsqrt(msq + EPS) * qg_ref[...]).astype(BF16)

    zk = lax.dot_general(wkt_ref[...], hb, _NT, preferred_element_type=F32)
    zk = zk.reshape(N_HEADS * 2, HEAD_DIM, tm)
    msk = jnp.mean(zk * zk, axis=1, keepdims=True)
    kt_ref[0] = (zk * lax.rsqrt(msk + EPS) * kg_ref[...]).reshape(QK_W, tm)

    c0 = 2 * QK_W
    v_ref[0] = proj(c0, c0 + ATTN_W)
    c0 += ATTN_W
    u = proj(c0, c0 + POOL_W)
    u_ref[0] = u
    c0 += POOL_W
    for c in range(0, 2 * D_MODEL, 512):
        g_ref[0, :, c:c + 512] = jax.nn.sigmoid(proj(c0 + c, c0 + c + 512)).astype(BF16)

    if seq_pool:
        d_ref, carry_ref = rest
        i = pl.program_id(1)

        @pl.when(i == 0)
        def _():
            carry_ref[...] = jnp.zeros_like(carry_ref)

        ext = jnp.concatenate([carry_ref[...], u], axis=0)
        carry_ref[...] = u[tm - POOL_HALO:, :]
        pos = i * tm + lax.broadcasted_iota(jnp.int32, (tm, 1), 0)
        d_ref[0] = _window_means(ext, u, pos).astype(BF16)


def _in_proj(x, mod, gmix, w_bf, wkt_bf, qg, kg, bd, *, tm, seq_pool):
    B, T, _ = x.shape
    nt = T // tm
    out_shape = [jax.ShapeDtypeStruct((B, T, QK_W), BF16),
                 jax.ShapeDtypeStruct((B, QK_W, T), F32),
                 jax.ShapeDtypeStruct((B, T, ATTN_W), F32),
                 jax.ShapeDtypeStruct((B, T, POOL_W), F32),
                 jax.ShapeDtypeStruct((B, T, 2 * D_MODEL), BF16)]
    out_specs = [_row_spec(tm, QK_W),
                 pl.BlockSpec((1, QK_W, tm), lambda b, i: (b, 0, i)),
                 _row_spec(tm, ATTN_W), _row_spec(tm, POOL_W), _row_spec(tm, 2 * D_MODEL)]
    scratch = []
    if seq_pool:
        out_shape.append(jax.ShapeDtypeStruct((B, T, POOL_W), BF16))
        out_specs.append(_row_spec(tm, POOL_W))
        scratch.append(pltpu.VMEM((POOL_HALO, POOL_W), F32))
    return pl.pallas_call(
        functools.partial(_in_proj_kernel, tm=tm, seq_pool=seq_pool),
        grid=(B, nt),
        in_specs=[_row_spec(tm, D_MODEL), _mod_spec(mod, tm, 0), _mod_spec(mod, tm, 1),
                  _const_spec(gmix), _const_spec(w_bf), _const_spec(wkt_bf),
                  _const_spec(qg), _const_spec(kg), _const_spec(bd)],
        out_specs=out_specs,
        out_shape=out_shape,
        scratch_shapes=scratch,
        compiler_params=_params(2),
        name="in_proj",
    )(x, mod, mod, gmix, w_bf, wkt_bf, qg, kg, bd)


def _diff_lambda(lp, layer):
    a = jnp.sum(lp[0:1] * lp[1:2], axis=-1, keepdims=True)
    b = jnp.sum(lp[2:3] * lp[3:4], axis=-1, keepdims=True)
    return jnp.exp(a) - jnp.exp(b) + _lambda_init(layer)


def _attn_kernel(qi_ref, kj_ref, q_ref, kt_ref, v_ref, bias_ref, rb_ref, lp_ref, subln_ref,
                 o_ref, m_ref, l_ref, acc_ref, s_ref, p_ref, alpha_ref, *, t, layer):
    s_idx = pl.program_id(1)
    i = qi_ref[s_idx]
    j = kj_ref[s_idx]

    @pl.when(j == 0)
    def _():
        m_ref[...] = jnp.full_like(m_ref, -jnp.inf)
        l_ref[...] = jnp.zeros_like(l_ref)
        acc_ref[...] = jnp.zeros_like(acc_ref)

    lane = lax.broadcasted_iota(jnp.int32, (1, V_DIM), 1)
    map0 = jnp.where(lane < HEAD_DIM, 1.0, 0.0).astype(BF16)
    map1 = jnp.where(lane >= HEAD_DIM, 1.0, 0.0).astype(BF16)
    far = (i - j) >= 2
    for h in range(N_HEADS):
        hs = slice(h * V_DIM, (h + 1) * V_DIM)
        q = q_ref[0, :, hs]
        kb = kt_ref[0, hs, :].astype(BF16)
        vb = jnp.concatenate([v_ref[0, :, hs].astype(BF16), jnp.ones((t, V_DIM), BF16)], axis=1)
        far_bias = rb_ref[N_BUCKETS - 1, h] * LOG2E
        for c in range(2 * t // ATTN_ROWS):
            qrow = (c * ATTN_ROWS) % t
            qc = q[qrow:qrow + ATTN_ROWS] * (map0 if c * ATTN_ROWS < t else map1)
            s_ref[...] = _dot(qc, kb)
            row0 = c * ATTN_ROWS

            def row_max(k, carry, h=h, qrow=qrow, row0=row0, far_bias=far_bias):
                r0 = pl.multiple_of(k * ATTN_SUB, ATTN_SUB)
                rows = pl.ds(row0 + r0, ATTN_SUB)
                bias = jnp.where(far, far_bias, bias_ref[h, 0, pl.ds(qrow + r0, ATTN_SUB), :])
                s = s_ref[pl.ds(r0, ATTN_SUB), :] + bias
                s_ref[pl.ds(r0, ATTN_SUB), :] = s
                m_prev = m_ref[h, rows, :]
                m_new = jnp.maximum(m_prev, jnp.max(s, axis=-1, keepdims=True))
                alpha_ref[pl.ds(r0, ATTN_SUB), :] = jnp.broadcast_to(jnp.exp2(m_prev - m_new),
                                                                     (ATTN_SUB, V_DIM))
                m_ref[h, rows, :] = m_new
                return carry

            def row_exp(k, carry, h=h, row0=row0):
                r0 = pl.multiple_of(k * ATTN_SUB, ATTN_SUB)
                m = m_ref[h, pl.ds(row0 + r0, ATTN_SUB), :]
                p_ref[pl.ds(r0, ATTN_SUB), :] = jnp.exp2(s_ref[pl.ds(r0, ATTN_SUB), :] - m).astype(BF16)
                return carry

            lax.fori_loop(0, ATTN_ROWS // ATTN_SUB, row_max, 0, unroll=ATTN_UNROLL)
            lax.fori_loop(0, ATTN_ROWS // ATTN_SUB, row_exp, 0, unroll=ATTN_UNROLL)
            blk = slice(row0, row0 + ATTN_ROWS)
            pv = _dot(p_ref[...], vb)
            alpha = alpha_ref[...]
            acc_ref[h, blk, :] = alpha * acc_ref[h, blk, :] + pv[:, :V_DIM]
            l_ref[h, blk, :] = alpha * l_ref[h, blk, :] + pv[:, V_DIM:]

    @pl.when(j == i)
    def _():
        lam = _diff_lambda(lp_ref[...], layer)
        for h in range(N_HEADS):
            o2 = acc_ref[h] / l_ref[h]
            o = o2[:t] - lam * o2[t:]
            o = _rms(o) * subln_ref[...] * (1.0 - _lambda_init(layer))
            o_ref[0, :, h * V_DIM:(h + 1) * V_DIM] = o.astype(BF16)


def _attn_prompt(q, kt, v, bias_tiles, rel_bias, lp, subln, *, layer, t):
    B, T, _ = q.shape
    n = T // t
    pairs = [(i, j) for i in range(n) for j in range(i + 1)]
    qi = jnp.asarray([p[0] for p in pairs], jnp.int32)
    kj = jnp.asarray([p[1] for p in pairs], jnp.int32)
    grid_spec = pltpu.PrefetchScalarGridSpec(
        num_scalar_prefetch=2,
        grid=(B, len(pairs)),
        in_specs=[pl.BlockSpec((1, t, QK_W), lambda b, s, qi, kj: (b, qi[s], 0)),
                  pl.BlockSpec((1, QK_W, t), lambda b, s, qi, kj: (b, 0, kj[s])),
                  pl.BlockSpec((1, t, ATTN_W), lambda b, s, qi, kj: (b, kj[s], 0)),
                  pl.BlockSpec((N_HEADS, 1, t, t),
                               lambda b, s, qi, kj: (0, jnp.minimum(qi[s] - kj[s], 1), 0, 0)),
                  pl.BlockSpec(memory_space=pltpu.SMEM),
                  pl.BlockSpec(lp.shape, lambda *_: (0, 0)),
                  pl.BlockSpec(subln.shape, lambda *_: (0, 0))],
        out_specs=pl.BlockSpec((1, t, ATTN_W), lambda b, s, qi, kj: (b, qi[s], 0)),
        scratch_shapes=[pltpu.VMEM((N_HEADS, 2 * t, 1), F32),
                        pltpu.VMEM((N_HEADS, 2 * t, V_DIM), F32),
                        pltpu.VMEM((N_HEADS, 2 * t, V_DIM), F32),
                        pltpu.VMEM((ATTN_ROWS, t), F32),
                        pltpu.VMEM((ATTN_ROWS, t), BF16),
                        pltpu.VMEM((ATTN_ROWS, V_DIM), F32)])
    return pl.pallas_call(
        functools.partial(_attn_kernel, t=t, layer=layer),
        grid_spec=grid_spec,
        out_shape=jax.ShapeDtypeStruct((B, T, ATTN_W), BF16),
        compiler_params=_params(2),
        name="attn_prompt",
    )(qi, kj, q, kt, v, bias_tiles, rel_bias, lp, subln)


def _attn_sample_kernel(pt_ref, q_ref, kn_ref, vn_ref, *refs, n_pages, page, layer):
    k_refs = refs[:n_pages]
    v_refs = refs[n_pages:2 * n_pages]
    bias_ref, lp_ref, subln_ref, o_ref = refs[2 * n_pages:]
    rows = 2 * N_HEADS
    row = lax.broadcasted_iota(jnp.int32, (rows, QK_W), 0)
    seg = lax.broadcasted_iota(jnp.int32, (rows, QK_W), 1) // HEAD_DIM
    q = jnp.broadcast_to(q_ref[0].astype(F32), (rows, QK_W))
    qbd_f = jnp.where(seg == (row % N_HEADS) * 2 + row // N_HEADS, q, 0.0)
    qbd = qbd_f.astype(BF16)

    s = jnp.concatenate([_dot(qbd, k_refs[p][0, 0].astype(BF16)) for p in range(n_pages)], axis=1)
    past = n_pages * page
    bias = bias_ref[...]
    bias = jnp.concatenate([bias, bias], axis=0)
    s = s + bias[:, :past]
    s_new = jnp.sum(qbd_f * kn_ref[0], axis=-1, keepdims=True) + bias[:, past:past + 1]
    m = jnp.maximum(jnp.max(s, axis=-1, keepdims=True), s_new)
    p = jnp.exp2(s - m)
    p_new = jnp.exp2(s_new - m)
    denom = jnp.sum(p, axis=-1, keepdims=True) + p_new
    p = p / denom
    p_new = p_new / denom

    lam = _diff_lambda(lp_ref[...], layer)
    first = lax.broadcasted_iota(jnp.int32, (rows, 1), 0) < N_HEADS
    a = jnp.where(first, p - lam * pltpu.roll(p, N_HEADS, 0), 0.0).astype(BF16)
    a_new = p_new[:N_HEADS] - lam * p_new[N_HEADS:]

    out = jnp.zeros((rows, ATTN_W), F32)
    for pg in range(n_pages):
        vp = jnp.concatenate([v_refs[pg][0, 0, pl.ds(h, page, stride=N_HEADS), :]
                              for h in range(N_HEADS)], axis=1)
        out = out + _dot(a[:, pg * page:(pg + 1) * page], vp.astype(BF16))
    out = out[:N_HEADS] + a_new * vn_ref[0]
    rowh = lax.broadcasted_iota(jnp.int32, (N_HEADS, ATTN_W), 0)
    head = lax.broadcasted_iota(jnp.int32, (N_HEADS, ATTN_W), 1) // V_DIM
    om = jnp.where(rowh == head, out, 0.0)
    ms = jnp.sum(om * om, axis=-1, keepdims=True) / V_DIM
    on = om * lax.rsqrt(ms + EPS)
    o = jnp.sum(on, axis=0, keepdims=True) * subln_ref[...] * (1.0 - _lambda_init(layer))
    o_ref[0] = o.astype(BF16)


def _attn_sample(q, k_new, v_new, cache_kt, cache_vr, page_table, bias_rows, lp, subln4, *, layer):
    DB = q.shape[0]
    n_pages = page_table.shape[1]
    page = cache_kt.shape[3]

    def page_spec(p):
        return pl.BlockSpec((1, 1, QK_W, page), lambda b, pt: (layer, pt[b, p], 0, 0))

    def vpage_spec(p):
        return pl.BlockSpec((1, 1, page * N_HEADS, V_DIM), lambda b, pt: (layer, pt[b, p], 0, 0))

    vec = pl.BlockSpec((1, 1, QK_W), lambda b, pt: (b, 0, 0))
    grid_spec = pltpu.PrefetchScalarGridSpec(
        num_scalar_prefetch=1,
        grid=(DB,),
        in_specs=([vec, vec, vec] + [page_spec(p) for p in range(n_pages)]
                  + [vpage_spec(p) for p in range(n_pages)]
                  + [pl.BlockSpec(bias_rows.shape, lambda *_: (0, 0)),
                     pl.BlockSpec(lp.shape, lambda *_: (0, 0)),
                     pl.BlockSpec(subln4.shape, lambda *_: (0, 0))]),
        out_specs=vec)
    return pl.pallas_call(
        functools.partial(_attn_sample_kernel, n_pages=n_pages, page=page, layer=layer),
        grid_spec=grid_spec,
        out_shape=jax.ShapeDtypeStruct((DB, 1, ATTN_W), BF16),
        compiler_params=_params(1),
        name="attn_sample",
    )(page_table, q, k_new, v_new, *([cache_kt] * n_pages), *([cache_vr] * n_pages),
      bias_rows, lp, subln4)


def _pool_sample_kernel(state_ref, u_ref, d_ref):
    u = u_ref[...]
    outs = []
    for gi, w in enumerate(POOL_WINDOWS):
        ch = slice(gi * POOL_GROUP_W, (gi + 1) * POOL_GROUP_W)
        s = u[:, ch]
        for r in range(POOL_BUF - (w - 1), POOL_BUF):
            s = s + state_ref[r, :, ch]
        outs.append(s / float(w) - u[:, ch])
    d_ref[...] = jnp.concatenate(outs, axis=-1).astype(BF16)


def _pool_sample(state_t, u):
    return pl.pallas_call(
        _pool_sample_kernel,
        out_shape=jax.ShapeDtypeStruct(u.shape, BF16),
        name="pool_sample",
    )(state_t, u)


def _mix_kernel(o_ref, d_ref, g_ref, x_ref, gt_ref, sh_ref, sc_ref, wp_ref, ps_ref, wb_ref, wo_ref,
                gffn_ref, *rest, router):
    d = d_ref[0]
    y = jnp.concatenate([_dot(d[:, g * POOL_GROUP_W:(g + 1) * POOL_GROUP_W], wp_ref[g])
                         for g in range(len(POOL_WINDOWS))], axis=-1)
    y = (y * ps_ref[...]).astype(BF16)
    pa = _dot(o_ref[0], wb_ref[:ATTN_W])
    pb = _dot(y, wb_ref[ATTN_W:])
    g = g_ref[0]
    merged = g[:, :D_MODEL].astype(F32) * pa + g[:, D_MODEL:].astype(F32) * pb
    xn = x_ref[0] + gt_ref[0] * _dot(merged.astype(BF16), wo_ref[...])
    h2 = _rms(xn) * gffn_ref[...]
    h2 = h2 * (1.0 + sc_ref[0]) + sh_ref[0]
    if not router:
        xn_ref, h2_ref = rest
        xn_ref[0] = xn
        h2_ref[0] = h2.astype(h2_ref.dtype)
        return

    wr_hi_ref, wr_lo_ref, xn_ref, h2_ref, ri_ref, rp_ref = rest
    xn_ref[0] = xn
    h2_ref[0] = h2
    hi = h2.astype(BF16)
    lo = (h2 - hi.astype(F32)).astype(BF16)
    logits = _dot(hi, wr_hi_ref[...]) + _dot(lo, wr_hi_ref[...]) + _dot(hi, wr_lo_ref[...])
    lane = lax.broadcasted_iota(jnp.int32, logits.shape, 1)
    lg = jnp.where(lane < N_EXPERTS, logits, -jnp.inf)
    m1 = jnp.max(lg, axis=-1, keepdims=True)
    i1 = jnp.min(jnp.where(lg == m1, lane, LANES), axis=-1, keepdims=True)
    lg = jnp.where(lane == i1, -jnp.inf, lg)
    m2 = jnp.max(lg, axis=-1, keepdims=True)
    i2 = jnp.min(jnp.where(lg == m2, lane, LANES), axis=-1, keepdims=True)
    e = jnp.exp(m2 - m1)
    ri_ref[0] = jnp.concatenate([i1, i2], axis=-1)
    rp_ref[0] = jnp.concatenate([1.0 / (1.0 + e), e / (1.0 + e)], axis=-1)


def _mix(o, d, g, x, mod, wp_bf, pscale, wb_bf, wo_bf, gffn, router_w, *, tm):
    B, T, _ = x.shape
    router = router_w is not None
    ins = [o, d, g, x, mod, mod, mod, wp_bf, pscale, wb_bf, wo_bf, gffn]
    in_specs = [_row_spec(tm, ATTN_W), _row_spec(tm, POOL_W), _row_spec(tm, 2 * D_MODEL),
                _row_spec(tm, D_MODEL), _mod_spec(mod, tm, 2), _mod_spec(mod, tm, 3),
                _mod_spec(mod, tm, 4)] + [_const_spec(a) for a in ins[7:]]
    out_shape = [jax.ShapeDtypeStruct((B, T, D_MODEL), F32),
                 jax.ShapeDtypeStruct((B, T, D_MODEL), F32 if router else BF16)]
    out_specs = [_row_spec(tm, D_MODEL), _row_spec(tm, D_MODEL)]
    if router:
        ins += list(router_w)
        in_specs += [_const_spec(a) for a in router_w]
        out_shape += [jax.ShapeDtypeStruct((B, T, TOP_K), jnp.int32),
                      jax.ShapeDtypeStruct((B, T, TOP_K), F32)]
        out_specs += [_row_spec(tm, TOP_K), _row_spec(tm, TOP_K)]
    return pl.pallas_call(
        functools.partial(_mix_kernel, router=router),
        grid=(B, T // tm),
        in_specs=in_specs,
        out_specs=out_specs,
        out_shape=out_shape,
        compiler_params=_params(2),
        name="mix",
    )(*ins)


def _ffn_kernel(h_ref, x_ref, gt_ref, wg_ref, wu_ref, wd_ref, y_ref, acc_ref):
    f = pl.program_id(2)

    @pl.when(f == 0)
    def _():
        acc_ref[...] = jnp.zeros_like(acc_ref)

    h = h_ref[0]
    a = (_silu(_dot(h, wg_ref[...])) * _dot(h, wu_ref[...])).astype(BF16)
    acc_ref[...] += _dot(a, wd_ref[...])

    @pl.when(f == pl.num_programs(2) - 1)
    def _():
        y_ref[0] = x_ref[0] + gt_ref[0] * acc_ref[...]


def _ffn_dense(h2, x, mod, wg_bf, wu_bf, wd_bf, *, tm):
    B, T, _ = x.shape
    d_ff = wg_bf.shape[1]
    tf = d_ff // 2
    return pl.pallas_call(
        _ffn_kernel,
        grid=(B, T // tm, d_ff // tf),
        in_specs=[_row_spec(tm, D_MODEL), _row_spec(tm, D_MODEL), _mod_spec(mod, tm, 5),
                  pl.BlockSpec((D_MODEL, tf), lambda b, i, f: (0, f)),
                  pl.BlockSpec((D_MODEL, tf), lambda b, i, f: (0, f)),
                  pl.BlockSpec((tf, D_MODEL), lambda b, i, f: (f, 0))],
        out_specs=_row_spec(tm, D_MODEL),
        out_shape=jax.ShapeDtypeStruct((B, T, D_MODEL), F32),
        scratch_shapes=[pltpu.VMEM((tm, D_MODEL), F32)],
        compiler_params=_params(3),
        name="ffn_dense",
    )(h2, x, mod, wg_bf, wu_bf, wd_bf)


def _row_copy(src_hbm, src_row, dst, dst_row, sem):
    return pltpu.make_async_copy(src_hbm.at[pl.ds(src_row, 1), :], dst.at[pl.ds(dst_row, 1), :], sem)


def _moe_ffn_kernel(te_ref, nu_ref, src_ref, src_next_ref, h_hbm, wg_ref, wu_ref, wd_ref, y_ref,
                    xbuf, acc_ref, sem, *, tm):
    i = pl.program_id(0)
    f = pl.program_id(1)
    nu = nu_ref[0]
    slot = i % 2

    def issue(idx_ref, dst):
        def body(r, carry):
            _row_copy(h_hbm, idx_ref[0, 0, r], xbuf.at[dst], r, sem.at[dst]).start()
            return carry
        lax.fori_loop(0, tm, body, 0, unroll=8)

    def drain(dst):
        def body(r, carry):
            _row_copy(h_hbm, 0, xbuf.at[dst], 0, sem.at[dst]).wait()
            return carry
        lax.fori_loop(0, tm, body, 0, unroll=8)

    @pl.when(i < nu)
    def _():
        @pl.when(f == 0)
        def _():
            @pl.when(i == 0)
            def _():
                issue(src_ref, 0)

            drain(slot)

            @pl.when(i + 1 < nu)
            def _():
                issue(src_next_ref, 1 - slot)

            acc_ref[...] = jnp.zeros_like(acc_ref)

        xb = xbuf[slot].astype(BF16)
        a = (_silu(_dot(xb, wg_ref[0].astype(BF16))) * _dot(xb, wu_ref[0].astype(BF16))).astype(BF16)
        acc_ref[...] += _dot(a, wd_ref[0].astype(BF16))

        @pl.when(f == pl.num_programs(1) - 1)
        def _():
            y_ref[...] = acc_ref[...]

    @pl.when(i >= nu)
    def _():
        y_ref[...] = jnp.zeros_like(y_ref)


def _moe_ffn(h_rows, src, tile_expert, n_used, wg, wu, wd, *, tm, tf):
    n_tiles = src.shape[0]
    d_ff = wg.shape[2]
    nf = d_ff // tf

    def tile(i, nu):
        return jnp.minimum(i, nu[0] - 1)

    def ff(i, f, nu):
        return jnp.where(i < nu[0], f, nf - 1)

    grid_spec = pltpu.PrefetchScalarGridSpec(
        num_scalar_prefetch=2,
        grid=(n_tiles, nf),
        in_specs=[pl.BlockSpec((1, 1, tm), lambda i, f, te, nu: (i, 0, 0), memory_space=pltpu.SMEM),
                  pl.BlockSpec((1, 1, tm), lambda i, f, te, nu: (jnp.minimum(i + 1, n_tiles - 1), 0, 0),
                               memory_space=pltpu.SMEM),
                  pl.BlockSpec(memory_space=pl.ANY),
                  pl.BlockSpec((1, D_MODEL, tf), lambda i, f, te, nu: (te[tile(i, nu)], 0, ff(i, f, nu))),
                  pl.BlockSpec((1, D_MODEL, tf), lambda i, f, te, nu: (te[tile(i, nu)], 0, ff(i, f, nu))),
                  pl.BlockSpec((1, tf, D_MODEL), lambda i, f, te, nu: (te[tile(i, nu)], ff(i, f, nu), 0))],
        out_specs=pl.BlockSpec((tm, D_MODEL), lambda i, f, te, nu: (i, 0)),
        scratch_shapes=[pltpu.VMEM((2, tm, D_MODEL), F32), pltpu.VMEM((tm, D_MODEL), F32),
                        pltpu.SemaphoreType.DMA((2,))])
    return pl.pallas_call(
        functools.partial(_moe_ffn_kernel, tm=tm),
        grid_spec=grid_spec,
        out_shape=jax.ShapeDtypeStruct((n_tiles * tm, D_MODEL), F32),
        compiler_params=_params(2),
        name="moe_ffn",
    )(tile_expert, n_used, src, src, h_rows, wg, wu, wd)


def _combine_kernel(pos_ref, y_hbm, x_ref, gt_ref, p_ref, o_ref, buf, sem, *, tc):
    def issue(t, carry):
        for k in range(TOP_K):
            _row_copy(y_hbm, pos_ref[0, 0, TOP_K * t + k], buf.at[k], t, sem).start()
        return carry

    lax.fori_loop(0, tc, issue, 0)

    def drain(t, carry):
        for k in range(TOP_K):
            _row_copy(y_hbm, 0, buf.at[0], 0, sem).wait()
        return carry

    lax.fori_loop(0, tc, drain, 0)
    p = p_ref[0]
    f = p[:, 0:1] * buf[0] + p[:, 1:2] * buf[1]
    o_ref[0] = x_ref[0] + gt_ref[0] * f


def _combine(y, pos, probs, x, mod, *, tc):
    B, T, _ = x.shape
    nt = T // tc
    return pl.pallas_call(
        functools.partial(_combine_kernel, tc=tc),
        grid=(B, nt),
        in_specs=[pl.BlockSpec((1, 1, TOP_K * tc), lambda b, i: (b * nt + i, 0, 0),
                               memory_space=pltpu.SMEM),
                  pl.BlockSpec(memory_space=pl.ANY),
                  _row_spec(tc, D_MODEL), _mod_spec(mod, tc, 5), _row_spec(tc, TOP_K)],
        out_specs=_row_spec(tc, D_MODEL),
        out_shape=jax.ShapeDtypeStruct(x.shape, F32),
        scratch_shapes=[pltpu.VMEM((TOP_K, tc, D_MODEL), F32), pltpu.SemaphoreType.DMA],
        compiler_params=_params(2),
        name="moe_combine",
    )(pos.reshape(B * nt, 1, TOP_K * tc), y, x, mod, probs)


def _route_plan(top_i, tm, n_tiles):
    e_flat = top_i.reshape(-1)
    onehot = (e_flat[:, None] == jnp.arange(N_EXPERTS, dtype=jnp.int32)[None, :]).astype(jnp.int32)
    csum = jnp.cumsum(onehot, axis=0)
    cnt = csum[-1]
    gsz = (cnt + tm - 1) // tm * tm
    gend = jnp.cumsum(gsz)
    gstart = gend - gsz
    pos = jnp.sum(onehot * (gstart[None, :] + csum - 1), axis=1).astype(jnp.int32)
    n_used = (gend[-1] // tm).astype(jnp.int32).reshape(1)
    tile_start = jnp.arange(n_tiles, dtype=jnp.int32) * tm
    tile_expert = jnp.sum((tile_start[:, None] >= gend[None, :]).astype(jnp.int32), axis=1)
    tile_expert = jnp.minimum(tile_expert, N_EXPERTS - 1).astype(jnp.int32)
    token = jnp.arange(e_flat.shape[0], dtype=jnp.int32) // TOP_K
    src = jnp.zeros((n_tiles * tm,), jnp.int32).at[pos].set(token, unique_indices=True)
    return pos.reshape(top_i.shape), src.reshape(n_tiles, 1, tm), tile_expert, n_used


def _moe(groups, wg, wu, wd):
    tm = MOE_ROW_TILE
    counts = [g[0].shape[0] * g[0].shape[1] for g in groups]
    total = sum(counts) * TOP_K
    n_tiles = (total + N_EXPERTS * (tm - 1)) // tm + 1
    top_all = jnp.concatenate([g[3].reshape(-1, TOP_K) for g in groups], axis=0)
    h_all = jnp.concatenate([g[0].reshape(-1, D_MODEL) for g in groups], axis=0)
    pos_all, src, tile_expert, n_used = _route_plan(top_all, tm, n_tiles)
    y = _moe_ffn(h_all, src, tile_expert, n_used, wg, wu, wd, tm=tm, tf=MOE_FF_TILE)
    outs = []
    offs = 0
    for (h2, x, mod, top_i, probs), cnt in zip(groups, counts):
        pos = pos_all[offs:offs + cnt].reshape(top_i.shape)
        offs += cnt
        outs.append(_combine(y, pos, probs, x, mod, tc=min(MOE_DMA_CHUNK, x.shape[1])))
    return outs


def kernel(x_prompt, x_sample, c_prompt, c_sample, cache_k, cache_v, state_pool, page_table,
           rel_bias, w_ada, b_ada, g_mix, g_ffn, w_in, q_norm, k_norm, diff_lambda, subln,
           w_pool, pool_scale, w_branch, w_out, w_ff_gate, w_ff_up, w_ff_down, w_router,
           w_exp_gate, w_exp_up, w_exp_down):
    B, S, D = x_prompt.shape
    DB = x_sample.shape[0]
    depth = w_in.shape[0]
    n_phys, page = cache_k.shape[1], cache_k.shape[2]
    past = page_table.shape[1] * page

    mod_all = _ada_mod(jnp.concatenate([c_prompt, c_sample], axis=0), w_ada, b_ada)
    bias_tiles = _bias_tiles(rel_bias, ATTN_TILE)
    bias_rows = _bias_rows(rel_bias, past)

    cache_kt = jnp.transpose(cache_k, (0, 1, 3, 4, 5, 2)).reshape(depth, n_phys, QK_W, page)
    cache_vr = cache_v.reshape(depth, n_phys, page * N_HEADS, V_DIM)
    state_t = jnp.transpose(state_pool, (0, 2, 1, 3))

    bd = jnp.kron(jnp.eye(QK_W // HEAD_DIM, dtype=F32),
                  jnp.full((HEAD_DIM, HEAD_DIM), 1.0 / HEAD_DIM, F32)).astype(BF16)

    xp = x_prompt
    xs = x_sample.reshape(1, DB, D)
    outs = {k: [] for k in ("kp", "vp", "pp", "ks", "vs", "ps")}
    for l in range(depth):
        mod_p = mod_all[l, :B].reshape(B, 1, 6 * D)
        mod_s = mod_all[l, B:].reshape(1, DB, 6 * D)
        w_bf = w_in[l].astype(BF16)
        wkt_bf = w_in[l][:, QK_W:2 * QK_W].T.astype(BF16)
        qg = jnp.tile(q_norm[l].reshape(1, 2 * HEAD_DIM), (1, N_HEADS)) * (HEAD_DIM ** -0.5 * LOG2E)
        kg = jnp.tile(k_norm[l], (N_HEADS, 1)).reshape(N_HEADS * 2, HEAD_DIM, 1)
        gmix = g_mix[l].reshape(1, D)
        gffn = g_ffn[l].reshape(1, D)
        lp = diff_lambda[l]
        sub1 = subln[l].reshape(1, V_DIM)
        sub4 = jnp.tile(sub1, (1, N_HEADS))
        wp_bf = w_pool[l].astype(BF16)
        pscale = pool_scale[l].reshape(1, POOL_W)
        wb_bf = w_branch[l].astype(BF16)
        wo_bf = w_out[l].astype(BF16)
        moe = l % 2 == 1
        j = l // 2
        router_w = None
        if moe:
            wr = jnp.pad(w_router[j], ((0, 0), (0, LANES - N_EXPERTS)))
            wr_hi = wr.astype(BF16)
            router_w = (wr_hi, (wr - wr_hi.astype(F32)).astype(BF16))

        qp, ktp, vp, up, gp, dp = _in_proj(xp, mod_p, gmix, w_bf, wkt_bf, qg, kg, bd,
                                           tm=ROW_TILE, seq_pool=True)
        op = _attn_prompt(qp, ktp, vp, bias_tiles, rel_bias, lp, sub1, layer=l, t=ATTN_TILE)
        mix_p = _mix(op, dp, gp, xp, mod_p, wp_bf, pscale, wb_bf, wo_bf, gffn, router_w, tm=ROW_TILE)

        qs, kts, vs, us, gs = _in_proj(xs, mod_s, gmix, w_bf, wkt_bf, qg, kg, bd,
                                       tm=DB, seq_pool=False)
        ks_rows = jnp.transpose(kts, (0, 2, 1)).reshape(DB, 1, QK_W)
        osamp = _attn_sample(qs.reshape(DB, 1, QK_W), ks_rows, vs.reshape(DB, 1, ATTN_W),
                             cache_kt, cache_vr, page_table, bias_rows, lp, sub4, layer=l)
        ds = _pool_sample(state_t[l], us[0])
        mix_s = _mix(osamp.reshape(1, DB, ATTN_W), ds.reshape(1, DB, POOL_W), gs, xs, mod_s,
                     wp_bf, pscale, wb_bf, wo_bf, gffn, router_w, tm=DB)

        if moe:
            xp, xs = _moe([(mix_p[1], mix_p[0], mod_p, mix_p[2], mix_p[3]),
                           (mix_s[1], mix_s[0], mod_s, mix_s[2], mix_s[3])],
                          w_exp_gate[j], w_exp_up[j], w_exp_down[j])
        else:
            wg_bf = w_ff_gate[j].astype(BF16)
            wu_bf = w_ff_up[j].astype(BF16)
            wd_bf = w_ff_down[j].astype(BF16)
            xp = _ffn_dense(mix_p[1], mix_p[0], mod_p, wg_bf, wu_bf, wd_bf, tm=FFN_ROW_TILE)
            xs = _ffn_dense(mix_s[1], mix_s[0], mod_s, wg_bf, wu_bf, wd_bf, tm=DB)

        outs["kp"].append(jnp.transpose(ktp.reshape(B, N_HEADS, 2, HEAD_DIM, S), (0, 4, 1, 2, 3)))
        outs["vp"].append(vp.reshape(B, S, N_HEADS, V_DIM))
        outs["pp"].append(up[:, S - POOL_BUF:, :])
        outs["ks"].append(ks_rows.reshape(DB, 1, N_HEADS, 2, HEAD_DIM))
        outs["vs"].append(vs.reshape(DB, 1, N_HEADS, V_DIM))
        outs["ps"].append(jnp.concatenate([state_pool[l][:, 1:], us.reshape(DB, 1, POOL_W)], axis=1))

    return (xp, xs.reshape(DB, 1, D),
            jnp.stack(outs["kp"]), jnp.stack(outs["vp"]), jnp.stack(outs["pp"]),
            jnp.stack(outs["ks"]), jnp.stack(outs["vs"]), jnp.stack(outs["ps"]))
```

```python
import functools
import math

import jax
import jax.numpy as jnp
from jax import lax
from jax.experimental import pallas as pl
from jax.experimental.pallas import tpu as pltpu

F32 = jnp.float32
BF16 = jnp.bfloat16

D_MODEL = 1024
N_HEADS = 4
HEAD_DIM = 64
V_DIM = 2 * HEAD_DIM
QK_W = N_HEADS * 2 * HEAD_DIM
ATTN_W = N_HEADS * V_DIM
POOL_WINDOWS = (2, 4, 8, 16)
POOL_W = D_MODEL // 2
POOL_GROUP_W = POOL_W // len(POOL_WINDOWS)
POOL_BUF = max(POOL_WINDOWS) - 1
POOL_HALO = max(POOL_WINDOWS)
IN_COLS = 2 * QK_W + ATTN_W + POOL_W + 2 * D_MODEL
N_BUCKETS = 32
MAX_DISTANCE = 128
N_EXPERTS = 8
TOP_K = 2
EPS = 1e-6
LOG2E = math.log2(math.e)
LANES = 128
V7X_VMEM_BYTES = 64 * 1024 * 1024
VMEM_LIMIT = V7X_VMEM_BYTES * 7 // 8

ATTN_TILE = 512
ATTN_ROWS = 256
ATTN_SUB = 32
ATTN_UNROLL = ATTN_ROWS // ATTN_SUB
ROW_TILE = 512
FFN_ROW_TILE = 1024
MOE_ROW_TILE = 1024
MOE_FF_TILE = 512
MOE_DMA_CHUNK = 256

_NT = (((1,), (1,)), ((), ()))


def _lambda_init(layer):
    return 0.8 - 0.6 * math.exp(-0.3 * layer)


def _params(n_axes, vmem=VMEM_LIMIT):
    return pltpu.CompilerParams(dimension_semantics=("arbitrary",) * n_axes,
                                vmem_limit_bytes=vmem)


def _dot(a, b):
    return jnp.dot(a, b, preferred_element_type=F32)


def _rms(x):
    return x * lax.rsqrt(jnp.mean(x * x, axis=-1, keepdims=True) + EPS)


def _silu(x):
    return x * jax.nn.sigmoid(x)


def _const_spec(a):
    nd = a.ndim
    return pl.BlockSpec(a.shape, lambda *_: (0,) * nd)


def _row_spec(tm, width):
    return pl.BlockSpec((1, tm, width), lambda b, i, *_: (b, i, 0))


def _mod_spec(mod, tm, chunk):
    if mod.shape[1] == 1:
        return pl.BlockSpec((1, 1, D_MODEL), lambda b, i, *_: (b, 0, chunk))
    return pl.BlockSpec((1, tm, D_MODEL), lambda b, i, *_: (b, i, chunk))


def _ada_kernel(c_ref, w_ref, b_ref, o_ref):
    a = _silu(c_ref[...]).astype(BF16)
    o_ref[0] = _dot(a, w_ref[0].astype(BF16)) + b_ref[0]


def _ada_mod(c_all, w_ada, b_ada):
    rows = c_all.shape[0]
    depth, _, cols = w_ada.shape
    tn = cols // 4
    return pl.pallas_call(
        _ada_kernel,
        grid=(depth, cols // tn),
        in_specs=[pl.BlockSpec((rows, D_MODEL), lambda l, n: (0, 0)),
                  pl.BlockSpec((1, D_MODEL, tn), lambda l, n: (l, 0, n)),
                  pl.BlockSpec((1, 1, tn), lambda l, n: (l, 0, n))],
        out_specs=pl.BlockSpec((1, rows, tn), lambda l, n: (l, 0, n)),
        out_shape=jax.ShapeDtypeStruct((depth, rows, cols), F32),
        compiler_params=_params(2),
        name="ada_mod",
    )(c_all, w_ada, b_ada.reshape(depth, 1, cols))


def _t5_bucket(rel):
    n = jnp.maximum(rel, 0)
    max_exact = N_BUCKETS // 2
    nf = jnp.maximum(n, 1).astype(F32)
    large = max_exact + (jnp.log(nf / max_exact) / math.log(MAX_DISTANCE / max_exact)
                         * (N_BUCKETS - max_exact)).astype(jnp.int32)
    large = jnp.minimum(large, N_BUCKETS - 1)
    return jnp.where(n < max_exact, n, large)


def _bias_lookup(bucket, rb_ref, h):
    acc = jnp.zeros(bucket.shape, F32)
    for b in range(N_BUCKETS):
        acc = jnp.where(bucket == b, rb_ref[b, h], acc)
    return acc


def _bias_tile_kernel(rb_ref, o_ref, *, t):
    h = pl.program_id(0)
    d = pl.program_id(1)
    r = lax.broadcasted_iota(jnp.int32, (t, t), 0)
    c = lax.broadcasted_iota(jnp.int32, (t, t), 1)
    rel = d * t + r - c
    val = _bias_lookup(_t5_bucket(rel), rb_ref, h)
    o_ref[0, 0] = jnp.where(rel >= 0, val * LOG2E, -jnp.inf)


def _bias_tiles(rel_bias, t):
    return pl.pallas_call(
        functools.partial(_bias_tile_kernel, t=t),
        grid=(N_HEADS, 2),
        in_specs=[pl.BlockSpec(memory_space=pltpu.SMEM)],
        out_specs=pl.BlockSpec((1, 1, t, t), lambda h, d: (h, d, 0, 0)),
        out_shape=jax.ShapeDtypeStruct((N_HEADS, 2, t, t), F32),
        compiler_params=_params(2),
        name="bias_tiles",
    )(rel_bias)


def _bias_row_kernel(rb_ref, o_ref, *, past):
    width = o_ref.shape[1]
    c = lax.broadcasted_iota(jnp.int32, (1, width), 1)
    bucket = _t5_bucket(jnp.maximum(past - c, 0))
    for h in range(N_HEADS):
        o_ref[h:h + 1, :] = _bias_lookup(bucket, rb_ref, h) * LOG2E


def _bias_rows(rel_bias, past):
    return pl.pallas_call(
        functools.partial(_bias_row_kernel, past=past),
        in_specs=[pl.BlockSpec(memory_space=pltpu.SMEM)],
        out_shape=jax.ShapeDtypeStruct((N_HEADS, past + LANES), F32),
        name="bias_rows",
    )(rel_bias)


def _window_means(ext, u, pos):
    tm = u.shape[0]
    outs = []
    for gi, w in enumerate(POOL_WINDOWS):
        ch = slice(gi * POOL_GROUP_W, (gi + 1) * POOL_GROUP_W)
        s = ext[:, ch]
        span = 1
        while span < w:
            s = s[span:] + s[:-span]
            span *= 2
        off = POOL_HALO - (w - 1)
        win = s[off:off + tm]
        cnt = jnp.minimum(pos + 1, w).astype(F32)
        outs.append(win / cnt - u[:, ch])
    return jnp.concatenate(outs, axis=-1)


def _in_proj_kernel(x_ref, sh_ref, sc_ref, gmix_ref, w_ref, wkt_ref, qg_ref, kg_ref, bd_ref,
                    q_ref, kt_ref, v_ref, u_ref, g_ref, *rest, tm, seq_pool):
    h = _rms(x_ref[0]) * gmix_ref[...]
    h = h * (1.0 + sc_ref[0]) + sh_ref[0]
    hb = h.astype(BF16)

    def proj(c0, c1):
        return _dot(hb, w_ref[:, c0:c1])

    zq = proj(0, QK_W)
    msq = _dot((zq * zq).astype(BF16), bd_ref[...])
    q_ref[0] = (zq * lax.rsqrt(msq + EPS) * qg_ref[...]).astype(BF16)

    zk = lax.dot_general(wkt_ref[...], hb, _NT, preferred_element_type=F32)
    zk = zk.reshape(N_HEADS * 2, HEAD_DIM, tm)
    msk = jnp.mean(zk * zk, axis=1, keepdims=True)
    kt_ref[0] = (zk * lax.rsqrt(msk + EPS) * kg_ref[...]).reshape(QK_W, tm)

    c0 = 2 * QK_W
    zv = proj(c0, c0 + ATTN_W)
    for hd in range(N_HEADS):
        v_ref[0, pl.ds(hd, tm, stride=N_HEADS), :] = zv[:, hd * V_DIM:(hd + 1) * V_DIM]
    c0 += ATTN_W
    u = proj(c0, c0 + POOL_W)
    u_ref[0] = u
    c0 += POOL_W
    for c in range(0, 2 * D_MODEL, 512):
        g_ref[0, :, c:c + 512] = jax.nn.sigmoid(proj(c0 + c, c0 + c + 512)).astype(BF16)

    if seq_pool:
        d_ref, carry_ref = rest
        i = pl.program_id(1)

        @pl.when(i == 0)
        def _():
            carry_ref[...] = jnp.zeros_like(carry_ref)

        ext = jnp.concatenate([carry_ref[...], u], axis=0)
        carry_ref[...] = u[tm - POOL_HALO:, :]
        pos = i * tm + lax.broadcasted_iota(jnp.int32, (tm, 1), 0)
        d_ref[0] = _window_means(ext, u, pos).astype(BF16)


def _in_proj(x, mod, gmix, w_bf, wkt_bf, qg, kg, bd, *, tm, seq_pool):
    B, T, _ = x.shape
    nt = T // tm
    out_shape = [jax.ShapeDtypeStruct((B, T, QK_W), BF16),
                 jax.ShapeDtypeStruct((B, QK_W, T), F32),
                 jax.ShapeDtypeStruct((B, T * N_HEADS, V_DIM), F32),
                 jax.ShapeDtypeStruct((B, T, POOL_W), F32),
                 jax.ShapeDtypeStruct((B, T, 2 * D_MODEL), BF16)]
    out_specs = [_row_spec(tm, QK_W),
                 pl.BlockSpec((1, QK_W, tm), lambda b, i: (b, 0, i)),
                 _row_spec(tm * N_HEADS, V_DIM), _row_spec(tm, POOL_W), _row_spec(tm, 2 * D_MODEL)]
    scratch = []
    if seq_pool:
        out_shape.append(jax.ShapeDtypeStruct((B, T, POOL_W), BF16))
        out_specs.append(_row_spec(tm, POOL_W))
        scratch.append(pltpu.VMEM((POOL_HALO, POOL_W), F32))
    return pl.pallas_call(
        functools.partial(_in_proj_kernel, tm=tm, seq_pool=seq_pool),
        grid=(B, nt),
        in_specs=[_row_spec(tm, D_MODEL), _mod_spec(mod, tm, 0), _mod_spec(mod, tm, 1),
                  _const_spec(gmix), _const_spec(w_bf), _const_spec(wkt_bf),
                  _const_spec(qg), _const_spec(kg), _const_spec(bd)],
        out_specs=out_specs,
        out_shape=out_shape,
        scratch_shapes=scratch,
        compiler_params=_params(2),
        name="in_proj",
    )(x, mod, mod, gmix, w_bf, wkt_bf, qg, kg, bd)


def _diff_lambda(lp, layer):
    a = jnp.sum(lp[0:1] * lp[1:2], axis=-1, keepdims=True)
    b = jnp.sum(lp[2:3] * lp[3:4], axis=-1, keepdims=True)
    return jnp.exp(a) - jnp.exp(b) + _lambda_init(layer)


def _attn_kernel(qi_ref, kj_ref, q_ref, kt_ref, v_ref, bias_ref, rb_ref, lp_ref, subln_ref,
                 o_ref, m_ref, l_ref, acc_ref, s_ref, p_ref, alpha_ref, *, t, layer):
    s_idx = pl.program_id(1)
    i = qi_ref[s_idx]
    j = kj_ref[s_idx]

    @pl.when(j == 0)
    def _():
        m_ref[...] = jnp.full_like(m_ref, -jnp.inf)
        l_ref[...] = jnp.zeros_like(l_ref)
        acc_ref[...] = jnp.zeros_like(acc_ref)

    lane = lax.broadcasted_iota(jnp.int32, (1, V_DIM), 1)
    map0 = jnp.where(lane < HEAD_DIM, 1.0, 0.0).astype(BF16)
    map1 = jnp.where(lane >= HEAD_DIM, 1.0, 0.0).astype(BF16)
    far = (i - j) >= 2
    for h in range(N_HEADS):
        hs = slice(h * V_DIM, (h + 1) * V_DIM)
        q = q_ref[0, :, hs]
        kb = kt_ref[0, hs, :].astype(BF16)
        vb = jnp.concatenate([v_ref[0, pl.ds(h, t, stride=N_HEADS), :].astype(BF16),
                              jnp.ones((t, V_DIM), BF16)], axis=1)
        far_bias = rb_ref[N_BUCKETS - 1, h] * LOG2E
        for c in range(2 * t // ATTN_ROWS):
            qrow = (c * ATTN_ROWS) % t
            qc = q[qrow:qrow + ATTN_ROWS] * (map0 if c * ATTN_ROWS < t else map1)
            s_ref[...] = _dot(qc, kb)
            row0 = c * ATTN_ROWS

            def row_max(k, carry, h=h, qrow=qrow, row0=row0, far_bias=far_bias):
                r0 = pl.multiple_of(k * ATTN_SUB, ATTN_SUB)
                rows = pl.ds(row0 + r0, ATTN_SUB)
                bias = jnp.where(far, far_bias, bias_ref[h, 0, pl.ds(qrow + r0, ATTN_SUB), :])
                s = s_ref[pl.ds(r0, ATTN_SUB), :] + bias
                s_ref[pl.ds(r0, ATTN_SUB), :] = s
                m_prev = m_ref[h, rows, :]
                m_new = jnp.maximum(m_prev, jnp.max(s, axis=-1, keepdims=True))
                alpha_ref[pl.ds(r0, ATTN_SUB), :] = jnp.broadcast_to(jnp.exp2(m_prev - m_new),
                                                                     (ATTN_SUB, V_DIM))
                m_ref[h, rows, :] = m_new
                return carry

            def row_exp(k, carry, h=h, row0=row0):
                r0 = pl.multiple_of(k * ATTN_SUB, ATTN_SUB)
                m = m_ref[h, pl.ds(row0 + r0, ATTN_SUB), :]
                p_ref[pl.ds(r0, ATTN_SUB), :] = jnp.exp2(s_ref[pl.ds(r0, ATTN_SUB), :] - m).astype(BF16)
                return carry

            lax.fori_loop(0, ATTN_ROWS // ATTN_SUB, row_max, 0, unroll=ATTN_UNROLL)
            lax.fori_loop(0, ATTN_ROWS // ATTN_SUB, row_exp, 0, unroll=ATTN_UNROLL)
            blk = slice(row0, row0 + ATTN_ROWS)
            pv = _dot(p_ref[...], vb)
            alpha = alpha_ref[...]
            acc_ref[h, blk, :] = alpha * acc_ref[h, blk, :] + pv[:, :V_DIM]
            l_ref[h, blk, :] = alpha * l_ref[h, blk, :] + pv[:, V_DIM:]

    @pl.when(j == i)
    def _():
        lam = _diff_lambda(lp_ref[...], layer)
        for h in range(N_HEADS):
            o2 = acc_ref[h] / l_ref[h]
            o = o2[:t] - lam * o2[t:]
            o = _rms(o) * subln_ref[...] * (1.0 - _lambda_init(layer))
            o_ref[0, :, h * V_DIM:(h + 1) * V_DIM] = o.astype(BF16)


def _attn_prompt(q, kt, v, bias_tiles, rel_bias, lp, subln, *, layer, t):
    B, T, _ = q.shape
    n = T // t
    pairs = [(i, j) for i in range(n) for j in range(i + 1)]
    qi = jnp.asarray([p[0] for p in pairs], jnp.int32)
    kj = jnp.asarray([p[1] for p in pairs], jnp.int32)
    grid_spec = pltpu.PrefetchScalarGridSpec(
        num_scalar_prefetch=2,
        grid=(B, len(pairs)),
        in_specs=[pl.BlockSpec((1, t, QK_W), lambda b, s, qi, kj: (b, qi[s], 0)),
                  pl.BlockSpec((1, QK_W, t), lambda b, s, qi, kj: (b, 0, kj[s])),
                  pl.BlockSpec((1, t * N_HEADS, V_DIM), lambda b, s, qi, kj: (b, kj[s], 0)),
                  pl.BlockSpec((N_HEADS, 1, t, t),
                               lambda b, s, qi, kj: (0, jnp.minimum(qi[s] - kj[s], 1), 0, 0)),
                  pl.BlockSpec(memory_space=pltpu.SMEM),
                  pl.BlockSpec(lp.shape, lambda *_: (0, 0)),
                  pl.BlockSpec(subln.shape, lambda *_: (0, 0))],
        out_specs=pl.BlockSpec((1, t, ATTN_W), lambda b, s, qi, kj: (b, qi[s], 0)),
        scratch_shapes=[pltpu.VMEM((N_HEADS, 2 * t, 1), F32),
                        pltpu.VMEM((N_HEADS, 2 * t, V_DIM), F32),
                        pltpu.VMEM((N_HEADS, 2 * t, V_DIM), F32),
                        pltpu.VMEM((ATTN_ROWS, t), F32),
                        pltpu.VMEM((ATTN_ROWS, t), BF16),
                        pltpu.VMEM((ATTN_ROWS, V_DIM), F32)])
    return pl.pallas_call(
        functools.partial(_attn_kernel, t=t, layer=layer),
        grid_spec=grid_spec,
        out_shape=jax.ShapeDtypeStruct((B, T, ATTN_W), BF16),
        compiler_params=_params(2),
        name="attn_prompt",
    )(qi, kj, q, kt, v, bias_tiles, rel_bias, lp, subln)


def _attn_sample_kernel(pt_ref, q_ref, kn_ref, vn_ref, *refs, n_pages, page, layer):
    k_refs = refs[:n_pages]
    v_refs = refs[n_pages:2 * n_pages]
    bias_ref, lp_ref, subln_ref, o_ref = refs[2 * n_pages:]
    rows = 2 * N_HEADS
    row = lax.broadcasted_iota(jnp.int32, (rows, QK_W), 0)
    seg = lax.broadcasted_iota(jnp.int32, (rows, QK_W), 1) // HEAD_DIM
    q = jnp.broadcast_to(q_ref[0].astype(F32), (rows, QK_W))
    qbd_f = jnp.where(seg == (row % N_HEADS) * 2 + row // N_HEADS, q, 0.0)
    qbd = qbd_f.astype(BF16)

    s = jnp.concatenate([_dot(qbd, k_refs[p][0, 0].astype(BF16)) for p in range(n_pages)], axis=1)
    past = n_pages * page
    bias = bias_ref[...]
    bias = jnp.concatenate([bias, bias], axis=0)
    s = s + bias[:, :past]
    s_new = jnp.sum(qbd_f * kn_ref[0], axis=-1, keepdims=True) + bias[:, past:past + 1]
    m = jnp.maximum(jnp.max(s, axis=-1, keepdims=True), s_new)
    p = jnp.exp2(s - m)
    p_new = jnp.exp2(s_new - m)
    denom = jnp.sum(p, axis=-1, keepdims=True) + p_new
    p = p / denom
    p_new = p_new / denom

    lam = _diff_lambda(lp_ref[...], layer)
    first = lax.broadcasted_iota(jnp.int32, (rows, 1), 0) < N_HEADS
    a = jnp.where(first, p - lam * pltpu.roll(p, N_HEADS, 0), 0.0).astype(BF16)
    a_new = p_new[:N_HEADS] - lam * p_new[N_HEADS:]

    out = jnp.zeros((rows, ATTN_W), F32)
    for pg in range(n_pages):
        vp = jnp.concatenate([v_refs[pg][0, 0, pl.ds(h, page, stride=N_HEADS), :]
                              for h in range(N_HEADS)], axis=1)
        out = out + _dot(a[:, pg * page:(pg + 1) * page], vp.astype(BF16))
    out = out[:N_HEADS] + a_new * vn_ref[0]
    rowh = lax.broadcasted_iota(jnp.int32, (N_HEADS, ATTN_W), 0)
    head = lax.broadcasted_iota(jnp.int32, (N_HEADS, ATTN_W), 1) // V_DIM
    om = jnp.where(rowh == head, out, 0.0)
    ms = jnp.sum(om * om, axis=-1, keepdims=True) / V_DIM
    on = om * lax.rsqrt(ms + EPS)
    o = jnp.sum(on, axis=0, keepdims=True) * subln_ref[...] * (1.0 - _lambda_init(layer))
    o_ref[0] = o.astype(BF16)


def _attn_sample(q, k_new, v_new, cache_kt, cache_vr, page_table, bias_rows, lp, subln4, *, layer):
    DB = q.shape[0]
    n_pages = page_table.shape[1]
    page = cache_kt.shape[3]

    def page_spec(p):
        return pl.BlockSpec((1, 1, QK_W, page), lambda b, pt: (layer, pt[b, p], 0, 0))

    def vpage_spec(p):
        return pl.BlockSpec((1, 1, page * N_HEADS, V_DIM), lambda b, pt: (layer, pt[b, p], 0, 0))

    vec = pl.BlockSpec((1, 1, QK_W), lambda b, pt: (b, 0, 0))
    grid_spec = pltpu.PrefetchScalarGridSpec(
        num_scalar_prefetch=1,
        grid=(DB,),
        in_specs=([vec, vec, vec] + [page_spec(p) for p in range(n_pages)]
                  + [vpage_spec(p) for p in range(n_pages)]
                  + [pl.BlockSpec(bias_rows.shape, lambda *_: (0, 0)),
                     pl.BlockSpec(lp.shape, lambda *_: (0, 0)),
                     pl.BlockSpec(subln4.shape, lambda *_: (0, 0))]),
        out_specs=vec)
    return pl.pallas_call(
        functools.partial(_attn_sample_kernel, n_pages=n_pages, page=page, layer=layer),
        grid_spec=grid_spec,
        out_shape=jax.ShapeDtypeStruct((DB, 1, ATTN_W), BF16),
        compiler_params=_params(1),
        name="attn_sample",
    )(page_table, q, k_new, v_new, *([cache_kt] * n_pages), *([cache_vr] * n_pages),
      bias_rows, lp, subln4)


def _pool_sample_kernel(state_ref, u_ref, d_ref):
    u = u_ref[...]
    outs = []
    for gi, w in enumerate(POOL_WINDOWS):
        ch = slice(gi * POOL_GROUP_W, (gi + 1) * POOL_GROUP_W)
        s = u[:, ch]
        for r in range(POOL_BUF - (w - 1), POOL_BUF):
            s = s + state_ref[r, :, ch]
        outs.append(s / float(w) - u[:, ch])
    d_ref[...] = jnp.concatenate(outs, axis=-1).astype(BF16)


def _pool_sample(state_t, u):
    return pl.pallas_call(
        _pool_sample_kernel,
        out_shape=jax.ShapeDtypeStruct(u.shape, BF16),
        name="pool_sample",
    )(state_t, u)


def _mix_kernel(o_ref, d_ref, g_ref, x_ref, gt_ref, sh_ref, sc_ref, wp_ref, ps_ref, wb_ref, wo_ref,
                gffn_ref, *rest, router):
    d = d_ref[0]
    y = jnp.concatenate([_dot(d[:, g * POOL_GROUP_W:(g + 1) * POOL_GROUP_W], wp_ref[g])
                         for g in range(len(POOL_WINDOWS))], axis=-1)
    y = (y * ps_ref[...]).astype(BF16)
    pa = _dot(o_ref[0], wb_ref[:ATTN_W])
    pb = _dot(y, wb_ref[ATTN_W:])
    g = g_ref[0]
    merged = g[:, :D_MODEL].astype(F32) * pa + g[:, D_MODEL:].astype(F32) * pb
    xn = x_ref[0] + gt_ref[0] * _dot(merged.astype(BF16), wo_ref[...])
    h2 = _rms(xn) * gffn_ref[...]
    h2 = h2 * (1.0 + sc_ref[0]) + sh_ref[0]
    if not router:
        xn_ref, h2_ref = rest
        xn_ref[0] = xn
        h2_ref[0] = h2.astype(h2_ref.dtype)
        return

    wr_hi_ref, wr_lo_ref, xn_ref, h2_ref, ri_ref, rp_ref = rest
    xn_ref[0] = xn
    h2_ref[0] = h2
    hi = h2.astype(BF16)
    lo = (h2 - hi.astype(F32)).astype(BF16)
    logits = _dot(hi, wr_hi_ref[...]) + _dot(lo, wr_hi_ref[...]) + _dot(hi, wr_lo_ref[...])
    lane = lax.broadcasted_iota(jnp.int32, logits.shape, 1)
    lg = jnp.where(lane < N_EXPERTS, logits, -jnp.inf)
    m1 = jnp.max(lg, axis=-1, keepdims=True)
    i1 = jnp.min(jnp.where(lg == m1, lane, LANES), axis=-1, keepdims=True)
    lg = jnp.where(lane == i1, -jnp.inf, lg)
    m2 = jnp.max(lg, axis=-1, keepdims=True)
    i2 = jnp.min(jnp.where(lg == m2, lane, LANES), axis=-1, keepdims=True)
    e = jnp.exp(m2 - m1)
    ri_ref[0] = jnp.concatenate([i1, i2], axis=-1)
    rp_ref[0] = jnp.concatenate([1.0 / (1.0 + e), e / (1.0 + e)], axis=-1)


def _mix(o, d, g, x, mod, wp_bf, pscale, wb_bf, wo_bf, gffn, router_w, *, tm):
    B, T, _ = x.shape
    router = router_w is not None
    ins = [o, d, g, x, mod, mod, mod, wp_bf, pscale, wb_bf, wo_bf, gffn]
    in_specs = [_row_spec(tm, ATTN_W), _row_spec(tm, POOL_W), _row_spec(tm, 2 * D_MODEL),
                _row_spec(tm, D_MODEL), _mod_spec(mod, tm, 2), _mod_spec(mod, tm, 3),
                _mod_spec(mod, tm, 4)] + [_const_spec(a) for a in ins[7:]]
    out_shape = [jax.ShapeDtypeStruct((B, T, D_MODEL), F32),
                 jax.ShapeDtypeStruct((B, T, D_MODEL), F32 if router else BF16)]
    out_specs = [_row_spec(tm, D_MODEL), _row_spec(tm, D_MODEL)]
    if router:
        ins += list(router_w)
        in_specs += [_const_spec(a) for a in router_w]
        out_shape += [jax.ShapeDtypeStruct((B, T, TOP_K), jnp.int32),
                      jax.ShapeDtypeStruct((B, T, TOP_K), F32)]
        out_specs += [_row_spec(tm, TOP_K), _row_spec(tm, TOP_K)]
    return pl.pallas_call(
        functools.partial(_mix_kernel, router=router),
        grid=(B, T // tm),
        in_specs=in_specs,
        out_specs=out_specs,
        out_shape=out_shape,
        compiler_params=_params(2),
        name="mix",
    )(*ins)


def _ffn_kernel(h_ref, x_ref, gt_ref, wg_ref, wu_ref, wd_ref, y_ref, acc_ref):
    f = pl.program_id(2)

    @pl.when(f == 0)
    def _():
        acc_ref[...] = jnp.zeros_like(acc_ref)

    h = h_ref[0]
    a = (_silu(_dot(h, wg_ref[...])) * _dot(h, wu_ref[...])).astype(BF16)
    acc_ref[...] += _dot(a, wd_ref[...])

    @pl.when(f == pl.num_programs(2) - 1)
    def _():
        y_ref[0] = x_ref[0] + gt_ref[0] * acc_ref[...]


def _ffn_dense(h2, x, mod, wg_bf, wu_bf, wd_bf, *, tm):
    B, T, _ = x.shape
    d_ff = wg_bf.shape[1]
    tf = d_ff // 2
    return pl.pallas_call(
        _ffn_kernel,
        grid=(B, T // tm, d_ff // tf),
        in_specs=[_row_spec(tm, D_MODEL), _row_spec(tm, D_MODEL), _mod_spec(mod, tm, 5),
                  pl.BlockSpec((D_MODEL, tf), lambda b, i, f: (0, f)),
                  pl.BlockSpec((D_MODEL, tf), lambda b, i, f: (0, f)),
                  pl.BlockSpec((tf, D_MODEL), lambda b, i, f: (f, 0))],
        out_specs=_row_spec(tm, D_MODEL),
        out_shape=jax.ShapeDtypeStruct((B, T, D_MODEL), F32),
        scratch_shapes=[pltpu.VMEM((tm, D_MODEL), F32)],
        compiler_params=_params(3),
        name="ffn_dense",
    )(h2, x, mod, wg_bf, wu_bf, wd_bf)


def _row_copy(src_hbm, src_row, dst, dst_row, sem):
    return pltpu.make_async_copy(src_hbm.at[pl.ds(src_row, 1), :], dst.at[pl.ds(dst_row, 1), :], sem)


def _moe_ffn_kernel(te_ref, nu_ref, src_ref, src_next_ref, h_hbm, wg_ref, wu_ref, wd_ref, y_ref,
                    xbuf, acc_ref, sem, *, tm):
    i = pl.program_id(0)
    f = pl.program_id(1)
    nu = nu_ref[0]
    slot = i % 2

    def issue(idx_ref, dst):
        def body(r, carry):
            _row_copy(h_hbm, idx_ref[0, 0, r], xbuf.at[dst], r, sem.at[dst]).start()
            return carry
        lax.fori_loop(0, tm, body, 0, unroll=8)

    def drain(dst):
        def body(r, carry):
            _row_copy(h_hbm, 0, xbuf.at[dst], 0, sem.at[dst]).wait()
            return carry
        lax.fori_loop(0, tm, body, 0, unroll=8)

    @pl.when(i < nu)
    def _():
        @pl.when(f == 0)
        def _():
            @pl.when(i == 0)
            def _():
                issue(src_ref, 0)

            drain(slot)

            @pl.when(i + 1 < nu)
            def _():
                issue(src_next_ref, 1 - slot)

            acc_ref[...] = jnp.zeros_like(acc_ref)

        xb = xbuf[slot].astype(BF16)
        a = (_silu(_dot(xb, wg_ref[0].astype(BF16))) * _dot(xb, wu_ref[0].astype(BF16))).astype(BF16)
        acc_ref[...] += _dot(a, wd_ref[0].astype(BF16))

        @pl.when(f == pl.num_programs(1) - 1)
        def _():
            y_ref[...] = acc_ref[...]

    @pl.when(i >= nu)
    def _():
        y_ref[...] = jnp.zeros_like(y_ref)


def _moe_ffn(h_rows, src, tile_expert, n_used, wg, wu, wd, *, tm, tf):
    n_tiles = src.shape[0]
    d_ff = wg.shape[2]
    assert d_ff % tf == 0, (d_ff, tf)
    nf = d_ff // tf

    def tile(i, nu):
        return jnp.minimum(i, nu[0] - 1)

    def ff(i, f, nu):
        return jnp.where(i < nu[0], f, nf - 1)

    grid_spec = pltpu.PrefetchScalarGridSpec(
        num_scalar_prefetch=2,
        grid=(n_tiles, nf),
        in_specs=[pl.BlockSpec((1, 1, tm), lambda i, f, te, nu: (i, 0, 0), memory_space=pltpu.SMEM),
                  pl.BlockSpec((1, 1, tm), lambda i, f, te, nu: (jnp.minimum(i + 1, n_tiles - 1), 0, 0),
                               memory_space=pltpu.SMEM),
                  pl.BlockSpec(memory_space=pl.ANY),
                  pl.BlockSpec((1, D_MODEL, tf), lambda i, f, te, nu: (te[tile(i, nu)], 0, ff(i, f, nu))),
                  pl.BlockSpec((1, D_MODEL, tf), lambda i, f, te, nu: (te[tile(i, nu)], 0, ff(i, f, nu))),
                  pl.BlockSpec((1, tf, D_MODEL), lambda i, f, te, nu: (te[tile(i, nu)], ff(i, f, nu), 0))],
        out_specs=pl.BlockSpec((tm, D_MODEL), lambda i, f, te, nu: (i, 0)),
        scratch_shapes=[pltpu.VMEM((2, tm, D_MODEL), F32), pltpu.VMEM((tm, D_MODEL), F32),
                        pltpu.SemaphoreType.DMA((2,))])
    return pl.pallas_call(
        functools.partial(_moe_ffn_kernel, tm=tm),
        grid_spec=grid_spec,
        out_shape=jax.ShapeDtypeStruct((n_tiles * tm, D_MODEL), F32),
        compiler_params=_params(2),
        name="moe_ffn",
    )(tile_expert, n_used, src, src, h_rows, wg, wu, wd)


def _combine_kernel(pos_ref, pos_next_ref, y_hbm, x_ref, gt_ref, p_ref, o_ref, buf, sem, *, tc):
    g = pl.program_id(0) * pl.num_programs(1) + pl.program_id(1)
    n_steps = pl.num_programs(0) * pl.num_programs(1)
    slot = g % 2

    def issue(idx_ref, dst):
        def body(t, carry):
            for k in range(TOP_K):
                _row_copy(y_hbm, idx_ref[0, 0, TOP_K * t + k], buf.at[dst, k], t, sem.at[dst]).start()
            return carry
        lax.fori_loop(0, tc, body, 0, unroll=4)

    @pl.when(g == 0)
    def _():
        issue(pos_ref, 0)

    def drain(t, carry):
        for k in range(TOP_K):
            _row_copy(y_hbm, 0, buf.at[slot, 0], 0, sem.at[slot]).wait()
        return carry

    lax.fori_loop(0, tc, drain, 0, unroll=4)

    @pl.when(g + 1 < n_steps)
    def _():
        issue(pos_next_ref, 1 - slot)

    p = p_ref[0]
    f = p[:, 0:1] * buf[slot, 0] + p[:, 1:2] * buf[slot, 1]
    o_ref[0] = x_ref[0] + gt_ref[0] * f


def _combine(y, pos, probs, x, mod, *, tc):
    B, T, _ = x.shape
    nt = T // tc
    last = B * nt - 1
    pos_chunks = pos.reshape(B * nt, 1, TOP_K * tc)
    return pl.pallas_call(
        functools.partial(_combine_kernel, tc=tc),
        grid=(B, nt),
        in_specs=[pl.BlockSpec((1, 1, TOP_K * tc), lambda b, i: (b * nt + i, 0, 0),
                               memory_space=pltpu.SMEM),
                  pl.BlockSpec((1, 1, TOP_K * tc), lambda b, i: (jnp.minimum(b * nt + i + 1, last), 0, 0),
                               memory_space=pltpu.SMEM),
                  pl.BlockSpec(memory_space=pl.ANY),
                  _row_spec(tc, D_MODEL), _mod_spec(mod, tc, 5), _row_spec(tc, TOP_K)],
        out_specs=_row_spec(tc, D_MODEL),
        out_shape=jax.ShapeDtypeStruct(x.shape, F32),
        scratch_shapes=[pltpu.VMEM((2, TOP_K, tc, D_MODEL), F32), pltpu.SemaphoreType.DMA((2,))],
        compiler_params=_params(2),
        name="moe_combine",
    )(pos_chunks, pos_chunks, y, x, mod, probs)


def _route_plan(top_i, tm, n_tiles):
    e_flat = top_i.reshape(-1)
    onehot = (e_flat[:, None] == jnp.arange(N_EXPERTS, dtype=jnp.int32)[None, :]).astype(jnp.int32)
    csum = jnp.cumsum(onehot, axis=0)
    cnt = csum[-1]
    gsz = (cnt + tm - 1) // tm * tm
    gend = jnp.cumsum(gsz)
    gstart = gend - gsz
    pos = jnp.sum(onehot * (gstart[None, :] + csum - 1), axis=1).astype(jnp.int32)
    n_used = (gend[-1] // tm).astype(jnp.int32).reshape(1)
    tile_start = jnp.arange(n_tiles, dtype=jnp.int32) * tm
    tile_expert = jnp.sum((tile_start[:, None] >= gend[None, :]).astype(jnp.int32), axis=1)
    tile_expert = jnp.minimum(tile_expert, N_EXPERTS - 1).astype(jnp.int32)
    token = jnp.arange(e_flat.shape[0], dtype=jnp.int32) // TOP_K
    src = jnp.zeros((n_tiles * tm,), jnp.int32).at[pos].set(token, unique_indices=True)
    return pos.reshape(top_i.shape), src.reshape(n_tiles, 1, tm), tile_expert, n_used


def _moe(groups, wg, wu, wd):
    tm = MOE_ROW_TILE
    counts = [g[0].shape[0] * g[0].shape[1] for g in groups]
    total = sum(counts) * TOP_K
    n_tiles = (total + N_EXPERTS * (tm - 1)) // tm + 1
    top_all = jnp.concatenate([g[3].reshape(-1, TOP_K) for g in groups], axis=0)
    h_all = jnp.concatenate([g[0].reshape(-1, D_MODEL) for g in groups], axis=0)
    pos_all, src, tile_expert, n_used = _route_plan(top_all, tm, n_tiles)
    y = _moe_ffn(h_all, src, tile_expert, n_used, wg, wu, wd, tm=tm, tf=MOE_FF_TILE)
    outs = []
    offs = 0
    for (h2, x, mod, top_i, probs), cnt in zip(groups, counts):
        pos = pos_all[offs:offs + cnt].reshape(top_i.shape)
        offs += cnt
        outs.append(_combine(y, pos, probs, x, mod, tc=min(MOE_DMA_CHUNK, x.shape[1])))
    return outs


def kernel(x_prompt, x_sample, c_prompt, c_sample, cache_k, cache_v, state_pool, page_table,
           rel_bias, w_ada, b_ada, g_mix, g_ffn, w_in, q_norm, k_norm, diff_lambda, subln,
           w_pool, pool_scale, w_branch, w_out, w_ff_gate, w_ff_up, w_ff_down, w_router,
           w_exp_gate, w_exp_up, w_exp_down):
    B, S, D = x_prompt.shape
    DB = x_sample.shape[0]
    depth = w_in.shape[0]
    n_phys, page = cache_k.shape[1], cache_k.shape[2]
    past = page_table.shape[1] * page

    mod_all = _ada_mod(jnp.concatenate([c_prompt, c_sample], axis=0), w_ada, b_ada)
    bias_tiles = _bias_tiles(rel_bias, ATTN_TILE)
    bias_rows = _bias_rows(rel_bias, past)

    cache_kt = jnp.transpose(cache_k, (0, 1, 3, 4, 5, 2)).reshape(depth, n_phys, QK_W, page)
    cache_vr = cache_v.reshape(depth, n_phys, page * N_HEADS, V_DIM)
    state_t = jnp.transpose(state_pool, (0, 2, 1, 3))

    bd = jnp.kron(jnp.eye(QK_W // HEAD_DIM, dtype=F32),
                  jnp.full((HEAD_DIM, HEAD_DIM), 1.0 / HEAD_DIM, F32)).astype(BF16)

    xp = x_prompt
    xs = x_sample.reshape(1, DB, D)
    outs = {k: [] for k in ("kp", "vp", "pp", "ks", "vs", "ps")}
    for l in range(depth):
        mod_p = mod_all[l, :B].reshape(B, 1, 6 * D)
        mod_s = mod_all[l, B:].reshape(1, DB, 6 * D)
        w_bf = w_in[l].astype(BF16)
        wkt_bf = w_in[l][:, QK_W:2 * QK_W].T.astype(BF16)
        qg = jnp.tile(q_norm[l].reshape(1, 2 * HEAD_DIM), (1, N_HEADS)) * (HEAD_DIM ** -0.5 * LOG2E)
        kg = jnp.tile(k_norm[l], (N_HEADS, 1)).reshape(N_HEADS * 2, HEAD_DIM, 1)
        gmix = g_mix[l].reshape(1, D)
        gffn = g_ffn[l].reshape(1, D)
        lp = diff_lambda[l]
        sub1 = subln[l].reshape(1, V_DIM)
        sub4 = jnp.tile(sub1, (1, N_HEADS))
        wp_bf = w_pool[l].astype(BF16)
        pscale = pool_scale[l].reshape(1, POOL_W)
        wb_bf = w_branch[l].astype(BF16)
        wo_bf = w_out[l].astype(BF16)
        moe = l % 2 == 1
        j = l // 2
        router_w = None
        if moe:
            wr = jnp.pad(w_router[j], ((0, 0), (0, LANES - N_EXPERTS)))
            wr_hi = wr.astype(BF16)
            router_w = (wr_hi, (wr - wr_hi.astype(F32)).astype(BF16))

        qp, ktp, vp, up, gp, dp = _in_proj(xp, mod_p, gmix, w_bf, wkt_bf, qg, kg, bd,
                                           tm=ROW_TILE, seq_pool=True)
        op = _attn_prompt(qp, ktp, vp, bias_tiles, rel_bias, lp, sub1, layer=l, t=ATTN_TILE)
        mix_p = _mix(op, dp, gp, xp, mod_p, wp_bf, pscale, wb_bf, wo_bf, gffn, router_w, tm=ROW_TILE)

        qs, kts, vs, us, gs = _in_proj(xs, mod_s, gmix, w_bf, wkt_bf, qg, kg, bd,
                                       tm=DB, seq_pool=False)
        ks_rows = jnp.transpose(kts, (0, 2, 1)).reshape(DB, 1, QK_W)
        osamp = _attn_sample(qs.reshape(DB, 1, QK_W), ks_rows, vs.reshape(DB, 1, ATTN_W),
                             cache_kt, cache_vr, page_table, bias_rows, lp, sub4, layer=l)
        ds = _pool_sample(state_t[l], us[0])
        mix_s = _mix(osamp.reshape(1, DB, ATTN_W), ds.reshape(1, DB, POOL_W), gs, xs, mod_s,
                     wp_bf, pscale, wb_bf, wo_bf, gffn, router_w, tm=DB)

        if moe:
            xp, xs = _moe([(mix_p[1], mix_p[0], mod_p, mix_p[2], mix_p[3]),
                           (mix_s[1], mix_s[0], mod_s, mix_s[2], mix_s[3])],
                          w_exp_gate[j], w_exp_up[j], w_exp_down[j])
        else:
            wg_bf = w_ff_gate[j].astype(BF16)
            wu_bf = w_ff_up[j].astype(BF16)
            wd_bf = w_ff_down[j].astype(BF16)
            xp = _ffn_dense(mix_p[1], mix_p[0], mod_p, wg_bf, wu_bf, wd_bf, tm=FFN_ROW_TILE)
            xs = _ffn_dense(mix_s[1], mix_s[0], mod_s, wg_bf, wu_bf, wd_bf, tm=DB)

        outs["kp"].append(jnp.transpose(ktp.reshape(B, N_HEADS, 2, HEAD_DIM, S), (0, 4, 1, 2, 3)))
        outs["vp"].append(vp.reshape(B, S, N_HEADS, V_DIM))
        outs["pp"].append(up[:, S - POOL_BUF:, :])
        outs["ks"].append(ks_rows.reshape(DB, 1, N_HEADS, 2, HEAD_DIM))
        outs["vs"].append(vs.reshape(DB, 1, N_HEADS, V_DIM))
        outs["ps"].append(jnp.concatenate([state_pool[l][:, 1:], us.reshape(DB, 1, POOL_W)], axis=1))

    return (xp, xs.reshape(DB, 1, D),
            jnp.stack(outs["kp"]), jnp.stack(outs["vp"]), jnp.stack(outs["pp"]),
            jnp.stack(outs["ks"]), jnp.stack(outs["vs"]), jnp.stack(outs["ps"]))
```

```python
import functools
import math

import jax
import jax.numpy as jnp
from jax import lax
from jax.experimental import pallas as pl
from jax.experimental.pallas import tpu as pltpu

F32 = jnp.float32
BF16 = jnp.bfloat16

D_MODEL = 1024
N_HEADS = 4
HEAD_DIM = 64
V_DIM = 2 * HEAD_DIM
QK_W = N_HEADS * 2 * HEAD_DIM
ATTN_W = N_HEADS * V_DIM
POOL_WINDOWS = (2, 4, 8, 16)
POOL_W = D_MODEL // 2
POOL_GROUP_W = POOL_W // len(POOL_WINDOWS)
POOL_BUF = max(POOL_WINDOWS) - 1
POOL_HALO = max(POOL_WINDOWS)
IN_COLS = 2 * QK_W + ATTN_W + POOL_W + 2 * D_MODEL
N_BUCKETS = 32
MAX_DISTANCE = 128
N_EXPERTS = 8
TOP_K = 2
EPS = 1e-6
LOG2E = math.log2(math.e)
LANES = 128
V7X_VMEM_BYTES = 64 * 1024 * 1024
VMEM_LIMIT = V7X_VMEM_BYTES * 7 // 8

ATTN_TILE = 512
ATTN_ROWS = 256
ATTN_SUB = 32
ATTN_UNROLL = ATTN_ROWS // ATTN_SUB
ROW_TILE = 512
FFN_ROW_TILE = 1024
MOE_ROW_TILE = 1024
MOE_FF_TILE = 512
MOE_DMA_CHUNK = 256

_NT = (((1,), (1,)), ((), ()))


def _lambda_init(layer):
    return 0.8 - 0.6 * math.exp(-0.3 * layer)


def _params(n_axes, vmem=VMEM_LIMIT):
    return pltpu.CompilerParams(dimension_semantics=("arbitrary",) * n_axes,
                                vmem_limit_bytes=vmem)


def _dot(a, b):
    return jnp.dot(a, b, preferred_element_type=F32)


def _rms(x):
    return x * lax.rsqrt(jnp.mean(x * x, axis=-1, keepdims=True) + EPS)


def _silu(x):
    return x * jax.nn.sigmoid(x)


def _const_spec(a):
    nd = a.ndim
    return pl.BlockSpec(a.shape, lambda *_: (0,) * nd)


def _row_spec(tm, width):
    return pl.BlockSpec((1, tm, width), lambda b, i, *_: (b, i, 0))


def _mod_spec(mod, tm, chunk):
    if mod.shape[1] == 1:
        return pl.BlockSpec((1, 1, D_MODEL), lambda b, i, *_: (b, 0, chunk))
    return pl.BlockSpec((1, tm, D_MODEL), lambda b, i, *_: (b, i, chunk))


def _ada_kernel(c_ref, w_ref, b_ref, o_ref):
    a = _silu(c_ref[...]).astype(BF16)
    o_ref[0] = _dot(a, w_ref[0].astype(BF16)) + b_ref[0]


def _ada_mod(c_all, w_ada, b_ada):
    rows = c_all.shape[0]
    depth, _, cols = w_ada.shape
    tn = cols // 4
    return pl.pallas_call(
        _ada_kernel,
        grid=(depth, cols // tn),
        in_specs=[pl.BlockSpec((rows, D_MODEL), lambda l, n: (0, 0)),
                  pl.BlockSpec((1, D_MODEL, tn), lambda l, n: (l, 0, n)),
                  pl.BlockSpec((1, 1, tn), lambda l, n: (l, 0, n))],
        out_specs=pl.BlockSpec((1, rows, tn), lambda l, n: (l, 0, n)),
        out_shape=jax.ShapeDtypeStruct((depth, rows, cols), F32),
        compiler_params=_params(2),
        name="ada_mod",
    )(c_all, w_ada, b_ada.reshape(depth, 1, cols))


def _t5_bucket(rel):
    n = jnp.maximum(rel, 0)
    max_exact = N_BUCKETS // 2
    nf = jnp.maximum(n, 1).astype(F32)
    large = max_exact + (jnp.log(nf / max_exact) / math.log(MAX_DISTANCE / max_exact)
                         * (N_BUCKETS - max_exact)).astype(jnp.int32)
    large = jnp.minimum(large, N_BUCKETS - 1)
    return jnp.where(n < max_exact, n, large)


def _bias_lookup(bucket, rb_ref, h):
    acc = jnp.zeros(bucket.shape, F32)
    for b in range(N_BUCKETS):
        acc = jnp.where(bucket == b, rb_ref[b, h], acc)
    return acc


def _bias_tile_kernel(rb_ref, o_ref, *, t):
    h = pl.program_id(0)
    d = pl.program_id(1)
    r = lax.broadcasted_iota(jnp.int32, (t, t), 0)
    c = lax.broadcasted_iota(jnp.int32, (t, t), 1)
    rel = d * t + r - c
    val = _bias_lookup(_t5_bucket(rel), rb_ref, h)
    o_ref[0, 0] = jnp.where(rel >= 0, val * LOG2E, -jnp.inf)


def _bias_tiles(rel_bias, t):
    return pl.pallas_call(
        functools.partial(_bias_tile_kernel, t=t),
        grid=(N_HEADS, 2),
        in_specs=[pl.BlockSpec(memory_space=pltpu.SMEM)],
        out_specs=pl.BlockSpec((1, 1, t, t), lambda h, d: (h, d, 0, 0)),
        out_shape=jax.ShapeDtypeStruct((N_HEADS, 2, t, t), F32),
        compiler_params=_params(2),
        name="bias_tiles",
    )(rel_bias)


def _bias_row_kernel(rb_ref, o_ref, *, past):
    width = o_ref.shape[1]
    c = lax.broadcasted_iota(jnp.int32, (1, width), 1)
    bucket = _t5_bucket(jnp.maximum(past - c, 0))
    for h in range(N_HEADS):
        o_ref[h:h + 1, :] = _bias_lookup(bucket, rb_ref, h) * LOG2E


def _bias_rows(rel_bias, past):
    return pl.pallas_call(
        functools.partial(_bias_row_kernel, past=past),
        in_specs=[pl.BlockSpec(memory_space=pltpu.SMEM)],
        out_shape=jax.ShapeDtypeStruct((N_HEADS, past + LANES), F32),
        name="bias_rows",
    )(rel_bias)


def _window_means(ext, u, pos):
    tm = u.shape[0]
    outs = []
    for gi, w in enumerate(POOL_WINDOWS):
        ch = slice(gi * POOL_GROUP_W, (gi + 1) * POOL_GROUP_W)
        s = ext[:, ch]
        span = 1
        while span < w:
            s = s[span:] + s[:-span]
            span *= 2
        off = POOL_HALO - (w - 1)
        win = s[off:off + tm]
        cnt = jnp.minimum(pos + 1, w).astype(F32)
        outs.append(win / cnt - u[:, ch])
    return jnp.concatenate(outs, axis=-1)


def _in_proj_kernel(x_ref, sh_ref, sc_ref, gmix_ref, w_ref, wkt_ref, qg_ref, kg_ref, bd_ref,
                    *rest, tm, seq_pool, n_prev):
    prev_k, prev_v = rest[:n_prev], rest[n_prev:2 * n_prev]
    q_ref, kt_ref, v_ref, u_ref, g_ref = rest[2 * n_prev:2 * n_prev + 5]
    rest = rest[2 * n_prev + 5:]
    for p in range(n_prev):
        kt_ref[p, 0] = prev_k[p][0, 0]
        v_ref[p, 0] = prev_v[p][0, 0]
    h = _rms(x_ref[0]) * gmix_ref[...]
    h = h * (1.0 + sc_ref[0]) + sh_ref[0]
    hb = h.astype(BF16)

    def proj(c0, c1):
        return _dot(hb, w_ref[:, c0:c1])

    zq = proj(0, QK_W)
    msq = _dot((zq * zq).astype(BF16), bd_ref[...])
    q_ref[0] = (zq * lax.rsqrt(msq + EPS) * qg_ref[...]).astype(BF16)

    zk = lax.dot_general(wkt_ref[...], hb, _NT, preferred_element_type=F32)
    zk = zk.reshape(N_HEADS * 2, HEAD_DIM, tm)
    msk = jnp.mean(zk * zk, axis=1, keepdims=True)
    kt_ref[n_prev, 0] = (zk * lax.rsqrt(msk + EPS) * kg_ref[...]).reshape(QK_W, tm)

    c0 = 2 * QK_W
    zv = proj(c0, c0 + ATTN_W)
    for hd in range(N_HEADS):
        v_ref[n_prev, 0, pl.ds(hd, tm, stride=N_HEADS), :] = zv[:, hd * V_DIM:(hd + 1) * V_DIM]
    c0 += ATTN_W
    u = proj(c0, c0 + POOL_W)
    u_ref[0] = u
    c0 += POOL_W
    for c in range(0, 2 * D_MODEL, 512):
        g_ref[0, :, c:c + 512] = jax.nn.sigmoid(proj(c0 + c, c0 + c + 512)).astype(BF16)

    if seq_pool:
        d_ref, carry_ref = rest
        i = pl.program_id(1)

        @pl.when(i == 0)
        def _():
            carry_ref[...] = jnp.zeros_like(carry_ref)

        ext = jnp.concatenate([carry_ref[...], u], axis=0)
        carry_ref[...] = u[tm - POOL_HALO:, :]
        pos = i * tm + lax.broadcasted_iota(jnp.int32, (tm, 1), 0)
        d_ref[0] = _window_means(ext, u, pos).astype(BF16)


def _in_proj(x, mod, gmix, w_bf, wkt_bf, qg, kg, bd, prev_k=(), prev_v=(), *, tm, seq_pool):
    B, T, _ = x.shape
    nt = T // tm
    n_prev = len(prev_k)
    k_block = lambda n: pl.BlockSpec((n, 1, QK_W, tm), lambda b, i: (0, b, 0, i))
    v_block = lambda n: pl.BlockSpec((n, 1, tm * N_HEADS, V_DIM), lambda b, i: (0, b, i, 0))
    out_shape = [jax.ShapeDtypeStruct((B, T, QK_W), BF16),
                 jax.ShapeDtypeStruct((n_prev + 1, B, QK_W, T), F32),
                 jax.ShapeDtypeStruct((n_prev + 1, B, T * N_HEADS, V_DIM), F32),
                 jax.ShapeDtypeStruct((B, T, POOL_W), F32),
                 jax.ShapeDtypeStruct((B, T, 2 * D_MODEL), BF16)]
    out_specs = [_row_spec(tm, QK_W), k_block(n_prev + 1), v_block(n_prev + 1),
                 _row_spec(tm, POOL_W), _row_spec(tm, 2 * D_MODEL)]
    scratch = []
    if seq_pool:
        out_shape.append(jax.ShapeDtypeStruct((B, T, POOL_W), BF16))
        out_specs.append(_row_spec(tm, POOL_W))
        scratch.append(pltpu.VMEM((POOL_HALO, POOL_W), F32))
    return pl.pallas_call(
        functools.partial(_in_proj_kernel, tm=tm, seq_pool=seq_pool, n_prev=n_prev),
        grid=(B, nt),
        in_specs=[_row_spec(tm, D_MODEL), _mod_spec(mod, tm, 0), _mod_spec(mod, tm, 1),
                  _const_spec(gmix), _const_spec(w_bf), _const_spec(wkt_bf),
                  _const_spec(qg), _const_spec(kg), _const_spec(bd)]
                 + [k_block(1)] * n_prev + [v_block(1)] * n_prev,
        out_specs=out_specs,
        out_shape=out_shape,
        scratch_shapes=scratch,
        compiler_params=_params(2),
        name="in_proj",
    )(x, mod, mod, gmix, w_bf, wkt_bf, qg, kg, bd, *prev_k, *prev_v)


def _diff_lambda(lp, layer):
    a = jnp.sum(lp[0:1] * lp[1:2], axis=-1, keepdims=True)
    b = jnp.sum(lp[2:3] * lp[3:4], axis=-1, keepdims=True)
    return jnp.exp(a) - jnp.exp(b) + _lambda_init(layer)


def _attn_kernel(qi_ref, kj_ref, q_ref, kt_ref, v_ref, bias_ref, rb_ref, lp_ref, subln_ref,
                 o_ref, m_ref, l_ref, acc_ref, s_ref, p_ref, alpha_ref, *, t, layer):
    s_idx = pl.program_id(1)
    i = qi_ref[s_idx]
    j = kj_ref[s_idx]

    @pl.when(j == 0)
    def _():
        m_ref[...] = jnp.full_like(m_ref, -jnp.inf)
        l_ref[...] = jnp.zeros_like(l_ref)
        acc_ref[...] = jnp.zeros_like(acc_ref)

    lane = lax.broadcasted_iota(jnp.int32, (1, V_DIM), 1)
    map0 = jnp.where(lane < HEAD_DIM, 1.0, 0.0).astype(BF16)
    map1 = jnp.where(lane >= HEAD_DIM, 1.0, 0.0).astype(BF16)
    far = (i - j) >= 2
    for h in range(N_HEADS):
        hs = slice(h * V_DIM, (h + 1) * V_DIM)
        q = q_ref[0, :, hs]
        kb = kt_ref[0, 0, hs, :].astype(BF16)
        vb = jnp.concatenate([v_ref[0, 0, pl.ds(h, t, stride=N_HEADS), :].astype(BF16),
                              jnp.ones((t, V_DIM), BF16)], axis=1)
        far_bias = rb_ref[N_BUCKETS - 1, h] * LOG2E
        for c in range(2 * t // ATTN_ROWS):
            qrow = (c * ATTN_ROWS) % t
            qc = q[qrow:qrow + ATTN_ROWS] * (map0 if c * ATTN_ROWS < t else map1)
            s_ref[...] = _dot(qc, kb)
            row0 = c * ATTN_ROWS

            def row_max(k, carry, h=h, qrow=qrow, row0=row0, far_bias=far_bias):
                r0 = pl.multiple_of(k * ATTN_SUB, ATTN_SUB)
                rows = pl.ds(row0 + r0, ATTN_SUB)
                bias = jnp.where(far, far_bias, bias_ref[h, 0, pl.ds(qrow + r0, ATTN_SUB), :])
                s = s_ref[pl.ds(r0, ATTN_SUB), :] + bias
                s_ref[pl.ds(r0, ATTN_SUB), :] = s
                m_prev = m_ref[h, rows, :]
                m_new = jnp.maximum(m_prev, jnp.max(s, axis=-1, keepdims=True))
                alpha_ref[pl.ds(r0, ATTN_SUB), :] = jnp.broadcast_to(jnp.exp2(m_prev - m_new),
                                                                     (ATTN_SUB, V_DIM))
                m_ref[h, rows, :] = m_new
                return carry

            def row_exp(k, carry, h=h, row0=row0):
                r0 = pl.multiple_of(k * ATTN_SUB, ATTN_SUB)
                m = m_ref[h, pl.ds(row0 + r0, ATTN_SUB), :]
                p_ref[pl.ds(r0, ATTN_SUB), :] = jnp.exp2(s_ref[pl.ds(r0, ATTN_SUB), :] - m).astype(BF16)
                return carry

            lax.fori_loop(0, ATTN_ROWS // ATTN_SUB, row_max, 0, unroll=ATTN_UNROLL)
            lax.fori_loop(0, ATTN_ROWS // ATTN_SUB, row_exp, 0, unroll=ATTN_UNROLL)
            blk = slice(row0, row0 + ATTN_ROWS)
            pv = _dot(p_ref[...], vb)
            alpha = alpha_ref[...]
            acc_ref[h, blk, :] = alpha * acc_ref[h, blk, :] + pv[:, :V_DIM]
            l_ref[h, blk, :] = alpha * l_ref[h, blk, :] + pv[:, V_DIM:]

    @pl.when(j == i)
    def _():
        lam = _diff_lambda(lp_ref[...], layer)
        for h in range(N_HEADS):
            o2 = acc_ref[h] / l_ref[h]
            o = o2[:t] - lam * o2[t:]
            o = _rms(o) * subln_ref[...] * (1.0 - _lambda_init(layer))
            o_ref[0, :, h * V_DIM:(h + 1) * V_DIM] = o.astype(BF16)


def _attn_prompt(q, kt, v, bias_tiles, rel_bias, lp, subln, *, layer, t):
    B, T, _ = q.shape
    kv_layer = kt.shape[0] - 1
    n = T // t
    pairs = [(i, j) for i in range(n) for j in range(i + 1)]
    qi = jnp.asarray([p[0] for p in pairs], jnp.int32)
    kj = jnp.asarray([p[1] for p in pairs], jnp.int32)
    grid_spec = pltpu.PrefetchScalarGridSpec(
        num_scalar_prefetch=2,
        grid=(B, len(pairs)),
        in_specs=[pl.BlockSpec((1, t, QK_W), lambda b, s, qi, kj: (b, qi[s], 0)),
                  pl.BlockSpec((1, 1, QK_W, t), lambda b, s, qi, kj: (kv_layer, b, 0, kj[s])),
                  pl.BlockSpec((1, 1, t * N_HEADS, V_DIM), lambda b, s, qi, kj: (kv_layer, b, kj[s], 0)),
                  pl.BlockSpec((N_HEADS, 1, t, t),
                               lambda b, s, qi, kj: (0, jnp.minimum(qi[s] - kj[s], 1), 0, 0)),
                  pl.BlockSpec(memory_space=pltpu.SMEM),
                  pl.BlockSpec(lp.shape, lambda *_: (0, 0)),
                  pl.BlockSpec(subln.shape, lambda *_: (0, 0))],
        out_specs=pl.BlockSpec((1, t, ATTN_W), lambda b, s, qi, kj: (b, qi[s], 0)),
        scratch_shapes=[pltpu.VMEM((N_HEADS, 2 * t, 1), F32),
                        pltpu.VMEM((N_HEADS, 2 * t, V_DIM), F32),
                        pltpu.VMEM((N_HEADS, 2 * t, V_DIM), F32),
                        pltpu.VMEM((ATTN_ROWS, t), F32),
                        pltpu.VMEM((ATTN_ROWS, t), BF16),
                        pltpu.VMEM((ATTN_ROWS, V_DIM), F32)])
    return pl.pallas_call(
        functools.partial(_attn_kernel, t=t, layer=layer),
        grid_spec=grid_spec,
        out_shape=jax.ShapeDtypeStruct((B, T, ATTN_W), BF16),
        compiler_params=_params(2),
        name="attn_prompt",
    )(qi, kj, q, kt, v, bias_tiles, rel_bias, lp, subln)


def _attn_sample_kernel(pt_ref, q_ref, kn_ref, vn_ref, *refs, n_pages, page, layer):
    k_refs = refs[:n_pages]
    v_refs = refs[n_pages:2 * n_pages]
    bias_ref, lp_ref, subln_ref, o_ref = refs[2 * n_pages:]
    rows = 2 * N_HEADS
    row = lax.broadcasted_iota(jnp.int32, (rows, QK_W), 0)
    seg = lax.broadcasted_iota(jnp.int32, (rows, QK_W), 1) // HEAD_DIM
    q = jnp.broadcast_to(q_ref[0].astype(F32), (rows, QK_W))
    qbd_f = jnp.where(seg == (row % N_HEADS) * 2 + row // N_HEADS, q, 0.0)
    qbd = qbd_f.astype(BF16)

    s = jnp.concatenate([_dot(qbd, k_refs[p][0, 0].astype(BF16)) for p in range(n_pages)], axis=1)
    past = n_pages * page
    bias = bias_ref[...]
    bias = jnp.concatenate([bias, bias], axis=0)
    s = s + bias[:, :past]
    s_new = jnp.sum(qbd_f * kn_ref[0], axis=-1, keepdims=True) + bias[:, past:past + 1]
    m = jnp.maximum(jnp.max(s, axis=-1, keepdims=True), s_new)
    p = jnp.exp2(s - m)
    p_new = jnp.exp2(s_new - m)
    denom = jnp.sum(p, axis=-1, keepdims=True) + p_new
    p = p / denom
    p_new = p_new / denom

    lam = _diff_lambda(lp_ref[...], layer)
    first = lax.broadcasted_iota(jnp.int32, (rows, 1), 0) < N_HEADS
    a = jnp.where(first, p - lam * pltpu.roll(p, N_HEADS, 0), 0.0).astype(BF16)
    a_new = p_new[:N_HEADS] - lam * p_new[N_HEADS:]

    out = jnp.zeros((rows, ATTN_W), F32)
    for pg in range(n_pages):
        vp = jnp.concatenate([v_refs[pg][0, 0, pl.ds(h, page, stride=N_HEADS), :]
                              for h in range(N_HEADS)], axis=1)
        out = out + _dot(a[:, pg * page:(pg + 1) * page], vp.astype(BF16))
    out = out[:N_HEADS] + a_new * vn_ref[0]
    rowh = lax.broadcasted_iota(jnp.int32, (N_HEADS, ATTN_W), 0)
    head = lax.broadcasted_iota(jnp.int32, (N_HEADS, ATTN_W), 1) // V_DIM
    om = jnp.where(rowh == head, out, 0.0)
    ms = jnp.sum(om * om, axis=-1, keepdims=True) / V_DIM
    on = om * lax.rsqrt(ms + EPS)
    o = jnp.sum(on, axis=0, keepdims=True) * subln_ref[...] * (1.0 - _lambda_init(layer))
    o_ref[0] = o.astype(BF16)


def _attn_sample(q, k_new, v_new, cache_kt, cache_vr, page_table, bias_rows, lp, subln4, *, layer):
    DB = q.shape[0]
    n_pages = page_table.shape[1]
    page = cache_kt.shape[3]

    def page_spec(p):
        return pl.BlockSpec((1, 1, QK_W, page), lambda b, pt: (layer, pt[b, p], 0, 0))

    def vpage_spec(p):
        return pl.BlockSpec((1, 1, page * N_HEADS, V_DIM), lambda b, pt: (layer, pt[b, p], 0, 0))

    vec = pl.BlockSpec((1, 1, QK_W), lambda b, pt: (b, 0, 0))
    grid_spec = pltpu.PrefetchScalarGridSpec(
        num_scalar_prefetch=1,
        grid=(DB,),
        in_specs=([vec, vec, vec] + [page_spec(p) for p in range(n_pages)]
                  + [vpage_spec(p) for p in range(n_pages)]
                  + [pl.BlockSpec(bias_rows.shape, lambda *_: (0, 0)),
                     pl.BlockSpec(lp.shape, lambda *_: (0, 0)),
                     pl.BlockSpec(subln4.shape, lambda *_: (0, 0))]),
        out_specs=vec)
    return pl.pallas_call(
        functools.partial(_attn_sample_kernel, n_pages=n_pages, page=page, layer=layer),
        grid_spec=grid_spec,
        out_shape=jax.ShapeDtypeStruct((DB, 1, ATTN_W), BF16),
        compiler_params=_params(1),
        name="attn_sample",
    )(page_table, q, k_new, v_new, *([cache_kt] * n_pages), *([cache_vr] * n_pages),
      bias_rows, lp, subln4)


def _pool_sample_kernel(state_ref, u_ref, d_ref):
    u = u_ref[...]
    outs = []
    for gi, w in enumerate(POOL_WINDOWS):
        ch = slice(gi * POOL_GROUP_W, (gi + 1) * POOL_GROUP_W)
        s = u[:, ch]
        for r in range(POOL_BUF - (w - 1), POOL_BUF):
            s = s + state_ref[r, :, ch]
        outs.append(s / float(w) - u[:, ch])
    d_ref[...] = jnp.concatenate(outs, axis=-1).astype(BF16)


def _pool_sample(state_t, u):
    return pl.pallas_call(
        _pool_sample_kernel,
        out_shape=jax.ShapeDtypeStruct(u.shape, BF16),
        name="pool_sample",
    )(state_t, u)


def _mix_kernel(o_ref, d_ref, g_ref, x_ref, gt_ref, sh_ref, sc_ref, wp_ref, ps_ref, wb_ref, wo_ref,
                gffn_ref, *rest, router):
    d = d_ref[0]
    y = jnp.concatenate([_dot(d[:, g * POOL_GROUP_W:(g + 1) * POOL_GROUP_W], wp_ref[g])
                         for g in range(len(POOL_WINDOWS))], axis=-1)
    y = (y * ps_ref[...]).astype(BF16)
    pa = _dot(o_ref[0], wb_ref[:ATTN_W])
    pb = _dot(y, wb_ref[ATTN_W:])
    g = g_ref[0]
    merged = g[:, :D_MODEL].astype(F32) * pa + g[:, D_MODEL:].astype(F32) * pb
    xn = x_ref[0] + gt_ref[0] * _dot(merged.astype(BF16), wo_ref[...])
    h2 = _rms(xn) * gffn_ref[...]
    h2 = h2 * (1.0 + sc_ref[0]) + sh_ref[0]
    if not router:
        xn_ref, h2_ref = rest
        xn_ref[0] = xn
        h2_ref[0] = h2.astype(h2_ref.dtype)
        return

    wr_hi_ref, wr_lo_ref, xn_ref, h2_ref, ri_ref, rp_ref = rest
    xn_ref[0] = xn
    h2_ref[0] = h2
    hi = h2.astype(BF16)
    lo = (h2 - hi.astype(F32)).astype(BF16)
    logits = _dot(hi, wr_hi_ref[...]) + _dot(lo, wr_hi_ref[...]) + _dot(hi, wr_lo_ref[...])
    lane = lax.broadcasted_iota(jnp.int32, logits.shape, 1)
    lg = jnp.where(lane < N_EXPERTS, logits, -jnp.inf)
    m1 = jnp.max(lg, axis=-1, keepdims=True)
    i1 = jnp.min(jnp.where(lg == m1, lane, LANES), axis=-1, keepdims=True)
    lg = jnp.where(lane == i1, -jnp.inf, lg)
    m2 = jnp.max(lg, axis=-1, keepdims=True)
    i2 = jnp.min(jnp.where(lg == m2, lane, LANES), axis=-1, keepdims=True)
    e = jnp.exp(m2 - m1)
    ri_ref[0] = jnp.concatenate([i1, i2], axis=-1)
    rp_ref[0] = jnp.concatenate([1.0 / (1.0 + e), e / (1.0 + e)], axis=-1)


def _mix(o, d, g, x, mod, wp_bf, pscale, wb_bf, wo_bf, gffn, router_w, *, tm):
    B, T, _ = x.shape
    router = router_w is not None
    ins = [o, d, g, x, mod, mod, mod, wp_bf, pscale, wb_bf, wo_bf, gffn]
    in_specs = [_row_spec(tm, ATTN_W), _row_spec(tm, POOL_W), _row_spec(tm, 2 * D_MODEL),
                _row_spec(tm, D_MODEL), _mod_spec(mod, tm, 2), _mod_spec(mod, tm, 3),
                _mod_spec(mod, tm, 4)] + [_const_spec(a) for a in ins[7:]]
    out_shape = [jax.ShapeDtypeStruct((B, T, D_MODEL), F32),
                 jax.ShapeDtypeStruct((B, T, D_MODEL), F32 if router else BF16)]
    out_specs = [_row_spec(tm, D_MODEL), _row_spec(tm, D_MODEL)]
    if router:
        ins += list(router_w)
        in_specs += [_const_spec(a) for a in router_w]
        out_shape += [jax.ShapeDtypeStruct((B, T, TOP_K), jnp.int32),
                      jax.ShapeDtypeStruct((B, T, TOP_K), F32)]
        out_specs += [_row_spec(tm, TOP_K), _row_spec(tm, TOP_K)]
    return pl.pallas_call(
        functools.partial(_mix_kernel, router=router),
        grid=(B, T // tm),
        in_specs=in_specs,
        out_specs=out_specs,
        out_shape=out_shape,
        compiler_params=_params(2),
        name="mix",
    )(*ins)


def _ffn_kernel(h_ref, x_ref, gt_ref, wg_ref, wu_ref, wd_ref, y_ref, acc_ref):
    f = pl.program_id(2)

    @pl.when(f == 0)
    def _():
        acc_ref[...] = jnp.zeros_like(acc_ref)

    h = h_ref[0]
    a = (_silu(_dot(h, wg_ref[...])) * _dot(h, wu_ref[...])).astype(BF16)
    acc_ref[...] += _dot(a, wd_ref[...])

    @pl.when(f == pl.num_programs(2) - 1)
    def _():
        y_ref[0] = x_ref[0] + gt_ref[0] * acc_ref[...]


def _ffn_dense(h2, x, mod, wg_bf, wu_bf, wd_bf, *, tm):
    B, T, _ = x.shape
    d_ff = wg_bf.shape[1]
    tf = d_ff // 2
    return pl.pallas_call(
        _ffn_kernel,
        grid=(B, T // tm, d_ff // tf),
        in_specs=[_row_spec(tm, D_MODEL), _row_spec(tm, D_MODEL), _mod_spec(mod, tm, 5),
                  pl.BlockSpec((D_MODEL, tf), lambda b, i, f: (0, f)),
                  pl.BlockSpec((D_MODEL, tf), lambda b, i, f: (0, f)),
                  pl.BlockSpec((tf, D_MODEL), lambda b, i, f: (f, 0))],
        out_specs=_row_spec(tm, D_MODEL),
        out_shape=jax.ShapeDtypeStruct((B, T, D_MODEL), F32),
        scratch_shapes=[pltpu.VMEM((tm, D_MODEL), F32)],
        compiler_params=_params(3),
        name="ffn_dense",
    )(h2, x, mod, wg_bf, wu_bf, wd_bf)


def _row_copy(src_hbm, src_row, dst, dst_row, sem):
    return pltpu.make_async_copy(src_hbm.at[pl.ds(src_row, 1), :], dst.at[pl.ds(dst_row, 1), :], sem)


def _moe_ffn_kernel(te_ref, nu_ref, src_ref, src_next_ref, h_hbm, wg_ref, wu_ref, wd_ref, y_ref,
                    xbuf, acc_ref, sem, *, tm):
    i = pl.program_id(0)
    f = pl.program_id(1)
    nu = nu_ref[0]
    slot = i % 2

    def issue(idx_ref, dst):
        def body(r, carry):
            _row_copy(h_hbm, idx_ref[0, 0, r], xbuf.at[dst], r, sem.at[dst]).start()
            return carry
        lax.fori_loop(0, tm, body, 0, unroll=8)

    def drain(dst):
        def body(r, carry):
            _row_copy(h_hbm, 0, xbuf.at[dst], 0, sem.at[dst]).wait()
            return carry
        lax.fori_loop(0, tm, body, 0, unroll=8)

    @pl.when(i < nu)
    def _():
        @pl.when(f == 0)
        def _():
            @pl.when(i == 0)
            def _():
                issue(src_ref, 0)

            drain(slot)

            @pl.when(i + 1 < nu)
            def _():
                issue(src_next_ref, 1 - slot)

            acc_ref[...] = jnp.zeros_like(acc_ref)

        xb = xbuf[slot].astype(BF16)
        a = (_silu(_dot(xb, wg_ref[0].astype(BF16))) * _dot(xb, wu_ref[0].astype(BF16))).astype(BF16)
        acc_ref[...] += _dot(a, wd_ref[0].astype(BF16))

        @pl.when(f == pl.num_programs(1) - 1)
        def _():
            y_ref[...] = acc_ref[...]

    @pl.when(i >= nu)
    def _():
        y_ref[...] = jnp.zeros_like(y_ref)


def _moe_ffn(h_rows, src, tile_expert, n_used, wg, wu, wd, *, tm, tf):
    n_tiles = src.shape[0]
    d_ff = wg.shape[2]
    assert d_ff % tf == 0, (d_ff, tf)
    nf = d_ff // tf

    def tile(i, nu):
        return jnp.minimum(i, nu[0] - 1)

    def ff(i, f, nu):
        return jnp.where(i < nu[0], f, nf - 1)

    grid_spec = pltpu.PrefetchScalarGridSpec(
        num_scalar_prefetch=2,
        grid=(n_tiles, nf),
        in_specs=[pl.BlockSpec((1, 1, tm), lambda i, f, te, nu: (i, 0, 0), memory_space=pltpu.SMEM),
                  pl.BlockSpec((1, 1, tm), lambda i, f, te, nu: (jnp.minimum(i + 1, n_tiles - 1), 0, 0),
                               memory_space=pltpu.SMEM),
                  pl.BlockSpec(memory_space=pl.ANY),
                  pl.BlockSpec((1, D_MODEL, tf), lambda i, f, te, nu: (te[tile(i, nu)], 0, ff(i, f, nu))),
                  pl.BlockSpec((1, D_MODEL, tf), lambda i, f, te, nu: (te[tile(i, nu)], 0, ff(i, f, nu))),
                  pl.BlockSpec((1, tf, D_MODEL), lambda i, f, te, nu: (te[tile(i, nu)], ff(i, f, nu), 0))],
        out_specs=pl.BlockSpec((tm, D_MODEL), lambda i, f, te, nu: (i, 0)),
        scratch_shapes=[pltpu.VMEM((2, tm, D_MODEL), F32), pltpu.VMEM((tm, D_MODEL), F32),
                        pltpu.SemaphoreType.DMA((2,))])
    return pl.pallas_call(
        functools.partial(_moe_ffn_kernel, tm=tm),
        grid_spec=grid_spec,
        out_shape=jax.ShapeDtypeStruct((n_tiles * tm, D_MODEL), F32),
        compiler_params=_params(2),
        name="moe_ffn",
    )(tile_expert, n_used, src, src, h_rows, wg, wu, wd)


def _combine_kernel(pos_ref, pos_next_ref, y_hbm, x_ref, gt_ref, p_ref, o_ref, buf, sem, *, tc):
    g = pl.program_id(0) * pl.num_programs(1) + pl.program_id(1)
    n_steps = pl.num_programs(0) * pl.num_programs(1)
    slot = g % 2

    def issue(idx_ref, dst):
        def body(t, carry):
            for k in range(TOP_K):
                _row_copy(y_hbm, idx_ref[0, 0, TOP_K * t + k], buf.at[dst, k], t, sem.at[dst]).start()
            return carry
        lax.fori_loop(0, tc, body, 0, unroll=4)

    @pl.when(g == 0)
    def _():
        issue(pos_ref, 0)

    def drain(t, carry):
        for k in range(TOP_K):
            _row_copy(y_hbm, 0, buf.at[slot, 0], 0, sem.at[slot]).wait()
        return carry

    lax.fori_loop(0, tc, drain, 0, unroll=4)

    @pl.when(g + 1 < n_steps)
    def _():
        issue(pos_next_ref, 1 - slot)

    p = p_ref[0]
    f = p[:, 0:1] * buf[slot, 0] + p[:, 1:2] * buf[slot, 1]
    o_ref[0] = x_ref[0] + gt_ref[0] * f


def _combine(y, pos, probs, x, mod, *, tc):
    B, T, _ = x.shape
    nt = T // tc
    last = B * nt - 1
    pos_chunks = pos.reshape(B * nt, 1, TOP_K * tc)
    return pl.pallas_call(
        functools.partial(_combine_kernel, tc=tc),
        grid=(B, nt),
        in_specs=[pl.BlockSpec((1, 1, TOP_K * tc), lambda b, i: (b * nt + i, 0, 0),
                               memory_space=pltpu.SMEM),
                  pl.BlockSpec((1, 1, TOP_K * tc), lambda b, i: (jnp.minimum(b * nt + i + 1, last), 0, 0),
                               memory_space=pltpu.SMEM),
                  pl.BlockSpec(memory_space=pl.ANY),
                  _row_spec(tc, D_MODEL), _mod_spec(mod, tc, 5), _row_spec(tc, TOP_K)],
        out_specs=_row_spec(tc, D_MODEL),
        out_shape=jax.ShapeDtypeStruct(x.shape, F32),
        scratch_shapes=[pltpu.VMEM((2, TOP_K, tc, D_MODEL), F32), pltpu.SemaphoreType.DMA((2,))],
        compiler_params=_params(2),
        name="moe_combine",
    )(pos_chunks, pos_chunks, y, x, mod, probs)


def _route_plan(top_i, tm, n_tiles):
    e_flat = top_i.reshape(-1)
    onehot = (e_flat[:, None] == jnp.arange(N_EXPERTS, dtype=jnp.int32)[None, :]).astype(jnp.int32)
    csum = jnp.cumsum(onehot, axis=0)
    cnt = csum[-1]
    gsz = (cnt + tm - 1) // tm * tm
    gend = jnp.cumsum(gsz)
    gstart = gend - gsz
    pos = jnp.sum(onehot * (gstart[None, :] + csum - 1), axis=1).astype(jnp.int32)
    n_used = (gend[-1] // tm).astype(jnp.int32).reshape(1)
    tile_start = jnp.arange(n_tiles, dtype=jnp.int32) * tm
    tile_expert = jnp.sum((tile_start[:, None] >= gend[None, :]).astype(jnp.int32), axis=1)
    tile_expert = jnp.minimum(tile_expert, N_EXPERTS - 1).astype(jnp.int32)
    token = jnp.arange(e_flat.shape[0], dtype=jnp.int32) // TOP_K
    src = jnp.zeros((n_tiles * tm,), jnp.int32).at[pos].set(token, unique_indices=True)
    return pos.reshape(top_i.shape), src.reshape(n_tiles, 1, tm), tile_expert, n_used


def _moe(groups, wg, wu, wd):
    tm = MOE_ROW_TILE
    counts = [g[0].shape[0] * g[0].shape[1] for g in groups]
    total = sum(counts) * TOP_K
    n_tiles = (total + N_EXPERTS * (tm - 1)) // tm + 1
    top_all = jnp.concatenate([g[3].reshape(-1, TOP_K) for g in groups], axis=0)
    h_all = jnp.concatenate([g[0].reshape(-1, D_MODEL) for g in groups], axis=0)
    pos_all, src, tile_expert, n_used = _route_plan(top_all, tm, n_tiles)
    y = _moe_ffn(h_all, src, tile_expert, n_used, wg, wu, wd, tm=tm, tf=MOE_FF_TILE)
    outs = []
    offs = 0
    for (h2, x, mod, top_i, probs), cnt in zip(groups, counts):
        pos = pos_all[offs:offs + cnt].reshape(top_i.shape)
        offs += cnt
        outs.append(_combine(y, pos, probs, x, mod, tc=min(MOE_DMA_CHUNK, x.shape[1])))
    return outs


def kernel(x_prompt, x_sample, c_prompt, c_sample, cache_k, cache_v, state_pool, page_table,
           rel_bias, w_ada, b_ada, g_mix, g_ffn, w_in, q_norm, k_norm, diff_lambda, subln,
           w_pool, pool_scale, w_branch, w_out, w_ff_gate, w_ff_up, w_ff_down, w_router,
           w_exp_gate, w_exp_up, w_exp_down):
    B, S, D = x_prompt.shape
    DB = x_sample.shape[0]
    depth = w_in.shape[0]
    n_phys, page = cache_k.shape[1], cache_k.shape[2]
    past = page_table.shape[1] * page

    mod_all = _ada_mod(jnp.concatenate([c_prompt, c_sample], axis=0), w_ada, b_ada)
    bias_tiles = _bias_tiles(rel_bias, ATTN_TILE)
    bias_rows = _bias_rows(rel_bias, past)

    cache_kt = jnp.transpose(cache_k, (0, 1, 3, 4, 5, 2)).reshape(depth, n_phys, QK_W, page)
    cache_vr = cache_v.reshape(depth, n_phys, page * N_HEADS, V_DIM)
    state_t = jnp.transpose(state_pool, (0, 2, 1, 3))

    bd = jnp.kron(jnp.eye(QK_W // HEAD_DIM, dtype=F32),
                  jnp.full((HEAD_DIM, HEAD_DIM), 1.0 / HEAD_DIM, F32)).astype(BF16)

    xp = x_prompt
    xs = x_sample.reshape(1, DB, D)
    outs = {k: [] for k in ("pp", "ks", "vs", "ps")}
    prev_kv = ((), ())
    for l in range(depth):
        mod_p = mod_all[l, :B].reshape(B, 1, 6 * D)
        mod_s = mod_all[l, B:].reshape(1, DB, 6 * D)
        w_bf = w_in[l].astype(BF16)
        wkt_bf = w_in[l][:, QK_W:2 * QK_W].T.astype(BF16)
        qg = jnp.tile(q_norm[l].reshape(1, 2 * HEAD_DIM), (1, N_HEADS)) * (HEAD_DIM ** -0.5 * LOG2E)
        kg = jnp.tile(k_norm[l], (N_HEADS, 1)).reshape(N_HEADS * 2, HEAD_DIM, 1)
        gmix = g_mix[l].reshape(1, D)
        gffn = g_ffn[l].reshape(1, D)
        lp = diff_lambda[l]
        sub1 = subln[l].reshape(1, V_DIM)
        sub4 = jnp.tile(sub1, (1, N_HEADS))
        wp_bf = w_pool[l].astype(BF16)
        pscale = pool_scale[l].reshape(1, POOL_W)
        wb_bf = w_branch[l].astype(BF16)
        wo_bf = w_out[l].astype(BF16)
        moe = l % 2 == 1
        j = l // 2
        router_w = None
        if moe:
            wr = jnp.pad(w_router[j], ((0, 0), (0, LANES - N_EXPERTS)))
            wr_hi = wr.astype(BF16)
            router_w = (wr_hi, (wr - wr_hi.astype(F32)).astype(BF16))

        stack = prev_kv if l == depth - 1 else ((), ())
        qp, ktp, vp, up, gp, dp = _in_proj(xp, mod_p, gmix, w_bf, wkt_bf, qg, kg, bd, *stack,
                                           tm=ROW_TILE, seq_pool=True)
        prev_kv = (prev_kv[0] + (ktp,), prev_kv[1] + (vp,))
        op = _attn_prompt(qp, ktp, vp, bias_tiles, rel_bias, lp, sub1, layer=l, t=ATTN_TILE)
        mix_p = _mix(op, dp, gp, xp, mod_p, wp_bf, pscale, wb_bf, wo_bf, gffn, router_w, tm=ROW_TILE)

        qs, kts, vs, us, gs = _in_proj(xs, mod_s, gmix, w_bf, wkt_bf, qg, kg, bd,
                                       tm=DB, seq_pool=False)
        ks_rows = jnp.transpose(kts[0], (0, 2, 1)).reshape(DB, 1, QK_W)
        osamp = _attn_sample(qs.reshape(DB, 1, QK_W), ks_rows, vs.reshape(DB, 1, ATTN_W),
                             cache_kt, cache_vr, page_table, bias_rows, lp, sub4, layer=l)
        ds = _pool_sample(state_t[l], us[0])
        mix_s = _mix(osamp.reshape(1, DB, ATTN_W), ds.reshape(1, DB, POOL_W), gs, xs, mod_s,
                     wp_bf, pscale, wb_bf, wo_bf, gffn, router_w, tm=DB)

        if moe:
            xp, xs = _moe([(mix_p[1], mix_p[0], mod_p, mix_p[2], mix_p[3]),
                           (mix_s[1], mix_s[0], mod_s, mix_s[2], mix_s[3])],
                          w_exp_gate[j], w_exp_up[j], w_exp_down[j])
        else:
            wg_bf = w_ff_gate[j].astype(BF16)
            wu_bf = w_ff_up[j].astype(BF16)
            wd_bf = w_ff_down[j].astype(BF16)
            xp = _ffn_dense(mix_p[1], mix_p[0], mod_p, wg_bf, wu_bf, wd_bf, tm=FFN_ROW_TILE)
            xs = _ffn_dense(mix_s[1], mix_s[0], mod_s, wg_bf, wu_bf, wd_bf, tm=DB)

        outs["pp"].append(up[:, S - POOL_BUF:, :])
        outs["ks"].append(ks_rows.reshape(DB, 1, N_HEADS, 2, HEAD_DIM))
        outs["vs"].append(vs.reshape(DB, 1, N_HEADS, V_DIM))
        outs["ps"].append(jnp.concatenate([state_pool[l][:, 1:], us.reshape(DB, 1, POOL_W)], axis=1))

    return (xp, xs.reshape(DB, 1, D),
            jnp.transpose(ktp.reshape(depth, B, N_HEADS, 2, HEAD_DIM, S), (0, 1, 5, 2, 3, 4)),
            vp.reshape(depth, B, S, N_HEADS, V_DIM), jnp.stack(outs["pp"]),
            jnp.stack(outs["ks"]), jnp.stack(outs["vs"]), jnp.stack(outs["ps"]))
```

```python
import functools
import math

import jax
import jax.numpy as jnp
from jax import lax
from jax.experimental import pallas as pl
from jax.experimental.pallas import tpu as pltpu

F32 = jnp.float32
BF16 = jnp.bfloat16

D_MODEL = 1024
N_HEADS = 4
HEAD_DIM = 64
V_DIM = 2 * HEAD_DIM
QK_W = N_HEADS * 2 * HEAD_DIM
ATTN_W = N_HEADS * V_DIM
POOL_WINDOWS = (2, 4, 8, 16)
POOL_W = D_MODEL // 2
POOL_GROUP_W = POOL_W // len(POOL_WINDOWS)
POOL_BUF = max(POOL_WINDOWS) - 1
POOL_HALO = max(POOL_WINDOWS)
IN_COLS = 2 * QK_W + ATTN_W + POOL_W + 2 * D_MODEL
N_BUCKETS = 32
MAX_DISTANCE = 128
N_EXPERTS = 8
TOP_K = 2
EPS = 1e-6
LOG2E = math.log2(math.e)
LANES = 128
V7X_VMEM_BYTES = 64 * 1024 * 1024
VMEM_LIMIT = V7X_VMEM_BYTES * 7 // 8

ATTN_TILE = 512
ATTN_ROWS = 256
ATTN_SUB = 32
ATTN_UNROLL = ATTN_ROWS // ATTN_SUB
ROW_TILE = 512
FFN_ROW_TILE = 1024
MOE_ROW_TILE = 1024
MOE_FF_TILE = 512
MOE_DMA_CHUNK = 256

_NT = (((1,), (1,)), ((), ()))


def _lambda_init(layer):
    return 0.8 - 0.6 * math.exp(-0.3 * layer)


def _params(n_axes, vmem=VMEM_LIMIT):
    return pltpu.CompilerParams(dimension_semantics=("arbitrary",) * n_axes,
                                vmem_limit_bytes=vmem)


def _dot(a, b):
    return jnp.dot(a, b, preferred_element_type=F32)


def _rms(x):
    return x * lax.rsqrt(jnp.mean(x * x, axis=-1, keepdims=True) + EPS)


def _silu(x):
    return x * jax.nn.sigmoid(x)


def _const_spec(a):
    nd = a.ndim
    return pl.BlockSpec(a.shape, lambda *_: (0,) * nd)


def _row_spec(tm, width):
    return pl.BlockSpec((1, tm, width), lambda b, i, *_: (b, i, 0))


def _mod_spec(mod, tm, chunk):
    if mod.shape[1] == 1:
        return pl.BlockSpec((1, 1, D_MODEL), lambda b, i, *_: (b, 0, chunk))
    return pl.BlockSpec((1, tm, D_MODEL), lambda b, i, *_: (b, i, chunk))


def _ada_kernel(c_ref, w_ref, b_ref, o_ref):
    a = _silu(c_ref[...]).astype(BF16)
    o_ref[0] = _dot(a, w_ref[0].astype(BF16)) + b_ref[0]


def _ada_mod(c_all, w_ada, b_ada):
    rows = c_all.shape[0]
    depth, _, cols = w_ada.shape
    tn = cols // 4
    return pl.pallas_call(
        _ada_kernel,
        grid=(depth, cols // tn),
        in_specs=[pl.BlockSpec((rows, D_MODEL), lambda l, n: (0, 0)),
                  pl.BlockSpec((1, D_MODEL, tn), lambda l, n: (l, 0, n)),
                  pl.BlockSpec((1, 1, tn), lambda l, n: (l, 0, n))],
        out_specs=pl.BlockSpec((1, rows, tn), lambda l, n: (l, 0, n)),
        out_shape=jax.ShapeDtypeStruct((depth, rows, cols), F32),
        compiler_params=_params(2),
        name="ada_mod",
    )(c_all, w_ada, b_ada.reshape(depth, 1, cols))


def _t5_bucket(rel):
    n = jnp.maximum(rel, 0)
    max_exact = N_BUCKETS // 2
    nf = jnp.maximum(n, 1).astype(F32)
    large = max_exact + (jnp.log(nf / max_exact) / math.log(MAX_DISTANCE / max_exact)
                         * (N_BUCKETS - max_exact)).astype(jnp.int32)
    large = jnp.minimum(large, N_BUCKETS - 1)
    return jnp.where(n < max_exact, n, large)


def _bias_lookup(bucket, rb_ref, h):
    acc = jnp.zeros(bucket.shape, F32)
    for b in range(N_BUCKETS):
        acc = jnp.where(bucket == b, rb_ref[b, h], acc)
    return acc


def _bias_tile_kernel(rb_ref, o_ref, *, t):
    h = pl.program_id(0)
    d = pl.program_id(1)
    r = lax.broadcasted_iota(jnp.int32, (t, t), 0)
    c = lax.broadcasted_iota(jnp.int32, (t, t), 1)
    rel = d * t + r - c
    val = _bias_lookup(_t5_bucket(rel), rb_ref, h)
    o_ref[0, 0] = jnp.where(rel >= 0, val * LOG2E, -jnp.inf)


def _bias_tiles(rel_bias, t):
    return pl.pallas_call(
        functools.partial(_bias_tile_kernel, t=t),
        grid=(N_HEADS, 2),
        in_specs=[pl.BlockSpec(memory_space=pltpu.SMEM)],
        out_specs=pl.BlockSpec((1, 1, t, t), lambda h, d: (h, d, 0, 0)),
        out_shape=jax.ShapeDtypeStruct((N_HEADS, 2, t, t), F32),
        compiler_params=_params(2),
        name="bias_tiles",
    )(rel_bias)


def _bias_row_kernel(rb_ref, o_ref, *, past):
    width = o_ref.shape[1]
    c = lax.broadcasted_iota(jnp.int32, (1, width), 1)
    bucket = _t5_bucket(jnp.maximum(past - c, 0))
    for h in range(N_HEADS):
        o_ref[h:h + 1, :] = _bias_lookup(bucket, rb_ref, h) * LOG2E


def _bias_rows(rel_bias, past):
    return pl.pallas_call(
        functools.partial(_bias_row_kernel, past=past),
        in_specs=[pl.BlockSpec(memory_space=pltpu.SMEM)],
        out_shape=jax.ShapeDtypeStruct((N_HEADS, past + LANES), F32),
        name="bias_rows",
    )(rel_bias)


def _window_means(ext, u, pos):
    tm = u.shape[0]
    outs = []
    for gi, w in enumerate(POOL_WINDOWS):
        ch = slice(gi * POOL_GROUP_W, (gi + 1) * POOL_GROUP_W)
        s = ext[:, ch]
        span = 1
        while span < w:
            s = s[span:] + s[:-span]
            span *= 2
        off = POOL_HALO - (w - 1)
        win = s[off:off + tm]
        cnt = jnp.minimum(pos + 1, w).astype(F32)
        outs.append(win / cnt - u[:, ch])
    return jnp.concatenate(outs, axis=-1)


def _in_proj_kernel(x_ref, sh_ref, sc_ref, gmix_ref, w_ref, wkt_ref, qg_ref, kg_ref, bd_ref,
                    *rest, tm, seq_pool, n_prev):
    prev_k, prev_v = rest[:n_prev], rest[n_prev:2 * n_prev]
    q_ref, kt_ref, v_ref, u_ref, g_ref = rest[2 * n_prev:2 * n_prev + 5]
    rest = rest[2 * n_prev + 5:]
    for p in range(n_prev):
        kt_ref[p, 0] = prev_k[p][0, 0]
        v_ref[p, 0] = prev_v[p][0, 0]
    h = _rms(x_ref[0]) * gmix_ref[...]
    h = h * (1.0 + sc_ref[0]) + sh_ref[0]
    hb = h.astype(BF16)

    def proj(c0, c1):
        return _dot(hb, w_ref[:, c0:c1])

    zq = proj(0, QK_W)
    msq = _dot((zq * zq).astype(BF16), bd_ref[...])
    q_ref[0] = (zq * lax.rsqrt(msq + EPS) * qg_ref[...]).astype(BF16)

    zk = lax.dot_general(wkt_ref[...], hb, _NT, preferred_element_type=F32)
    zk = zk.reshape(N_HEADS * 2, HEAD_DIM, tm)
    msk = jnp.mean(zk * zk, axis=1, keepdims=True)
    kt_ref[n_prev, 0] = (zk * lax.rsqrt(msk + EPS) * kg_ref[...]).reshape(QK_W, tm)

    c0 = 2 * QK_W
    zv = proj(c0, c0 + ATTN_W)
    for hd in range(N_HEADS):
        v_ref[n_prev, 0, pl.ds(hd, tm, stride=N_HEADS), :] = zv[:, hd * V_DIM:(hd + 1) * V_DIM]
    c0 += ATTN_W
    u = proj(c0, c0 + POOL_W)
    u_ref[0] = u
    c0 += POOL_W
    for c in range(0, 2 * D_MODEL, 512):
        g_ref[0, :, c:c + 512] = jax.nn.sigmoid(proj(c0 + c, c0 + c + 512)).astype(BF16)

    if seq_pool:
        d_ref, carry_ref = rest
        i = pl.program_id(1)

        @pl.when(i == 0)
        def _():
            carry_ref[...] = jnp.zeros_like(carry_ref)

        ext = jnp.concatenate([carry_ref[...], u], axis=0)
        carry_ref[...] = u[tm - POOL_HALO:, :]
        pos = i * tm + lax.broadcasted_iota(jnp.int32, (tm, 1), 0)
        d_ref[0] = _window_means(ext, u, pos).astype(BF16)


def _in_proj(x, mod, gmix, w_bf, wkt_bf, qg, kg, bd, prev_k=(), prev_v=(), *, tm, seq_pool):
    B, T, _ = x.shape
    nt = T // tm
    n_prev = len(prev_k)
    k_block = lambda n: pl.BlockSpec((n, 1, QK_W, tm), lambda b, i: (0, b, 0, i))
    v_block = lambda n: pl.BlockSpec((n, 1, tm * N_HEADS, V_DIM), lambda b, i: (0, b, i, 0))
    out_shape = [jax.ShapeDtypeStruct((B, T, QK_W), BF16),
                 jax.ShapeDtypeStruct((n_prev + 1, B, QK_W, T), F32),
                 jax.ShapeDtypeStruct((n_prev + 1, B, T * N_HEADS, V_DIM), F32),
                 jax.ShapeDtypeStruct((B, T, POOL_W), F32),
                 jax.ShapeDtypeStruct((B, T, 2 * D_MODEL), BF16)]
    out_specs = [_row_spec(tm, QK_W), k_block(n_prev + 1), v_block(n_prev + 1),
                 _row_spec(tm, POOL_W), _row_spec(tm, 2 * D_MODEL)]
    scratch = []
    if seq_pool:
        out_shape.append(jax.ShapeDtypeStruct((B, T, POOL_W), BF16))
        out_specs.append(_row_spec(tm, POOL_W))
        scratch.append(pltpu.VMEM((POOL_HALO, POOL_W), F32))
    return pl.pallas_call(
        functools.partial(_in_proj_kernel, tm=tm, seq_pool=seq_pool, n_prev=n_prev),
        grid=(B, nt),
        in_specs=[_row_spec(tm, D_MODEL), _mod_spec(mod, tm, 0), _mod_spec(mod, tm, 1),
                  _const_spec(gmix), _const_spec(w_bf), _const_spec(wkt_bf),
                  _const_spec(qg), _const_spec(kg), _const_spec(bd)]
                 + [k_block(1)] * n_prev + [v_block(1)] * n_prev,
        out_specs=out_specs,
        out_shape=out_shape,
        scratch_shapes=scratch,
        compiler_params=_params(2),
        name="in_proj",
    )(x, mod, mod, gmix, w_bf, wkt_bf, qg, kg, bd, *prev_k, *prev_v)


def _diff_lambda(lp, layer):
    a = jnp.sum(lp[0:1] * lp[1:2], axis=-1, keepdims=True)
    b = jnp.sum(lp[2:3] * lp[3:4], axis=-1, keepdims=True)
    return jnp.exp(a) - jnp.exp(b) + _lambda_init(layer)


def _attn_kernel(qi_ref, kj_ref, q_ref, kt_ref, v_ref, bias_ref, rb_ref, lp_ref, subln_ref,
                 o_ref, m_ref, l_ref, acc_ref, s_ref, p_ref, alpha_ref, *, t, layer):
    s_idx = pl.program_id(1)
    i = qi_ref[s_idx]
    j = kj_ref[s_idx]

    @pl.when(j == 0)
    def _():
        m_ref[...] = jnp.full_like(m_ref, -jnp.inf)
        l_ref[...] = jnp.zeros_like(l_ref)
        acc_ref[...] = jnp.zeros_like(acc_ref)

    lane = lax.broadcasted_iota(jnp.int32, (1, V_DIM), 1)
    map0 = jnp.where(lane < HEAD_DIM, 1.0, 0.0).astype(BF16)
    map1 = jnp.where(lane >= HEAD_DIM, 1.0, 0.0).astype(BF16)
    far = (i - j) >= 2
    for h in range(N_HEADS):
        hs = slice(h * V_DIM, (h + 1) * V_DIM)
        q = q_ref[0, :, hs]
        kb = kt_ref[0, 0, hs, :].astype(BF16)
        vb = jnp.concatenate([v_ref[0, 0, pl.ds(h, t, stride=N_HEADS), :].astype(BF16),
                              jnp.ones((t, V_DIM), BF16)], axis=1)
        far_bias = rb_ref[N_BUCKETS - 1, h] * LOG2E
        for c in range(2 * t // ATTN_ROWS):
            qrow = (c * ATTN_ROWS) % t
            qc = q[qrow:qrow + ATTN_ROWS] * (map0 if c * ATTN_ROWS < t else map1)
            s_ref[...] = _dot(qc, kb)
            row0 = c * ATTN_ROWS

            def row_max(k, carry, h=h, qrow=qrow, row0=row0, far_bias=far_bias):
                r0 = pl.multiple_of(k * ATTN_SUB, ATTN_SUB)
                rows = pl.ds(row0 + r0, ATTN_SUB)
                bias = jnp.where(far, far_bias, bias_ref[h, 0, pl.ds(qrow + r0, ATTN_SUB), :])
                s = s_ref[pl.ds(r0, ATTN_SUB), :] + bias
                s_ref[pl.ds(r0, ATTN_SUB), :] = s
                m_prev = m_ref[h, rows, :]
                m_new = jnp.maximum(m_prev, jnp.max(s, axis=-1, keepdims=True))
                alpha_ref[pl.ds(r0, ATTN_SUB), :] = jnp.broadcast_to(jnp.exp2(m_prev - m_new),
                                                                     (ATTN_SUB, V_DIM))
                m_ref[h, rows, :] = m_new
                return carry

            def row_exp(k, carry, h=h, row0=row0):
                r0 = pl.multiple_of(k * ATTN_SUB, ATTN_SUB)
                m = m_ref[h, pl.ds(row0 + r0, ATTN_SUB), :]
                p_ref[pl.ds(r0, ATTN_SUB), :] = jnp.exp2(s_ref[pl.ds(r0, ATTN_SUB), :] - m).astype(BF16)
                return carry

            lax.fori_loop(0, ATTN_ROWS // ATTN_SUB, row_max, 0, unroll=ATTN_UNROLL)
            lax.fori_loop(0, ATTN_ROWS // ATTN_SUB, row_exp, 0, unroll=ATTN_UNROLL)
            blk = slice(row0, row0 + ATTN_ROWS)
            pv = _dot(p_ref[...], vb)
            alpha = alpha_ref[...]
            acc_ref[h, blk, :] = alpha * acc_ref[h, blk, :] + pv[:, :V_DIM]
            l_ref[h, blk, :] = alpha * l_ref[h, blk, :] + pv[:, V_DIM:]

    @pl.when(j == i)
    def _():
        lam = _diff_lambda(lp_ref[...], layer)
        for h in range(N_HEADS):
            o2 = acc_ref[h] / l_ref[h]
            o = o2[:t] - lam * o2[t:]
            o = _rms(o) * subln_ref[...] * (1.0 - _lambda_init(layer))
            o_ref[0, :, h * V_DIM:(h + 1) * V_DIM] = o.astype(BF16)


def _attn_prompt(q, kt, v, bias_tiles, rel_bias, lp, subln, *, layer, t):
    B, T, _ = q.shape
    kv_layer = kt.shape[0] - 1
    n = T // t
    pairs = [(i, j) for i in range(n) for j in range(i + 1)]
    qi = jnp.asarray([p[0] for p in pairs], jnp.int32)
    kj = jnp.asarray([p[1] for p in pairs], jnp.int32)
    grid_spec = pltpu.PrefetchScalarGridSpec(
        num_scalar_prefetch=2,
        grid=(B, len(pairs)),
        in_specs=[pl.BlockSpec((1, t, QK_W), lambda b, s, qi, kj: (b, qi[s], 0)),
                  pl.BlockSpec((1, 1, QK_W, t), lambda b, s, qi, kj: (kv_layer, b, 0, kj[s])),
                  pl.BlockSpec((1, 1, t * N_HEADS, V_DIM), lambda b, s, qi, kj: (kv_layer, b, kj[s], 0)),
                  pl.BlockSpec((N_HEADS, 1, t, t),
                               lambda b, s, qi, kj: (0, jnp.minimum(qi[s] - kj[s], 1), 0, 0)),
                  pl.BlockSpec(memory_space=pltpu.SMEM),
                  pl.BlockSpec(lp.shape, lambda *_: (0, 0)),
                  pl.BlockSpec(subln.shape, lambda *_: (0, 0))],
        out_specs=pl.BlockSpec((1, t, ATTN_W), lambda b, s, qi, kj: (b, qi[s], 0)),
        scratch_shapes=[pltpu.VMEM((N_HEADS, 2 * t, 1), F32),
                        pltpu.VMEM((N_HEADS, 2 * t, V_DIM), F32),
                        pltpu.VMEM((N_HEADS, 2 * t, V_DIM), F32),
                        pltpu.VMEM((ATTN_ROWS, t), F32),
                        pltpu.VMEM((ATTN_ROWS, t), BF16),
                        pltpu.VMEM((ATTN_ROWS, V_DIM), F32)])
    return pl.pallas_call(
        functools.partial(_attn_kernel, t=t, layer=layer),
        grid_spec=grid_spec,
        out_shape=jax.ShapeDtypeStruct((B, T, ATTN_W), BF16),
        compiler_params=_params(2),
        name="attn_prompt",
    )(qi, kj, q, kt, v, bias_tiles, rel_bias, lp, subln)


def _attn_sample_kernel(pt_ref, q_ref, kn_ref, vn_ref, *refs, n_pages, page, layer):
    k_refs = refs[:n_pages]
    v_refs = refs[n_pages:2 * n_pages]
    bias_ref, lp_ref, subln_ref, o_ref = refs[2 * n_pages:]
    rows = 2 * N_HEADS
    row = lax.broadcasted_iota(jnp.int32, (rows, QK_W), 0)
    seg = lax.broadcasted_iota(jnp.int32, (rows, QK_W), 1) // HEAD_DIM
    q = jnp.broadcast_to(q_ref[0].astype(F32), (rows, QK_W))
    qbd_f = jnp.where(seg == (row % N_HEADS) * 2 + row // N_HEADS, q, 0.0)
    qbd = qbd_f.astype(BF16)

    s = jnp.concatenate([_dot(qbd, k_refs[p][0, 0].astype(BF16)) for p in range(n_pages)], axis=1)
    past = n_pages * page
    bias = bias_ref[...]
    bias = jnp.concatenate([bias, bias], axis=0)
    s = s + bias[:, :past]
    s_new = jnp.sum(qbd_f * kn_ref[0], axis=-1, keepdims=True) + bias[:, past:past + 1]
    m = jnp.maximum(jnp.max(s, axis=-1, keepdims=True), s_new)
    p = jnp.exp2(s - m)
    p_new = jnp.exp2(s_new - m)
    denom = jnp.sum(p, axis=-1, keepdims=True) + p_new
    p = p / denom
    p_new = p_new / denom

    lam = _diff_lambda(lp_ref[...], layer)
    first = lax.broadcasted_iota(jnp.int32, (rows, 1), 0) < N_HEADS
    a = jnp.where(first, p - lam * pltpu.roll(p, N_HEADS, 0), 0.0).astype(BF16)
    a_new = p_new[:N_HEADS] - lam * p_new[N_HEADS:]

    out = jnp.zeros((rows, ATTN_W), F32)
    for pg in range(n_pages):
        vp = jnp.concatenate([v_refs[pg][0, 0, pl.ds(h, page, stride=N_HEADS), :]
                              for h in range(N_HEADS)], axis=1)
        out = out + _dot(a[:, pg * page:(pg + 1) * page], vp.astype(BF16))
    out = out[:N_HEADS] + a_new * vn_ref[0]
    rowh = lax.broadcasted_iota(jnp.int32, (N_HEADS, ATTN_W), 0)
    head = lax.broadcasted_iota(jnp.int32, (N_HEADS, ATTN_W), 1) // V_DIM
    om = jnp.where(rowh == head, out, 0.0)
    ms = jnp.sum(om * om, axis=-1, keepdims=True) / V_DIM
    on = om * lax.rsqrt(ms + EPS)
    o = jnp.sum(on, axis=0, keepdims=True) * subln_ref[...] * (1.0 - _lambda_init(layer))
    o_ref[0] = o.astype(BF16)


def _attn_sample(q, k_new, v_new, cache_kt, cache_vr, page_table, bias_rows, lp, subln4, *, layer):
    DB = q.shape[0]
    n_pages = page_table.shape[1]
    page = cache_kt.shape[3]

    def page_spec(p):
        return pl.BlockSpec((1, 1, QK_W, page), lambda b, pt: (layer, pt[b, p], 0, 0))

    def vpage_spec(p):
        return pl.BlockSpec((1, 1, page * N_HEADS, V_DIM), lambda b, pt: (layer, pt[b, p], 0, 0))

    vec = pl.BlockSpec((1, 1, QK_W), lambda b, pt: (b, 0, 0))
    grid_spec = pltpu.PrefetchScalarGridSpec(
        num_scalar_prefetch=1,
        grid=(DB,),
        in_specs=([vec, vec, vec] + [page_spec(p) for p in range(n_pages)]
                  + [vpage_spec(p) for p in range(n_pages)]
                  + [pl.BlockSpec(bias_rows.shape, lambda *_: (0, 0)),
                     pl.BlockSpec(lp.shape, lambda *_: (0, 0)),
                     pl.BlockSpec(subln4.shape, lambda *_: (0, 0))]),
        out_specs=vec)
    return pl.pallas_call(
        functools.partial(_attn_sample_kernel, n_pages=n_pages, page=page, layer=layer),
        grid_spec=grid_spec,
        out_shape=jax.ShapeDtypeStruct((DB, 1, ATTN_W), BF16),
        compiler_params=_params(1),
        name="attn_sample",
    )(page_table, q, k_new, v_new, *([cache_kt] * n_pages), *([cache_vr] * n_pages),
      bias_rows, lp, subln4)


def _pool_sample_kernel(state_ref, u_ref, d_ref):
    u = u_ref[...]
    outs = []
    for gi, w in enumerate(POOL_WINDOWS):
        ch = slice(gi * POOL_GROUP_W, (gi + 1) * POOL_GROUP_W)
        s = u[:, ch]
        for r in range(POOL_BUF - (w - 1), POOL_BUF):
            s = s + state_ref[r, :, ch]
        outs.append(s / float(w) - u[:, ch])
    d_ref[...] = jnp.concatenate(outs, axis=-1).astype(BF16)


def _pool_sample(state_t, u):
    return pl.pallas_call(
        _pool_sample_kernel,
        out_shape=jax.ShapeDtypeStruct(u.shape, BF16),
        name="pool_sample",
    )(state_t, u)


def _mix_kernel(o_ref, d_ref, g_ref, x_ref, gt_ref, sh_ref, sc_ref, wp_ref, ps_ref, wb_ref, wo_ref,
                gffn_ref, *rest, router):
    d = d_ref[0]
    y = jnp.concatenate([_dot(d[:, g * POOL_GROUP_W:(g + 1) * POOL_GROUP_W], wp_ref[g])
                         for g in range(len(POOL_WINDOWS))], axis=-1)
    y = (y * ps_ref[...]).astype(BF16)
    pa = _dot(o_ref[0], wb_ref[:ATTN_W])
    pb = _dot(y, wb_ref[ATTN_W:])
    g = g_ref[0]
    merged = g[:, :D_MODEL].astype(F32) * pa + g[:, D_MODEL:].astype(F32) * pb
    xn = x_ref[0] + gt_ref[0] * _dot(merged.astype(BF16), wo_ref[...])
    h2 = _rms(xn) * gffn_ref[...]
    h2 = h2 * (1.0 + sc_ref[0]) + sh_ref[0]
    if not router:
        xn_ref, h2_ref = rest
        xn_ref[0] = xn
        h2_ref[0] = h2.astype(h2_ref.dtype)
        return

    wr_hi_ref, wr_lo_ref, xn_ref, h2_ref, ri_ref, rp_ref = rest
    xn_ref[0] = xn
    _slab_store(h2_ref, (0,), h2)
    hi = h2.astype(BF16)
    lo = (h2 - hi.astype(F32)).astype(BF16)
    logits = _dot(hi, wr_hi_ref[...]) + _dot(lo, wr_hi_ref[...]) + _dot(hi, wr_lo_ref[...])
    lane = lax.broadcasted_iota(jnp.int32, logits.shape, 1)
    lg = jnp.where(lane < N_EXPERTS, logits, -jnp.inf)
    m1 = jnp.max(lg, axis=-1, keepdims=True)
    i1 = jnp.min(jnp.where(lg == m1, lane, LANES), axis=-1, keepdims=True)
    lg = jnp.where(lane == i1, -jnp.inf, lg)
    m2 = jnp.max(lg, axis=-1, keepdims=True)
    i2 = jnp.min(jnp.where(lg == m2, lane, LANES), axis=-1, keepdims=True)
    e = jnp.exp(m2 - m1)
    ri_ref[0] = jnp.concatenate([i1, i2], axis=-1)
    rp_ref[0] = jnp.concatenate([1.0 / (1.0 + e), e / (1.0 + e)], axis=-1)


def _mix(o, d, g, x, mod, wp_bf, pscale, wb_bf, wo_bf, gffn, router_w, *, tm):
    B, T, _ = x.shape
    router = router_w is not None
    ins = [o, d, g, x, mod, mod, mod, wp_bf, pscale, wb_bf, wo_bf, gffn]
    in_specs = [_row_spec(tm, ATTN_W), _row_spec(tm, POOL_W), _row_spec(tm, 2 * D_MODEL),
                _row_spec(tm, D_MODEL), _mod_spec(mod, tm, 2), _mod_spec(mod, tm, 3),
                _mod_spec(mod, tm, 4)] + [_const_spec(a) for a in ins[7:]]
    out_shape = [jax.ShapeDtypeStruct((B, T, D_MODEL), F32), jax.ShapeDtypeStruct((B, T, D_MODEL), BF16)]
    out_specs = [_row_spec(tm, D_MODEL), _row_spec(tm, D_MODEL)]
    if router:
        out_shape[1] = jax.ShapeDtypeStruct((B, T * ROW_SLAB, LANES), F32)
        out_specs[1] = _row_spec(tm * ROW_SLAB, LANES)
        ins += list(router_w)
        in_specs += [_const_spec(a) for a in router_w]
        out_shape += [jax.ShapeDtypeStruct((B, T, TOP_K), jnp.int32),
                      jax.ShapeDtypeStruct((B, T, TOP_K), F32)]
        out_specs += [_row_spec(tm, TOP_K), _row_spec(tm, TOP_K)]
    return pl.pallas_call(
        functools.partial(_mix_kernel, router=router),
        grid=(B, T // tm),
        in_specs=in_specs,
        out_specs=out_specs,
        out_shape=out_shape,
        compiler_params=_params(2),
        name="mix",
    )(*ins)


def _ffn_kernel(h_ref, x_ref, gt_ref, wg_ref, wu_ref, wd_ref, y_ref, acc_ref):
    f = pl.program_id(2)

    @pl.when(f == 0)
    def _():
        acc_ref[...] = jnp.zeros_like(acc_ref)

    h = h_ref[0]
    a = (_silu(_dot(h, wg_ref[...])) * _dot(h, wu_ref[...])).astype(BF16)
    acc_ref[...] += _dot(a, wd_ref[...])

    @pl.when(f == pl.num_programs(2) - 1)
    def _():
        y_ref[0] = x_ref[0] + gt_ref[0] * acc_ref[...]


def _ffn_dense(h2, x, mod, wg_bf, wu_bf, wd_bf, *, tm):
    B, T, _ = x.shape
    d_ff = wg_bf.shape[1]
    tf = d_ff // 2
    return pl.pallas_call(
        _ffn_kernel,
        grid=(B, T // tm, d_ff // tf),
        in_specs=[_row_spec(tm, D_MODEL), _row_spec(tm, D_MODEL), _mod_spec(mod, tm, 5),
                  pl.BlockSpec((D_MODEL, tf), lambda b, i, f: (0, f)),
                  pl.BlockSpec((D_MODEL, tf), lambda b, i, f: (0, f)),
                  pl.BlockSpec((tf, D_MODEL), lambda b, i, f: (f, 0))],
        out_specs=_row_spec(tm, D_MODEL),
        out_shape=jax.ShapeDtypeStruct((B, T, D_MODEL), F32),
        scratch_shapes=[pltpu.VMEM((tm, D_MODEL), F32)],
        compiler_params=_params(3),
        name="ffn_dense",
    )(h2, x, mod, wg_bf, wu_bf, wd_bf)


ROW_SLAB = D_MODEL // LANES


def _slab_copy(src_hbm, src_row, dst, dst_row, sem):
    src = src_hbm.at[pl.ds(pl.multiple_of(src_row * ROW_SLAB, ROW_SLAB), ROW_SLAB), :]
    return pltpu.make_async_copy(src, dst.at[pl.ds(pl.multiple_of(dst_row * ROW_SLAB, ROW_SLAB), ROW_SLAB), :], sem)


def _slab_store(ref, idx, x):
    rows = x.shape[0]
    for s in range(ROW_SLAB):
        ref[idx + (pl.ds(s, rows, stride=ROW_SLAB), slice(None))] = x[:, s * LANES:(s + 1) * LANES]


def _slab_load(ref, idx, rows, start=0):
    return jnp.concatenate([ref[idx + (pl.ds(start + s, rows, stride=ROW_SLAB), slice(None))]
                            for s in range(ROW_SLAB)], axis=1)


def _moe_ffn_kernel(te_ref, nu_ref, src_ref, src_next_ref, h_hbm, wg_ref, wu_ref, wd_ref, y_ref,
                    xbuf, xb_ref, acc_ref, sem, *, tm):
    i = pl.program_id(0)
    f = pl.program_id(1)
    nu = nu_ref[0]
    slot = i % 2

    def issue(idx_ref, dst):
        def body(r, carry):
            _slab_copy(h_hbm, idx_ref[0, 0, r], xbuf.at[dst], r, sem.at[dst]).start()
            return carry
        lax.fori_loop(0, tm, body, 0, unroll=8)

    def drain(dst):
        def body(r, carry):
            _slab_copy(h_hbm, 0, xbuf.at[dst], 0, sem.at[dst]).wait()
            return carry
        lax.fori_loop(0, tm, body, 0, unroll=8)

    @pl.when(i < nu)
    def _():
        @pl.when(f == 0)
        def _():
            @pl.when(i == 0)
            def _():
                issue(src_ref, 0)

            drain(slot)

            @pl.when(i + 1 < nu)
            def _():
                issue(src_next_ref, 1 - slot)

            acc_ref[...] = jnp.zeros_like(acc_ref)
            xb_ref[...] = _slab_load(xbuf, (slot,), tm).astype(BF16)

        xb = xb_ref[...]
        a = (_silu(_dot(xb, wg_ref[0].astype(BF16))) * _dot(xb, wu_ref[0].astype(BF16))).astype(BF16)
        acc_ref[...] += _dot(a, wd_ref[0].astype(BF16))

        @pl.when(f == pl.num_programs(1) - 1)
        def _():
            _slab_store(y_ref, (), acc_ref[...])

    @pl.when(i >= nu)
    def _():
        y_ref[...] = jnp.zeros_like(y_ref)


def _moe_ffn(h_rows, src, tile_expert, n_used, wg, wu, wd, *, tm, tf):
    n_tiles = src.shape[0]
    d_ff = wg.shape[2]
    assert d_ff % tf == 0, (d_ff, tf)
    nf = d_ff // tf

    def tile(i, nu):
        return jnp.minimum(i, nu[0] - 1)

    def ff(i, f, nu):
        return jnp.where(i < nu[0], f, nf - 1)

    grid_spec = pltpu.PrefetchScalarGridSpec(
        num_scalar_prefetch=2,
        grid=(n_tiles, nf),
        in_specs=[pl.BlockSpec((1, 1, tm), lambda i, f, te, nu: (i, 0, 0), memory_space=pltpu.SMEM),
                  pl.BlockSpec((1, 1, tm), lambda i, f, te, nu: (jnp.minimum(i + 1, n_tiles - 1), 0, 0),
                               memory_space=pltpu.SMEM),
                  pl.BlockSpec(memory_space=pl.ANY),
                  pl.BlockSpec((1, D_MODEL, tf), lambda i, f, te, nu: (te[tile(i, nu)], 0, ff(i, f, nu))),
                  pl.BlockSpec((1, D_MODEL, tf), lambda i, f, te, nu: (te[tile(i, nu)], 0, ff(i, f, nu))),
                  pl.BlockSpec((1, tf, D_MODEL), lambda i, f, te, nu: (te[tile(i, nu)], ff(i, f, nu), 0))],
        out_specs=pl.BlockSpec((tm * ROW_SLAB, LANES), lambda i, f, te, nu: (i, 0)),
        scratch_shapes=[pltpu.VMEM((2, tm * ROW_SLAB, LANES), F32), pltpu.VMEM((tm, D_MODEL), BF16),
                        pltpu.VMEM((tm, D_MODEL), F32),
                        pltpu.SemaphoreType.DMA((2,))])
    return pl.pallas_call(
        functools.partial(_moe_ffn_kernel, tm=tm),
        grid_spec=grid_spec,
        out_shape=jax.ShapeDtypeStruct((n_tiles * tm * ROW_SLAB, LANES), F32),
        compiler_params=_params(2),
        name="moe_ffn",
    )(tile_expert, n_used, src, src, h_rows, wg, wu, wd)


def _combine_kernel(pos_ref, pos_next_ref, y_hbm, x_ref, gt_ref, p_ref, o_ref, buf, sem, *, tc):
    g = pl.program_id(0) * pl.num_programs(1) + pl.program_id(1)
    n_steps = pl.num_programs(0) * pl.num_programs(1)
    slot = g % 2

    def issue(idx_ref, dst):
        def body(t, carry):
            for k in range(TOP_K):
                _slab_copy(y_hbm, idx_ref[0, 0, TOP_K * t + k], buf.at[dst, k], t, sem.at[dst]).start()
            return carry
        lax.fori_loop(0, tc, body, 0, unroll=4)

    @pl.when(g == 0)
    def _():
        issue(pos_ref, 0)

    def drain(t, carry):
        for k in range(TOP_K):
            _slab_copy(y_hbm, 0, buf.at[slot, 0], 0, sem.at[slot]).wait()
        return carry

    lax.fori_loop(0, tc, drain, 0, unroll=4)

    @pl.when(g + 1 < n_steps)
    def _():
        issue(pos_next_ref, 1 - slot)

    p = p_ref[0]
    f = p[:, 0:1] * _slab_load(buf, (slot, 0), tc) + p[:, 1:2] * _slab_load(buf, (slot, 1), tc)
    o_ref[0] = x_ref[0] + gt_ref[0] * f


def _combine(y, pos, probs, x, mod, *, tc):
    B, T, _ = x.shape
    nt = T // tc
    last = B * nt - 1
    pos_chunks = pos.reshape(B * nt, 1, TOP_K * tc)
    return pl.pallas_call(
        functools.partial(_combine_kernel, tc=tc),
        grid=(B, nt),
        in_specs=[pl.BlockSpec((1, 1, TOP_K * tc), lambda b, i: (b * nt + i, 0, 0),
                               memory_space=pltpu.SMEM),
                  pl.BlockSpec((1, 1, TOP_K * tc), lambda b, i: (jnp.minimum(b * nt + i + 1, last), 0, 0),
                               memory_space=pltpu.SMEM),
                  pl.BlockSpec(memory_space=pl.ANY),
                  _row_spec(tc, D_MODEL), _mod_spec(mod, tc, 5), _row_spec(tc, TOP_K)],
        out_specs=_row_spec(tc, D_MODEL),
        out_shape=jax.ShapeDtypeStruct(x.shape, F32),
        scratch_shapes=[pltpu.VMEM((2, TOP_K, tc * ROW_SLAB, LANES), F32), pltpu.SemaphoreType.DMA((2,))],
        compiler_params=_params(2),
        name="moe_combine",
    )(pos_chunks, pos_chunks, y, x, mod, probs)


def _route_plan(top_i, tm, n_tiles):
    e_flat = top_i.reshape(-1)
    onehot = (e_flat[:, None] == jnp.arange(N_EXPERTS, dtype=jnp.int32)[None, :]).astype(jnp.int32)
    csum = jnp.cumsum(onehot, axis=0)
    cnt = csum[-1]
    gsz = (cnt + tm - 1) // tm * tm
    gend = jnp.cumsum(gsz)
    gstart = gend - gsz
    pos = jnp.sum(onehot * (gstart[None, :] + csum - 1), axis=1).astype(jnp.int32)
    n_used = (gend[-1] // tm).astype(jnp.int32).reshape(1)
    tile_start = jnp.arange(n_tiles, dtype=jnp.int32) * tm
    tile_expert = jnp.sum((tile_start[:, None] >= gend[None, :]).astype(jnp.int32), axis=1)
    tile_expert = jnp.minimum(tile_expert, N_EXPERTS - 1).astype(jnp.int32)
    token = jnp.arange(e_flat.shape[0], dtype=jnp.int32) // TOP_K
    src = jnp.zeros((n_tiles * tm,), jnp.int32).at[pos].set(token, unique_indices=True)
    return pos.reshape(top_i.shape), src.reshape(n_tiles, 1, tm), tile_expert, n_used


def _moe(groups, wg, wu, wd):
    tm = MOE_ROW_TILE
    counts = [g[1].shape[0] * g[1].shape[1] for g in groups]
    total = sum(counts) * TOP_K
    n_tiles = (total + N_EXPERTS * (tm - 1)) // tm + 1
    top_all = jnp.concatenate([g[3].reshape(-1, TOP_K) for g in groups], axis=0)
    h_all = jnp.concatenate([g[0].reshape(-1, LANES) for g in groups], axis=0)
    pos_all, src, tile_expert, n_used = _route_plan(top_all, tm, n_tiles)
    y = _moe_ffn(h_all, src, tile_expert, n_used, wg, wu, wd, tm=tm, tf=MOE_FF_TILE)
    outs = []
    offs = 0
    for (h2, x, mod, top_i, probs), cnt in zip(groups, counts):
        pos = pos_all[offs:offs + cnt].reshape(top_i.shape)
        offs += cnt
        outs.append(_combine(y, pos, probs, x, mod, tc=min(MOE_DMA_CHUNK, x.shape[1])))
    return outs


def kernel(x_prompt, x_sample, c_prompt, c_sample, cache_k, cache_v, state_pool, page_table,
           rel_bias, w_ada, b_ada, g_mix, g_ffn, w_in, q_norm, k_norm, diff_lambda, subln,
           w_pool, pool_scale, w_branch, w_out, w_ff_gate, w_ff_up, w_ff_down, w_router,
           w_exp_gate, w_exp_up, w_exp_down):
    B, S, D = x_prompt.shape
    DB = x_sample.shape[0]
    depth = w_in.shape[0]
    n_phys, page = cache_k.shape[1], cache_k.shape[2]
    past = page_table.shape[1] * page

    mod_all = _ada_mod(jnp.concatenate([c_prompt, c_sample], axis=0), w_ada, b_ada)
    bias_tiles = _bias_tiles(rel_bias, ATTN_TILE)
    bias_rows = _bias_rows(rel_bias, past)

    cache_kt = jnp.transpose(cache_k, (0, 1, 3, 4, 5, 2)).reshape(depth, n_phys, QK_W, page)
    cache_vr = cache_v.reshape(depth, n_phys, page * N_HEADS, V_DIM)
    state_t = jnp.transpose(state_pool, (0, 2, 1, 3))

    bd = jnp.kron(jnp.eye(QK_W // HEAD_DIM, dtype=F32),
                  jnp.full((HEAD_DIM, HEAD_DIM), 1.0 / HEAD_DIM, F32)).astype(BF16)

    xp = x_prompt
    xs = x_sample.reshape(1, DB, D)
    outs = {k: [] for k in ("pp", "ks", "vs", "ps")}
    prev_kv = ((), ())
    for l in range(depth):
        mod_p = mod_all[l, :B].reshape(B, 1, 6 * D)
        mod_s = mod_all[l, B:].reshape(1, DB, 6 * D)
        w_bf = w_in[l].astype(BF16)
        wkt_bf = w_in[l][:, QK_W:2 * QK_W].T.astype(BF16)
        qg = jnp.tile(q_norm[l].reshape(1, 2 * HEAD_DIM), (1, N_HEADS)) * (HEAD_DIM ** -0.5 * LOG2E)
        kg = jnp.tile(k_norm[l], (N_HEADS, 1)).reshape(N_HEADS * 2, HEAD_DIM, 1)
        gmix = g_mix[l].reshape(1, D)
        gffn = g_ffn[l].reshape(1, D)
        lp = diff_lambda[l]
        sub1 = subln[l].reshape(1, V_DIM)
        sub4 = jnp.tile(sub1, (1, N_HEADS))
        wp_bf = w_pool[l].astype(BF16)
        pscale = pool_scale[l].reshape(1, POOL_W)
        wb_bf = w_branch[l].astype(BF16)
        wo_bf = w_out[l].astype(BF16)
        moe = l % 2 == 1
        j = l // 2
        router_w = None
        if moe:
            wr = jnp.pad(w_router[j], ((0, 0), (0, LANES - N_EXPERTS)))
            wr_hi = wr.astype(BF16)
            router_w = (wr_hi, (wr - wr_hi.astype(F32)).astype(BF16))

        stack = prev_kv if l == depth - 1 else ((), ())
        qp, ktp, vp, up, gp, dp = _in_proj(xp, mod_p, gmix, w_bf, wkt_bf, qg, kg, bd, *stack,
                                           tm=ROW_TILE, seq_pool=True)
        prev_kv = (prev_kv[0] + (ktp,), prev_kv[1] + (vp,))
        op = _attn_prompt(qp, ktp, vp, bias_tiles, rel_bias, lp, sub1, layer=l, t=ATTN_TILE)
        mix_p = _mix(op, dp, gp, xp, mod_p, wp_bf, pscale, wb_bf, wo_bf, gffn, router_w, tm=ROW_TILE)

        qs, kts, vs, us, gs = _in_proj(xs, mod_s, gmix, w_bf, wkt_bf, qg, kg, bd,
                                       tm=DB, seq_pool=False)
        ks_rows = jnp.transpose(kts[0], (0, 2, 1)).reshape(DB, 1, QK_W)
        osamp = _attn_sample(qs.reshape(DB, 1, QK_W), ks_rows, vs.reshape(DB, 1, ATTN_W),
                             cache_kt, cache_vr, page_table, bias_rows, lp, sub4, layer=l)
        ds = _pool_sample(state_t[l], us[0])
        mix_s = _mix(osamp.reshape(1, DB, ATTN_W), ds.reshape(1, DB, POOL_W), gs, xs, mod_s,
                     wp_bf, pscale, wb_bf, wo_bf, gffn, router_w, tm=DB)

        if moe:
            xp, xs = _moe([(mix_p[1], mix_p[0], mod_p, mix_p[2], mix_p[3]),
                           (mix_s[1], mix_s[0], mod_s, mix_s[2], mix_s[3])],
                          w_exp_gate[j], w_exp_up[j], w_exp_down[j])
        else:
            wg_bf = w_ff_gate[j].astype(BF16)
            wu_bf = w_ff_up[j].astype(BF16)
            wd_bf = w_ff_down[j].astype(BF16)
            xp = _ffn_dense(mix_p[1], mix_p[0], mod_p, wg_bf, wu_bf, wd_bf, tm=FFN_ROW_TILE)
            xs = _ffn_dense(mix_s[1], mix_s[0], mod_s, wg_bf, wu_bf, wd_bf, tm=DB)

        outs["pp"].append(up[:, S - POOL_BUF:, :])
        outs["ks"].append(ks_rows.reshape(DB, 1, N_HEADS, 2, HEAD_DIM))
        outs["vs"].append(vs.reshape(DB, 1, N_HEADS, V_DIM))
        outs["ps"].append(jnp.concatenate([state_pool[l][:, 1:], us.reshape(DB, 1, POOL_W)], axis=1))

    return (xp, xs.reshape(DB, 1, D),
            jnp.transpose(ktp.reshape(depth, B, N_HEADS, 2, HEAD_DIM, S), (0, 1, 5, 2, 3, 4)),
            vp.reshape(depth, B, S, N_HEADS, V_DIM), jnp.stack(outs["pp"]),
            jnp.stack(outs["ks"]), jnp.stack(outs["vs"]), jnp.stack(outs["ps"]))
```

```python
import functools
import math

import jax
import jax.numpy as jnp
from jax import lax
from jax.experimental import pallas as pl
from jax.experimental.pallas import tpu as pltpu

F32 = jnp.float32
BF16 = jnp.bfloat16

D_MODEL = 1024
N_HEADS = 4
HEAD_DIM = 64
V_DIM = 2 * HEAD_DIM
QK_W = N_HEADS * 2 * HEAD_DIM
ATTN_W = N_HEADS * V_DIM
POOL_WINDOWS = (2, 4, 8, 16)
POOL_W = D_MODEL // 2
POOL_GROUP_W = POOL_W // len(POOL_WINDOWS)
POOL_BUF = max(POOL_WINDOWS) - 1
POOL_HALO = max(POOL_WINDOWS)
IN_COLS = 2 * QK_W + ATTN_W + POOL_W + 2 * D_MODEL
N_BUCKETS = 32
MAX_DISTANCE = 128
N_EXPERTS = 8
TOP_K = 2
EPS = 1e-6
LOG2E = math.log2(math.e)
LANES = 128
V7X_VMEM_BYTES = 64 * 1024 * 1024
VMEM_LIMIT = V7X_VMEM_BYTES * 7 // 8

ATTN_TILE = 512
ATTN_ROWS = 256
ATTN_SUB = 32
ATTN_UNROLL = ATTN_ROWS // ATTN_SUB
ROW_TILE = 512
FFN_ROW_TILE = 1024
MOE_ROW_TILE = 768
MOE_FF_TILE = 512
MOE_DMA_CHUNK = 512

_NT = (((1,), (1,)), ((), ()))


def _lambda_init(layer):
    return 0.8 - 0.6 * math.exp(-0.3 * layer)


def _params(n_axes, vmem=VMEM_LIMIT):
    return pltpu.CompilerParams(dimension_semantics=("arbitrary",) * n_axes,
                                vmem_limit_bytes=vmem)


def _dot(a, b):
    return jnp.dot(a, b, preferred_element_type=F32)


def _rms(x):
    return x * lax.rsqrt(jnp.mean(x * x, axis=-1, keepdims=True) + EPS)


def _silu(x):
    return x * jax.nn.sigmoid(x)


def _const_spec(a):
    nd = a.ndim
    return pl.BlockSpec(a.shape, lambda *_: (0,) * nd)


def _row_spec(tm, width):
    return pl.BlockSpec((1, tm, width), lambda b, i, *_: (b, i, 0))


def _mod_spec(mod, tm, chunk):
    if mod.shape[1] == 1:
        return pl.BlockSpec((1, 1, D_MODEL), lambda b, i, *_: (b, 0, chunk))
    return pl.BlockSpec((1, tm, D_MODEL), lambda b, i, *_: (b, i, chunk))


def _ada_kernel(c_ref, w_ref, b_ref, o_ref):
    a = _silu(c_ref[...]).astype(BF16)
    o_ref[0] = _dot(a, w_ref[0].astype(BF16)) + b_ref[0]


def _ada_mod(c_all, w_ada, b_ada):
    rows = c_all.shape[0]
    depth, _, cols = w_ada.shape
    tn = cols // 4
    return pl.pallas_call(
        _ada_kernel,
        grid=(depth, cols // tn),
        in_specs=[pl.BlockSpec((rows, D_MODEL), lambda l, n: (0, 0)),
                  pl.BlockSpec((1, D_MODEL, tn), lambda l, n: (l, 0, n)),
                  pl.BlockSpec((1, 1, tn), lambda l, n: (l, 0, n))],
        out_specs=pl.BlockSpec((1, rows, tn), lambda l, n: (l, 0, n)),
        out_shape=jax.ShapeDtypeStruct((depth, rows, cols), F32),
        compiler_params=_params(2),
        name="ada_mod",
    )(c_all, w_ada, b_ada.reshape(depth, 1, cols))


def _t5_bucket(rel):
    n = jnp.maximum(rel, 0)
    max_exact = N_BUCKETS // 2
    nf = jnp.maximum(n, 1).astype(F32)
    large = max_exact + (jnp.log(nf / max_exact) / math.log(MAX_DISTANCE / max_exact)
                         * (N_BUCKETS - max_exact)).astype(jnp.int32)
    large = jnp.minimum(large, N_BUCKETS - 1)
    return jnp.where(n < max_exact, n, large)


def _bias_lookup(bucket, rb_ref, h):
    acc = jnp.zeros(bucket.shape, F32)
    for b in range(N_BUCKETS):
        acc = jnp.where(bucket == b, rb_ref[b, h], acc)
    return acc


def _bias_tile_kernel(rb_ref, o_ref, *, t):
    h = pl.program_id(0)
    d = pl.program_id(1)
    r = lax.broadcasted_iota(jnp.int32, (t, t), 0)
    c = lax.broadcasted_iota(jnp.int32, (t, t), 1)
    rel = d * t + r - c
    val = _bias_lookup(_t5_bucket(rel), rb_ref, h)
    o_ref[0, 0] = jnp.where(rel >= 0, val * LOG2E, -jnp.inf)


def _bias_tiles(rel_bias, t):
    return pl.pallas_call(
        functools.partial(_bias_tile_kernel, t=t),
        grid=(N_HEADS, 2),
        in_specs=[pl.BlockSpec(memory_space=pltpu.SMEM)],
        out_specs=pl.BlockSpec((1, 1, t, t), lambda h, d: (h, d, 0, 0)),
        out_shape=jax.ShapeDtypeStruct((N_HEADS, 2, t, t), F32),
        compiler_params=_params(2),
        name="bias_tiles",
    )(rel_bias)


def _bias_row_kernel(rb_ref, o_ref, *, past):
    width = o_ref.shape[1]
    c = lax.broadcasted_iota(jnp.int32, (1, width), 1)
    bucket = _t5_bucket(jnp.maximum(past - c, 0))
    for h in range(N_HEADS):
        o_ref[h:h + 1, :] = _bias_lookup(bucket, rb_ref, h) * LOG2E


def _bias_rows(rel_bias, past):
    return pl.pallas_call(
        functools.partial(_bias_row_kernel, past=past),
        in_specs=[pl.BlockSpec(memory_space=pltpu.SMEM)],
        out_shape=jax.ShapeDtypeStruct((N_HEADS, past + LANES), F32),
        name="bias_rows",
    )(rel_bias)


def _window_means(ext, u, pos):
    tm = u.shape[0]
    outs = []
    for gi, w in enumerate(POOL_WINDOWS):
        ch = slice(gi * POOL_GROUP_W, (gi + 1) * POOL_GROUP_W)
        s = ext[:, ch]
        span = 1
        while span < w:
            s = s[span:] + s[:-span]
            span *= 2
        off = POOL_HALO - (w - 1)
        win = s[off:off + tm]
        cnt = jnp.minimum(pos + 1, w).astype(F32)
        outs.append(win / cnt - u[:, ch])
    return jnp.concatenate(outs, axis=-1)


def _in_proj_kernel(x_ref, sh_ref, sc_ref, gmix_ref, w_ref, wkt_ref, qg_ref, kg_ref, bd_ref,
                    *rest, tm, seq_pool, n_prev):
    prev_k, prev_v = rest[:n_prev], rest[n_prev:2 * n_prev]
    q_ref, kt_ref, v_ref, u_ref, g_ref = rest[2 * n_prev:2 * n_prev + 5]
    rest = rest[2 * n_prev + 5:]
    for p in range(n_prev):
        kt_ref[p, 0] = prev_k[p][0, 0]
        v_ref[p, 0] = prev_v[p][0, 0]
    h = _rms(x_ref[0]) * gmix_ref[...]
    h = h * (1.0 + sc_ref[0]) + sh_ref[0]
    hb = h.astype(BF16)

    def proj(c0, c1):
        return _dot(hb, w_ref[:, c0:c1])

    zq = proj(0, QK_W)
    msq = _dot((zq * zq).astype(BF16), bd_ref[...])
    q_ref[0] = (zq * lax.rsqrt(msq + EPS) * qg_ref[...]).astype(BF16)

    zk = lax.dot_general(wkt_ref[...], hb, _NT, preferred_element_type=F32)
    zk = zk.reshape(N_HEADS * 2, HEAD_DIM, tm)
    msk = jnp.mean(zk * zk, axis=1, keepdims=True)
    kt_ref[n_prev, 0] = (zk * lax.rsqrt(msk + EPS) * kg_ref[...]).reshape(QK_W, tm)

    c0 = 2 * QK_W
    zv = proj(c0, c0 + ATTN_W)
    for hd in range(N_HEADS):
        v_ref[n_prev, 0, pl.ds(hd, tm, stride=N_HEADS), :] = zv[:, hd * V_DIM:(hd + 1) * V_DIM]
    c0 += ATTN_W
    u = proj(c0, c0 + POOL_W)
    u_ref[0] = u
    c0 += POOL_W
    for c in range(0, 2 * D_MODEL, 512):
        g_ref[0, :, c:c + 512] = jax.nn.sigmoid(proj(c0 + c, c0 + c + 512)).astype(BF16)

    if seq_pool:
        d_ref, carry_ref = rest
        i = pl.program_id(1)

        @pl.when(i == 0)
        def _():
            carry_ref[...] = jnp.zeros_like(carry_ref)

        ext = jnp.concatenate([carry_ref[...], u], axis=0)
        carry_ref[...] = u[tm - POOL_HALO:, :]
        pos = i * tm + lax.broadcasted_iota(jnp.int32, (tm, 1), 0)
        d_ref[0] = _window_means(ext, u, pos).astype(BF16)


def _in_proj(x, mod, gmix, w_bf, wkt_bf, qg, kg, bd, prev_k=(), prev_v=(), *, tm, seq_pool):
    B, T, _ = x.shape
    nt = T // tm
    n_prev = len(prev_k)
    k_block = lambda n: pl.BlockSpec((n, 1, QK_W, tm), lambda b, i: (0, b, 0, i))
    v_block = lambda n: pl.BlockSpec((n, 1, tm * N_HEADS, V_DIM), lambda b, i: (0, b, i, 0))
    out_shape = [jax.ShapeDtypeStruct((B, T, QK_W), BF16),
                 jax.ShapeDtypeStruct((n_prev + 1, B, QK_W, T), F32),
                 jax.ShapeDtypeStruct((n_prev + 1, B, T * N_HEADS, V_DIM), F32),
                 jax.ShapeDtypeStruct((B, T, POOL_W), F32),
                 jax.ShapeDtypeStruct((B, T, 2 * D_MODEL), BF16)]
    out_specs = [_row_spec(tm, QK_W), k_block(n_prev + 1), v_block(n_prev + 1),
                 _row_spec(tm, POOL_W), _row_spec(tm, 2 * D_MODEL)]
    scratch = []
    if seq_pool:
        out_shape.append(jax.ShapeDtypeStruct((B, T, POOL_W), BF16))
        out_specs.append(_row_spec(tm, POOL_W))
        scratch.append(pltpu.VMEM((POOL_HALO, POOL_W), F32))
    return pl.pallas_call(
        functools.partial(_in_proj_kernel, tm=tm, seq_pool=seq_pool, n_prev=n_prev),
        grid=(B, nt),
        in_specs=[_row_spec(tm, D_MODEL), _mod_spec(mod, tm, 0), _mod_spec(mod, tm, 1),
                  _const_spec(gmix), _const_spec(w_bf), _const_spec(wkt_bf),
                  _const_spec(qg), _const_spec(kg), _const_spec(bd)]
                 + [k_block(1)] * n_prev + [v_block(1)] * n_prev,
        out_specs=out_specs,
        out_shape=out_shape,
        scratch_shapes=scratch,
        compiler_params=_params(2),
        name="in_proj",
    )(x, mod, mod, gmix, w_bf, wkt_bf, qg, kg, bd, *prev_k, *prev_v)


def _diff_lambda(lp, layer):
    a = jnp.sum(lp[0:1] * lp[1:2], axis=-1, keepdims=True)
    b = jnp.sum(lp[2:3] * lp[3:4], axis=-1, keepdims=True)
    return jnp.exp(a) - jnp.exp(b) + _lambda_init(layer)


def _attn_kernel(qi_ref, kj_ref, q_ref, kt_ref, v_ref, bias_ref, rb_ref, lp_ref, subln_ref,
                 o_ref, m_ref, l_ref, acc_ref, s_ref, p_ref, alpha_ref, *, t, layer):
    s_idx = pl.program_id(1)
    i = qi_ref[s_idx]
    j = kj_ref[s_idx]

    @pl.when(j == 0)
    def _():
        m_ref[...] = jnp.full_like(m_ref, -jnp.inf)
        l_ref[...] = jnp.zeros_like(l_ref)
        acc_ref[...] = jnp.zeros_like(acc_ref)

    lane = lax.broadcasted_iota(jnp.int32, (1, V_DIM), 1)
    map0 = jnp.where(lane < HEAD_DIM, 1.0, 0.0).astype(BF16)
    map1 = jnp.where(lane >= HEAD_DIM, 1.0, 0.0).astype(BF16)
    far = (i - j) >= 2
    for h in range(N_HEADS):
        hs = slice(h * V_DIM, (h + 1) * V_DIM)
        q = q_ref[0, :, hs]
        kb = kt_ref[0, 0, hs, :].astype(BF16)
        vb = jnp.concatenate([v_ref[0, 0, pl.ds(h, t, stride=N_HEADS), :].astype(BF16),
                              jnp.ones((t, V_DIM), BF16)], axis=1)
        far_bias = rb_ref[N_BUCKETS - 1, h] * LOG2E
        for c in range(2 * t // ATTN_ROWS):
            qrow = (c * ATTN_ROWS) % t
            qc = q[qrow:qrow + ATTN_ROWS] * (map0 if c * ATTN_ROWS < t else map1)
            s_ref[...] = _dot(qc, kb)
            row0 = c * ATTN_ROWS

            def row_max(k, carry, h=h, qrow=qrow, row0=row0, far_bias=far_bias):
                r0 = pl.multiple_of(k * ATTN_SUB, ATTN_SUB)
                rows = pl.ds(row0 + r0, ATTN_SUB)
                bias = jnp.where(far, far_bias, bias_ref[h, 0, pl.ds(qrow + r0, ATTN_SUB), :])
                s = s_ref[pl.ds(r0, ATTN_SUB), :] + bias
                s_ref[pl.ds(r0, ATTN_SUB), :] = s
                m_prev = m_ref[h, rows, :]
                m_new = jnp.maximum(m_prev, jnp.max(s, axis=-1, keepdims=True))
                alpha_ref[pl.ds(r0, ATTN_SUB), :] = jnp.broadcast_to(jnp.exp2(m_prev - m_new),
                                                                     (ATTN_SUB, V_DIM))
                m_ref[h, rows, :] = m_new
                return carry

            def row_exp(k, carry, h=h, row0=row0):
                r0 = pl.multiple_of(k * ATTN_SUB, ATTN_SUB)
                m = m_ref[h, pl.ds(row0 + r0, ATTN_SUB), :]
                p_ref[pl.ds(r0, ATTN_SUB), :] = jnp.exp2(s_ref[pl.ds(r0, ATTN_SUB), :] - m).astype(BF16)
                return carry

            lax.fori_loop(0, ATTN_ROWS // ATTN_SUB, row_max, 0, unroll=ATTN_UNROLL)
            lax.fori_loop(0, ATTN_ROWS // ATTN_SUB, row_exp, 0, unroll=ATTN_UNROLL)
            blk = slice(row0, row0 + ATTN_ROWS)
            pv = _dot(p_ref[...], vb)
            alpha = alpha_ref[...]
            acc_ref[h, blk, :] = alpha * acc_ref[h, blk, :] + pv[:, :V_DIM]
            l_ref[h, blk, :] = alpha * l_ref[h, blk, :] + pv[:, V_DIM:]

    @pl.when(j == i)
    def _():
        lam = _diff_lambda(lp_ref[...], layer)
        for h in range(N_HEADS):
            o2 = acc_ref[h] / l_ref[h]
            o = o2[:t] - lam * o2[t:]
            o = _rms(o) * subln_ref[...] * (1.0 - _lambda_init(layer))
            o_ref[0, :, h * V_DIM:(h + 1) * V_DIM] = o.astype(BF16)


def _attn_prompt(q, kt, v, bias_tiles, rel_bias, lp, subln, *, layer, t):
    B, T, _ = q.shape
    kv_layer = kt.shape[0] - 1
    n = T // t
    pairs = [(i, j) for i in range(n) for j in range(i + 1)]
    qi = jnp.asarray([p[0] for p in pairs], jnp.int32)
    kj = jnp.asarray([p[1] for p in pairs], jnp.int32)
    grid_spec = pltpu.PrefetchScalarGridSpec(
        num_scalar_prefetch=2,
        grid=(B, len(pairs)),
        in_specs=[pl.BlockSpec((1, t, QK_W), lambda b, s, qi, kj: (b, qi[s], 0)),
                  pl.BlockSpec((1, 1, QK_W, t), lambda b, s, qi, kj: (kv_layer, b, 0, kj[s])),
                  pl.BlockSpec((1, 1, t * N_HEADS, V_DIM), lambda b, s, qi, kj: (kv_layer, b, kj[s], 0)),
                  pl.BlockSpec((N_HEADS, 1, t, t),
                               lambda b, s, qi, kj: (0, jnp.minimum(qi[s] - kj[s], 1), 0, 0)),
                  pl.BlockSpec(memory_space=pltpu.SMEM),
                  pl.BlockSpec(lp.shape, lambda *_: (0, 0)),
                  pl.BlockSpec(subln.shape, lambda *_: (0, 0))],
        out_specs=pl.BlockSpec((1, t, ATTN_W), lambda b, s, qi, kj: (b, qi[s], 0)),
        scratch_shapes=[pltpu.VMEM((N_HEADS, 2 * t, 1), F32),
                        pltpu.VMEM((N_HEADS, 2 * t, V_DIM), F32),
                        pltpu.VMEM((N_HEADS, 2 * t, V_DIM), F32),
                        pltpu.VMEM((ATTN_ROWS, t), F32),
                        pltpu.VMEM((ATTN_ROWS, t), BF16),
                        pltpu.VMEM((ATTN_ROWS, V_DIM), F32)])
    return pl.pallas_call(
        functools.partial(_attn_kernel, t=t, layer=layer),
        grid_spec=grid_spec,
        out_shape=jax.ShapeDtypeStruct((B, T, ATTN_W), BF16),
        compiler_params=_params(2),
        name="attn_prompt",
    )(qi, kj, q, kt, v, bias_tiles, rel_bias, lp, subln)


def _attn_sample_kernel(pt_ref, q_ref, kn_ref, vn_ref, *refs, n_pages, page, layer):
    k_refs = refs[:n_pages]
    v_refs = refs[n_pages:2 * n_pages]
    bias_ref, lp_ref, subln_ref, o_ref = refs[2 * n_pages:]
    rows = 2 * N_HEADS
    row = lax.broadcasted_iota(jnp.int32, (rows, QK_W), 0)
    seg = lax.broadcasted_iota(jnp.int32, (rows, QK_W), 1) // HEAD_DIM
    q = jnp.broadcast_to(q_ref[0].astype(F32), (rows, QK_W))
    qbd_f = jnp.where(seg == (row % N_HEADS) * 2 + row // N_HEADS, q, 0.0)
    qbd = qbd_f.astype(BF16)

    s = jnp.concatenate([_dot(qbd, k_refs[p][0, 0].astype(BF16)) for p in range(n_pages)], axis=1)
    past = n_pages * page
    bias = bias_ref[...]
    bias = jnp.concatenate([bias, bias], axis=0)
    s = s + bias[:, :past]
    s_new = jnp.sum(qbd_f * kn_ref[0], axis=-1, keepdims=True) + bias[:, past:past + 1]
    m = jnp.maximum(jnp.max(s, axis=-1, keepdims=True), s_new)
    p = jnp.exp2(s - m)
    p_new = jnp.exp2(s_new - m)
    denom = jnp.sum(p, axis=-1, keepdims=True) + p_new
    p = p / denom
    p_new = p_new / denom

    lam = _diff_lambda(lp_ref[...], layer)
    first = lax.broadcasted_iota(jnp.int32, (rows, 1), 0) < N_HEADS
    a = jnp.where(first, p - lam * pltpu.roll(p, N_HEADS, 0), 0.0).astype(BF16)
    a_new = p_new[:N_HEADS] - lam * p_new[N_HEADS:]

    out = jnp.zeros((rows, ATTN_W), F32)
    for pg in range(n_pages):
        vp = jnp.concatenate([v_refs[pg][0, 0, pl.ds(h, page, stride=N_HEADS), :]
                              for h in range(N_HEADS)], axis=1)
        out = out + _dot(a[:, pg * page:(pg + 1) * page], vp.astype(BF16))
    out = out[:N_HEADS] + a_new * vn_ref[0]
    rowh = lax.broadcasted_iota(jnp.int32, (N_HEADS, ATTN_W), 0)
    head = lax.broadcasted_iota(jnp.int32, (N_HEADS, ATTN_W), 1) // V_DIM
    om = jnp.where(rowh == head, out, 0.0)
    ms = jnp.sum(om * om, axis=-1, keepdims=True) / V_DIM
    on = om * lax.rsqrt(ms + EPS)
    o = jnp.sum(on, axis=0, keepdims=True) * subln_ref[...] * (1.0 - _lambda_init(layer))
    o_ref[0] = o.astype(BF16)


def _attn_sample(q, k_new, v_new, cache_kt, cache_vr, page_table, bias_rows, lp, subln4, *, layer):
    DB = q.shape[0]
    n_pages = page_table.shape[1]
    page = cache_kt.shape[3]

    def page_spec(p):
        return pl.BlockSpec((1, 1, QK_W, page), lambda b, pt: (layer, pt[b, p], 0, 0))

    def vpage_spec(p):
        return pl.BlockSpec((1, 1, page * N_HEADS, V_DIM), lambda b, pt: (layer, pt[b, p], 0, 0))

    vec = pl.BlockSpec((1, 1, QK_W), lambda b, pt: (b, 0, 0))
    grid_spec = pltpu.PrefetchScalarGridSpec(
        num_scalar_prefetch=1,
        grid=(DB,),
        in_specs=([vec, vec, vec] + [page_spec(p) for p in range(n_pages)]
                  + [vpage_spec(p) for p in range(n_pages)]
                  + [pl.BlockSpec(bias_rows.shape, lambda *_: (0, 0)),
                     pl.BlockSpec(lp.shape, lambda *_: (0, 0)),
                     pl.BlockSpec(subln4.shape, lambda *_: (0, 0))]),
        out_specs=vec)
    return pl.pallas_call(
        functools.partial(_attn_sample_kernel, n_pages=n_pages, page=page, layer=layer),
        grid_spec=grid_spec,
        out_shape=jax.ShapeDtypeStruct((DB, 1, ATTN_W), BF16),
        compiler_params=_params(1),
        name="attn_sample",
    )(page_table, q, k_new, v_new, *([cache_kt] * n_pages), *([cache_vr] * n_pages),
      bias_rows, lp, subln4)


def _pool_sample_kernel(state_ref, u_ref, d_ref):
    u = u_ref[...]
    outs = []
    for gi, w in enumerate(POOL_WINDOWS):
        ch = slice(gi * POOL_GROUP_W, (gi + 1) * POOL_GROUP_W)
        s = u[:, ch]
        for r in range(POOL_BUF - (w - 1), POOL_BUF):
            s = s + state_ref[r, :, ch]
        outs.append(s / float(w) - u[:, ch])
    d_ref[...] = jnp.concatenate(outs, axis=-1).astype(BF16)


def _pool_sample(state_t, u):
    return pl.pallas_call(
        _pool_sample_kernel,
        out_shape=jax.ShapeDtypeStruct(u.shape, BF16),
        name="pool_sample",
    )(state_t, u)


def _mix_kernel(o_ref, d_ref, g_ref, x_ref, gt_ref, sh_ref, sc_ref, wp_ref, ps_ref, wb_ref, wo_ref,
                gffn_ref, *rest, router):
    d = d_ref[0]
    y = jnp.concatenate([_dot(d[:, g * POOL_GROUP_W:(g + 1) * POOL_GROUP_W], wp_ref[g])
                         for g in range(len(POOL_WINDOWS))], axis=-1)
    y = (y * ps_ref[...]).astype(BF16)
    pa = _dot(o_ref[0], wb_ref[:ATTN_W])
    pb = _dot(y, wb_ref[ATTN_W:])
    g = g_ref[0]
    merged = g[:, :D_MODEL].astype(F32) * pa + g[:, D_MODEL:].astype(F32) * pb
    xn = x_ref[0] + gt_ref[0] * _dot(merged.astype(BF16), wo_ref[...])
    h2 = _rms(xn) * gffn_ref[...]
    h2 = h2 * (1.0 + sc_ref[0]) + sh_ref[0]
    if not router:
        xn_ref, h2_ref = rest
        xn_ref[0] = xn
        h2_ref[0] = h2.astype(h2_ref.dtype)
        return

    wr_hi_ref, wr_lo_ref, xn_ref, h2_ref, ri_ref, rp_ref = rest
    xn_ref[0] = xn
    _slab_store(h2_ref, (0,), h2)
    hi = h2.astype(BF16)
    lo = (h2 - hi.astype(F32)).astype(BF16)
    logits = _dot(hi, wr_hi_ref[...]) + _dot(lo, wr_hi_ref[...]) + _dot(hi, wr_lo_ref[...])
    lane = lax.broadcasted_iota(jnp.int32, logits.shape, 1)
    lg = jnp.where(lane < N_EXPERTS, logits, -jnp.inf)
    m1 = jnp.max(lg, axis=-1, keepdims=True)
    i1 = jnp.min(jnp.where(lg == m1, lane, LANES), axis=-1, keepdims=True)
    lg = jnp.where(lane == i1, -jnp.inf, lg)
    m2 = jnp.max(lg, axis=-1, keepdims=True)
    i2 = jnp.min(jnp.where(lg == m2, lane, LANES), axis=-1, keepdims=True)
    e = jnp.exp(m2 - m1)
    ri_ref[0] = jnp.concatenate([i1, i2], axis=-1)
    rp_ref[0] = jnp.concatenate([1.0 / (1.0 + e), e / (1.0 + e)], axis=-1)


def _mix(o, d, g, x, mod, wp_bf, pscale, wb_bf, wo_bf, gffn, router_w, *, tm):
    B, T, _ = x.shape
    router = router_w is not None
    ins = [o, d, g, x, mod, mod, mod, wp_bf, pscale, wb_bf, wo_bf, gffn]
    in_specs = [_row_spec(tm, ATTN_W), _row_spec(tm, POOL_W), _row_spec(tm, 2 * D_MODEL),
                _row_spec(tm, D_MODEL), _mod_spec(mod, tm, 2), _mod_spec(mod, tm, 3),
                _mod_spec(mod, tm, 4)] + [_const_spec(a) for a in ins[7:]]
    out_shape = [jax.ShapeDtypeStruct((B, T, D_MODEL), F32), jax.ShapeDtypeStruct((B, T, D_MODEL), BF16)]
    out_specs = [_row_spec(tm, D_MODEL), _row_spec(tm, D_MODEL)]
    if router:
        out_shape[1] = jax.ShapeDtypeStruct((B, T * ROW_SLAB, LANES), F32)
        out_specs[1] = _row_spec(tm * ROW_SLAB, LANES)
        ins += list(router_w)
        in_specs += [_const_spec(a) for a in router_w]
        out_shape += [jax.ShapeDtypeStruct((B, T, TOP_K), jnp.int32),
                      jax.ShapeDtypeStruct((B, T, TOP_K), F32)]
        out_specs += [_row_spec(tm, TOP_K), _row_spec(tm, TOP_K)]
    return pl.pallas_call(
        functools.partial(_mix_kernel, router=router),
        grid=(B, T // tm),
        in_specs=in_specs,
        out_specs=out_specs,
        out_shape=out_shape,
        compiler_params=_params(2),
        name="mix",
    )(*ins)


def _ffn_kernel(h_ref, x_ref, gt_ref, wg_ref, wu_ref, wd_ref, y_ref, acc_ref):
    f = pl.program_id(2)

    @pl.when(f == 0)
    def _():
        acc_ref[...] = jnp.zeros_like(acc_ref)

    h = h_ref[0]
    a = (_silu(_dot(h, wg_ref[...])) * _dot(h, wu_ref[...])).astype(BF16)
    acc_ref[...] += _dot(a, wd_ref[...])

    @pl.when(f == pl.num_programs(2) - 1)
    def _():
        y_ref[0] = x_ref[0] + gt_ref[0] * acc_ref[...]


def _ffn_dense(h2, x, mod, wg_bf, wu_bf, wd_bf, *, tm):
    B, T, _ = x.shape
    d_ff = wg_bf.shape[1]
    tf = d_ff // 2
    return pl.pallas_call(
        _ffn_kernel,
        grid=(B, T // tm, d_ff // tf),
        in_specs=[_row_spec(tm, D_MODEL), _row_spec(tm, D_MODEL), _mod_spec(mod, tm, 5),
                  pl.BlockSpec((D_MODEL, tf), lambda b, i, f: (0, f)),
                  pl.BlockSpec((D_MODEL, tf), lambda b, i, f: (0, f)),
                  pl.BlockSpec((tf, D_MODEL), lambda b, i, f: (f, 0))],
        out_specs=_row_spec(tm, D_MODEL),
        out_shape=jax.ShapeDtypeStruct((B, T, D_MODEL), F32),
        scratch_shapes=[pltpu.VMEM((tm, D_MODEL), F32)],
        compiler_params=_params(3),
        name="ffn_dense",
    )(h2, x, mod, wg_bf, wu_bf, wd_bf)


ROW_SLAB = D_MODEL // LANES


def _slab_copy(src_hbm, src_row, dst, dst_row, sem):
    src = src_hbm.at[pl.ds(pl.multiple_of(src_row * ROW_SLAB, ROW_SLAB), ROW_SLAB), :]
    return pltpu.make_async_copy(src, dst.at[pl.ds(pl.multiple_of(dst_row * ROW_SLAB, ROW_SLAB), ROW_SLAB), :], sem)


def _slab_store(ref, idx, x):
    rows = x.shape[0]
    for s in range(ROW_SLAB):
        ref[idx + (pl.ds(s, rows, stride=ROW_SLAB), slice(None))] = x[:, s * LANES:(s + 1) * LANES]


def _slab_load(ref, idx, rows, start=0):
    return jnp.concatenate([ref[idx + (pl.ds(start + s, rows, stride=ROW_SLAB), slice(None))]
                            for s in range(ROW_SLAB)], axis=1)


def _moe_ffn_kernel(te_ref, nu_ref, src_ref, src_next_ref, h_hbm, wg_ref, wu_ref, wd_ref, y_ref,
                    xbuf, xb_ref, acc_ref, sem, *, tm):
    i = pl.program_id(0)
    f = pl.program_id(1)
    nu = nu_ref[0]
    slot = i % 2

    def issue(idx_ref, dst):
        def body(r, carry):
            _slab_copy(h_hbm, idx_ref[0, 0, r], xbuf.at[dst], r, sem.at[dst]).start()
            return carry
        lax.fori_loop(0, tm, body, 0, unroll=8)

    def drain(dst):
        def body(r, carry):
            _slab_copy(h_hbm, 0, xbuf.at[dst], 0, sem.at[dst]).wait()
            return carry
        lax.fori_loop(0, tm, body, 0, unroll=8)

    @pl.when(i < nu)
    def _():
        @pl.when(f == 0)
        def _():
            @pl.when(i == 0)
            def _():
                issue(src_ref, 0)

            drain(slot)

            @pl.when(i + 1 < nu)
            def _():
                issue(src_next_ref, 1 - slot)

            acc_ref[...] = jnp.zeros_like(acc_ref)
            xb_ref[...] = _slab_load(xbuf, (slot,), tm).astype(BF16)

        xb = xb_ref[...]
        a = (_silu(_dot(xb, wg_ref[0].astype(BF16))) * _dot(xb, wu_ref[0].astype(BF16))).astype(BF16)
        acc_ref[...] += _dot(a, wd_ref[0].astype(BF16))

        @pl.when(f == pl.num_programs(1) - 1)
        def _():
            _slab_store(y_ref, (), acc_ref[...])

    @pl.when(i >= nu)
    def _():
        y_ref[...] = jnp.zeros_like(y_ref)


def _moe_ffn(h_rows, src, tile_expert, n_used, wg, wu, wd, *, tm, tf):
    n_tiles = src.shape[0]
    d_ff = wg.shape[2]
    assert d_ff % tf == 0, (d_ff, tf)
    nf = d_ff // tf

    def tile(i, nu):
        return jnp.minimum(i, nu[0] - 1)

    def ff(i, f, nu):
        return jnp.where(i < nu[0], f, nf - 1)

    grid_spec = pltpu.PrefetchScalarGridSpec(
        num_scalar_prefetch=2,
        grid=(n_tiles, nf),
        in_specs=[pl.BlockSpec((1, 1, tm), lambda i, f, te, nu: (i, 0, 0), memory_space=pltpu.SMEM),
                  pl.BlockSpec((1, 1, tm), lambda i, f, te, nu: (jnp.minimum(i + 1, n_tiles - 1), 0, 0),
                               memory_space=pltpu.SMEM),
                  pl.BlockSpec(memory_space=pl.ANY),
                  pl.BlockSpec((1, D_MODEL, tf), lambda i, f, te, nu: (te[tile(i, nu)], 0, ff(i, f, nu))),
                  pl.BlockSpec((1, D_MODEL, tf), lambda i, f, te, nu: (te[tile(i, nu)], 0, ff(i, f, nu))),
                  pl.BlockSpec((1, tf, D_MODEL), lambda i, f, te, nu: (te[tile(i, nu)], ff(i, f, nu), 0))],
        out_specs=pl.BlockSpec((tm * ROW_SLAB, LANES), lambda i, f, te, nu: (i, 0)),
        scratch_shapes=[pltpu.VMEM((2, tm * ROW_SLAB, LANES), F32), pltpu.VMEM((tm, D_MODEL), BF16),
                        pltpu.VMEM((tm, D_MODEL), F32),
                        pltpu.SemaphoreType.DMA((2,))])
    return pl.pallas_call(
        functools.partial(_moe_ffn_kernel, tm=tm),
        grid_spec=grid_spec,
        out_shape=jax.ShapeDtypeStruct((n_tiles * tm * ROW_SLAB, LANES), F32),
        compiler_params=_params(2),
        name="moe_ffn",
    )(tile_expert, n_used, src, src, h_rows, wg, wu, wd)


def _combine_kernel(pos_ref, pos_next_ref, y_hbm, x_ref, gt_ref, p_ref, o_ref, buf, sem, *, tc):
    g = pl.program_id(0) * pl.num_programs(1) + pl.program_id(1)
    n_steps = pl.num_programs(0) * pl.num_programs(1)
    slot = g % 2

    def issue(idx_ref, dst):
        def body(t, carry):
            for k in range(TOP_K):
                _slab_copy(y_hbm, idx_ref[0, 0, TOP_K * t + k], buf.at[dst, k], t, sem.at[dst]).start()
            return carry
        lax.fori_loop(0, tc, body, 0, unroll=4)

    @pl.when(g == 0)
    def _():
        issue(pos_ref, 0)

    def drain(t, carry):
        for k in range(TOP_K):
            _slab_copy(y_hbm, 0, buf.at[slot, 0], 0, sem.at[slot]).wait()
        return carry

    lax.fori_loop(0, tc, drain, 0, unroll=4)

    @pl.when(g + 1 < n_steps)
    def _():
        issue(pos_next_ref, 1 - slot)

    p = p_ref[0]
    f = p[:, 0:1] * _slab_load(buf, (slot, 0), tc) + p[:, 1:2] * _slab_load(buf, (slot, 1), tc)
    o_ref[0] = x_ref[0] + gt_ref[0] * f


def _combine(y, pos, probs, x, mod, *, tc):
    B, T, _ = x.shape
    nt = T // tc
    last = B * nt - 1
    pos_chunks = pos.reshape(B * nt, 1, TOP_K * tc)
    return pl.pallas_call(
        functools.partial(_combine_kernel, tc=tc),
        grid=(B, nt),
        in_specs=[pl.BlockSpec((1, 1, TOP_K * tc), lambda b, i: (b * nt + i, 0, 0),
                               memory_space=pltpu.SMEM),
                  pl.BlockSpec((1, 1, TOP_K * tc), lambda b, i: (jnp.minimum(b * nt + i + 1, last), 0, 0),
                               memory_space=pltpu.SMEM),
                  pl.BlockSpec(memory_space=pl.ANY),
                  _row_spec(tc, D_MODEL), _mod_spec(mod, tc, 5), _row_spec(tc, TOP_K)],
        out_specs=_row_spec(tc, D_MODEL),
        out_shape=jax.ShapeDtypeStruct(x.shape, F32),
        scratch_shapes=[pltpu.VMEM((2, TOP_K, tc * ROW_SLAB, LANES), F32), pltpu.SemaphoreType.DMA((2,))],
        compiler_params=_params(2),
        name="moe_combine",
    )(pos_chunks, pos_chunks, y, x, mod, probs)


def _route_plan(top_i, tm, n_tiles):
    e_flat = top_i.reshape(-1)
    onehot = (e_flat[:, None] == jnp.arange(N_EXPERTS, dtype=jnp.int32)[None, :]).astype(jnp.int32)
    csum = jnp.cumsum(onehot, axis=0)
    cnt = csum[-1]
    gsz = (cnt + tm - 1) // tm * tm
    gend = jnp.cumsum(gsz)
    gstart = gend - gsz
    pos = jnp.sum(onehot * (gstart[None, :] + csum - 1), axis=1).astype(jnp.int32)
    n_used = (gend[-1] // tm).astype(jnp.int32).reshape(1)
    tile_start = jnp.arange(n_tiles, dtype=jnp.int32) * tm
    tile_expert = jnp.sum((tile_start[:, None] >= gend[None, :]).astype(jnp.int32), axis=1)
    tile_expert = jnp.minimum(tile_expert, N_EXPERTS - 1).astype(jnp.int32)
    token = jnp.arange(e_flat.shape[0], dtype=jnp.int32) // TOP_K
    src = jnp.zeros((n_tiles * tm,), jnp.int32).at[pos].set(token, unique_indices=True)
    return pos.reshape(top_i.shape), src.reshape(n_tiles, 1, tm), tile_expert, n_used


def _moe(groups, wg, wu, wd):
    tm = MOE_ROW_TILE
    counts = [g[1].shape[0] * g[1].shape[1] for g in groups]
    total = sum(counts) * TOP_K
    n_tiles = (total + N_EXPERTS * (tm - 1)) // tm + 1
    top_all = jnp.concatenate([g[3].reshape(-1, TOP_K) for g in groups], axis=0)
    h_all = jnp.concatenate([g[0].reshape(-1, LANES) for g in groups], axis=0)
    pos_all, src, tile_expert, n_used = _route_plan(top_all, tm, n_tiles)
    y = _moe_ffn(h_all, src, tile_expert, n_used, wg, wu, wd, tm=tm, tf=MOE_FF_TILE)
    outs = []
    offs = 0
    for (h2, x, mod, top_i, probs), cnt in zip(groups, counts):
        pos = pos_all[offs:offs + cnt].reshape(top_i.shape)
        offs += cnt
        outs.append(_combine(y, pos, probs, x, mod, tc=min(MOE_DMA_CHUNK, x.shape[1])))
    return outs


def kernel(x_prompt, x_sample, c_prompt, c_sample, cache_k, cache_v, state_pool, page_table,
           rel_bias, w_ada, b_ada, g_mix, g_ffn, w_in, q_norm, k_norm, diff_lambda, subln,
           w_pool, pool_scale, w_branch, w_out, w_ff_gate, w_ff_up, w_ff_down, w_router,
           w_exp_gate, w_exp_up, w_exp_down):
    B, S, D = x_prompt.shape
    DB = x_sample.shape[0]
    depth = w_in.shape[0]
    n_phys, page = cache_k.shape[1], cache_k.shape[2]
    past = page_table.shape[1] * page

    mod_all = _ada_mod(jnp.concatenate([c_prompt, c_sample], axis=0), w_ada, b_ada)
    bias_tiles = _bias_tiles(rel_bias, ATTN_TILE)
    bias_rows = _bias_rows(rel_bias, past)

    cache_kt = jnp.transpose(cache_k, (0, 1, 3, 4, 5, 2)).reshape(depth, n_phys, QK_W, page)
    cache_vr = cache_v.reshape(depth, n_phys, page * N_HEADS, V_DIM)
    state_t = jnp.transpose(state_pool, (0, 2, 1, 3))

    bd = jnp.kron(jnp.eye(QK_W // HEAD_DIM, dtype=F32),
                  jnp.full((HEAD_DIM, HEAD_DIM), 1.0 / HEAD_DIM, F32)).astype(BF16)

    xp = x_prompt
    xs = x_sample.reshape(1, DB, D)
    outs = {k: [] for k in ("pp", "ks", "vs", "ps")}
    prev_kv = ((), ())
    for l in range(depth):
        mod_p = mod_all[l, :B].reshape(B, 1, 6 * D)
        mod_s = mod_all[l, B:].reshape(1, DB, 6 * D)
        w_bf = w_in[l].astype(BF16)
        wkt_bf = w_in[l][:, QK_W:2 * QK_W].T.astype(BF16)
        qg = jnp.tile(q_norm[l].reshape(1, 2 * HEAD_DIM), (1, N_HEADS)) * (HEAD_DIM ** -0.5 * LOG2E)
        kg = jnp.tile(k_norm[l], (N_HEADS, 1)).reshape(N_HEADS * 2, HEAD_DIM, 1)
        gmix = g_mix[l].reshape(1, D)
        gffn = g_ffn[l].reshape(1, D)
        lp = diff_lambda[l]
        sub1 = subln[l].reshape(1, V_DIM)
        sub4 = jnp.tile(sub1, (1, N_HEADS))
        wp_bf = w_pool[l].astype(BF16)
        pscale = pool_scale[l].reshape(1, POOL_W)
        wb_bf = w_branch[l].astype(BF16)
        wo_bf = w_out[l].astype(BF16)
        moe = l % 2 == 1
        j = l // 2
        router_w = None
        if moe:
            wr = jnp.pad(w_router[j], ((0, 0), (0, LANES - N_EXPERTS)))
            wr_hi = wr.astype(BF16)
            router_w = (wr_hi, (wr - wr_hi.astype(F32)).astype(BF16))

        stack = prev_kv if l == depth - 1 else ((), ())
        qp, ktp, vp, up, gp, dp = _in_proj(xp, mod_p, gmix, w_bf, wkt_bf, qg, kg, bd, *stack,
                                           tm=ROW_TILE, seq_pool=True)
        prev_kv = (prev_kv[0] + (ktp,), prev_kv[1] + (vp,))
        op = _attn_prompt(qp, ktp, vp, bias_tiles, rel_bias, lp, sub1, layer=l, t=ATTN_TILE)
        mix_p = _mix(op, dp, gp, xp, mod_p, wp_bf, pscale, wb_bf, wo_bf, gffn, router_w, tm=ROW_TILE)

        qs, kts, vs, us, gs = _in_proj(xs, mod_s, gmix, w_bf, wkt_bf, qg, kg, bd,
                                       tm=DB, seq_pool=False)
        ks_rows = jnp.transpose(kts[0], (0, 2, 1)).reshape(DB, 1, QK_W)
        osamp = _attn_sample(qs.reshape(DB, 1, QK_W), ks_rows, vs.reshape(DB, 1, ATTN_W),
                             cache_kt, cache_vr, page_table, bias_rows, lp, sub4, layer=l)
        ds = _pool_sample(state_t[l], us[0])
        mix_s = _mix(osamp.reshape(1, DB, ATTN_W), ds.reshape(1, DB, POOL_W), gs, xs, mod_s,
                     wp_bf, pscale, wb_bf, wo_bf, gffn, router_w, tm=DB)

        if moe:
            xp, xs = _moe([(mix_p[1], mix_p[0], mod_p, mix_p[2], mix_p[3]),
                           (mix_s[1], mix_s[0], mod_s, mix_s[2], mix_s[3])],
                          w_exp_gate[j], w_exp_up[j], w_exp_down[j])
        else:
            wg_bf = w_ff_gate[j].astype(BF16)
            wu_bf = w_ff_up[j].astype(BF16)
            wd_bf = w_ff_down[j].astype(BF16)
            xp = _ffn_dense(mix_p[1], mix_p[0], mod_p, wg_bf, wu_bf, wd_bf, tm=FFN_ROW_TILE)
            xs = _ffn_dense(mix_s[1], mix_s[0], mod_s, wg_bf, wu_bf, wd_bf, tm=DB)

        outs["pp"].append(up[:, S - POOL_BUF:, :])
        outs["ks"].append(ks_rows.reshape(DB, 1, N_HEADS, 2, HEAD_DIM))
        outs["vs"].append(vs.reshape(DB, 1, N_HEADS, V_DIM))
        outs["ps"].append(jnp.concatenate([state_pool[l][:, 1:], us.reshape(DB, 1, POOL_W)], axis=1))

    return (xp, xs.reshape(DB, 1, D),
            jnp.transpose(ktp.reshape(depth, B, N_HEADS, 2, HEAD_DIM, S), (0, 1, 5, 2, 3, 4)),
            vp.reshape(depth, B, S, N_HEADS, V_DIM), jnp.stack(outs["pp"]),
            jnp.stack(outs["ks"]), jnp.stack(outs["vs"]), jnp.stack(outs["ps"]))
```

```python
import functools
import math

import jax
import jax.numpy as jnp
from jax import lax
from jax.experimental import pallas as pl
from jax.experimental.pallas import tpu as pltpu

F32 = jnp.float32
BF16 = jnp.bfloat16

D_MODEL = 1024
N_HEADS = 4
HEAD_DIM = 64
V_DIM = 2 * HEAD_DIM
QK_W = N_HEADS * 2 * HEAD_DIM
ATTN_W = N_HEADS * V_DIM
POOL_WINDOWS = (2, 4, 8, 16)
POOL_W = D_MODEL // 2
POOL_GROUP_W = POOL_W // len(POOL_WINDOWS)
POOL_BUF = max(POOL_WINDOWS) - 1
POOL_HALO = max(POOL_WINDOWS)
IN_COLS = 2 * QK_W + ATTN_W + POOL_W + 2 * D_MODEL
N_BUCKETS = 32
MAX_DISTANCE = 128
N_EXPERTS = 8
TOP_K = 2
EPS = 1e-6
LOG2E = math.log2(math.e)
LANES = 128
V7X_VMEM_BYTES = 64 * 1024 * 1024
VMEM_LIMIT = V7X_VMEM_BYTES * 7 // 8

ATTN_TILE = 512
ATTN_ROWS = 256
ATTN_SUB = 32
ATTN_UNROLL = ATTN_ROWS // ATTN_SUB
ROW_TILE = 512
FFN_ROW_TILE = 1024
MOE_ROW_TILE = 1024
MOE_FF_TILE = 512
MOE_DMA_CHUNK = 512

_NT = (((1,), (1,)), ((), ()))


def _lambda_init(layer):
    return 0.8 - 0.6 * math.exp(-0.3 * layer)


def _params(n_axes, vmem=VMEM_LIMIT):
    return pltpu.CompilerParams(dimension_semantics=("arbitrary",) * n_axes,
                                vmem_limit_bytes=vmem)


def _dot(a, b):
    return jnp.dot(a, b, preferred_element_type=F32)


def _rms(x):
    return x * lax.rsqrt(jnp.mean(x * x, axis=-1, keepdims=True) + EPS)


def _silu(x):
    return x * jax.nn.sigmoid(x)


def _const_spec(a):
    nd = a.ndim
    return pl.BlockSpec(a.shape, lambda *_: (0,) * nd)


def _row_spec(tm, width):
    return pl.BlockSpec((1, tm, width), lambda b, i, *_: (b, i, 0))


def _mod_spec(mod, tm, chunk):
    if mod.shape[1] == 1:
        return pl.BlockSpec((1, 1, D_MODEL), lambda b, i, *_: (b, 0, chunk))
    return pl.BlockSpec((1, tm, D_MODEL), lambda b, i, *_: (b, i, chunk))


def _ada_kernel(c_ref, w_ref, b_ref, o_ref):
    a = _silu(c_ref[...]).astype(BF16)
    o_ref[0] = _dot(a, w_ref[0].astype(BF16)) + b_ref[0]


def _ada_mod(c_all, w_ada, b_ada):
    rows = c_all.shape[0]
    depth, _, cols = w_ada.shape
    tn = cols // 4
    return pl.pallas_call(
        _ada_kernel,
        grid=(depth, cols // tn),
        in_specs=[pl.BlockSpec((rows, D_MODEL), lambda l, n: (0, 0)),
                  pl.BlockSpec((1, D_MODEL, tn), lambda l, n: (l, 0, n)),
                  pl.BlockSpec((1, 1, tn), lambda l, n: (l, 0, n))],
        out_specs=pl.BlockSpec((1, rows, tn), lambda l, n: (l, 0, n)),
        out_shape=jax.ShapeDtypeStruct((depth, rows, cols), F32),
        compiler_params=_params(2),
        name="ada_mod",
    )(c_all, w_ada, b_ada.reshape(depth, 1, cols))


def _t5_bucket(rel):
    n = jnp.maximum(rel, 0)
    max_exact = N_BUCKETS // 2
    nf = jnp.maximum(n, 1).astype(F32)
    large = max_exact + (jnp.log(nf / max_exact) / math.log(MAX_DISTANCE / max_exact)
                         * (N_BUCKETS - max_exact)).astype(jnp.int32)
    large = jnp.minimum(large, N_BUCKETS - 1)
    return jnp.where(n < max_exact, n, large)


def _bias_lookup(bucket, rb_ref, h):
    acc = jnp.zeros(bucket.shape, F32)
    for b in range(N_BUCKETS):
        acc = jnp.where(bucket == b, rb_ref[b, h], acc)
    return acc


def _bias_tile_kernel(rb_ref, o_ref, *, t):
    h = pl.program_id(0)
    d = pl.program_id(1)
    r = lax.broadcasted_iota(jnp.int32, (t, t), 0)
    c = lax.broadcasted_iota(jnp.int32, (t, t), 1)
    rel = d * t + r - c
    val = _bias_lookup(_t5_bucket(rel), rb_ref, h)
    o_ref[0, 0] = jnp.where(rel >= 0, val * LOG2E, -jnp.inf)


def _bias_tiles(rel_bias, t):
    return pl.pallas_call(
        functools.partial(_bias_tile_kernel, t=t),
        grid=(N_HEADS, 2),
        in_specs=[pl.BlockSpec(memory_space=pltpu.SMEM)],
        out_specs=pl.BlockSpec((1, 1, t, t), lambda h, d: (h, d, 0, 0)),
        out_shape=jax.ShapeDtypeStruct((N_HEADS, 2, t, t), F32),
        compiler_params=_params(2),
        name="bias_tiles",
    )(rel_bias)


def _bias_row_kernel(rb_ref, o_ref, *, past):
    width = o_ref.shape[1]
    c = lax.broadcasted_iota(jnp.int32, (1, width), 1)
    bucket = _t5_bucket(jnp.maximum(past - c, 0))
    for h in range(N_HEADS):
        o_ref[h:h + 1, :] = _bias_lookup(bucket, rb_ref, h) * LOG2E


def _bias_rows(rel_bias, past):
    return pl.pallas_call(
        functools.partial(_bias_row_kernel, past=past),
        in_specs=[pl.BlockSpec(memory_space=pltpu.SMEM)],
        out_shape=jax.ShapeDtypeStruct((N_HEADS, past + LANES), F32),
        name="bias_rows",
    )(rel_bias)


def _window_means(ext, u, pos):
    tm = u.shape[0]
    outs = []
    for gi, w in enumerate(POOL_WINDOWS):
        ch = slice(gi * POOL_GROUP_W, (gi + 1) * POOL_GROUP_W)
        s = ext[:, ch]
        span = 1
        while span < w:
            s = s[span:] + s[:-span]
            span *= 2
        off = POOL_HALO - (w - 1)
        win = s[off:off + tm]
        cnt = jnp.minimum(pos + 1, w).astype(F32)
        outs.append(win / cnt - u[:, ch])
    return jnp.concatenate(outs, axis=-1)


def _in_proj_kernel(x_ref, sh_ref, sc_ref, gmix_ref, w_ref, wkt_ref, qg_ref, kg_ref, bd_ref,
                    *rest, tm, seq_pool, n_prev):
    prev_k, prev_v = rest[:n_prev], rest[n_prev:2 * n_prev]
    q_ref, kt_ref, v_ref, u_ref, g_ref = rest[2 * n_prev:2 * n_prev + 5]
    rest = rest[2 * n_prev + 5:]
    for p in range(n_prev):
        kt_ref[p, 0] = prev_k[p][0, 0]
        v_ref[p, 0] = prev_v[p][0, 0]
    h = _rms(x_ref[0]) * gmix_ref[...]
    h = h * (1.0 + sc_ref[0]) + sh_ref[0]
    hb = h.astype(BF16)

    def proj(c0, c1):
        return _dot(hb, w_ref[:, c0:c1])

    zq = proj(0, QK_W)
    msq = _dot((zq * zq).astype(BF16), bd_ref[...])
    q_ref[0] = (zq * lax.rsqrt(msq + EPS) * qg_ref[...]).astype(BF16)

    zk = lax.dot_general(wkt_ref[...], hb, _NT, preferred_element_type=F32)
    zk = zk.reshape(N_HEADS * 2, HEAD_DIM, tm)
    msk = jnp.mean(zk * zk, axis=1, keepdims=True)
    kt_ref[n_prev, 0] = (zk * lax.rsqrt(msk + EPS) * kg_ref[...]).reshape(QK_W, tm)

    c0 = 2 * QK_W
    zv = proj(c0, c0 + ATTN_W)
    for hd in range(N_HEADS):
        v_ref[n_prev, 0, pl.ds(hd, tm, stride=N_HEADS), :] = zv[:, hd * V_DIM:(hd + 1) * V_DIM]
    c0 += ATTN_W
    u = proj(c0, c0 + POOL_W)
    u_ref[0] = u
    c0 += POOL_W
    for c in range(0, 2 * D_MODEL, 512):
        g_ref[0, :, c:c + 512] = jax.nn.sigmoid(proj(c0 + c, c0 + c + 512)).astype(BF16)

    if seq_pool:
        d_ref, carry_ref = rest
        i = pl.program_id(1)

        @pl.when(i == 0)
        def _():
            carry_ref[...] = jnp.zeros_like(carry_ref)

        ext = jnp.concatenate([carry_ref[...], u], axis=0)
        carry_ref[...] = u[tm - POOL_HALO:, :]
        pos = i * tm + lax.broadcasted_iota(jnp.int32, (tm, 1), 0)
        d_ref[0] = _window_means(ext, u, pos).astype(BF16)


def _in_proj(x, mod, gmix, w_bf, wkt_bf, qg, kg, bd, prev_k=(), prev_v=(), *, tm, seq_pool):
    B, T, _ = x.shape
    nt = T // tm
    n_prev = len(prev_k)
    k_block = lambda n: pl.BlockSpec((n, 1, QK_W, tm), lambda b, i: (0, b, 0, i))
    v_block = lambda n: pl.BlockSpec((n, 1, tm * N_HEADS, V_DIM), lambda b, i: (0, b, i, 0))
    out_shape = [jax.ShapeDtypeStruct((B, T, QK_W), BF16),
                 jax.ShapeDtypeStruct((n_prev + 1, B, QK_W, T), F32),
                 jax.ShapeDtypeStruct((n_prev + 1, B, T * N_HEADS, V_DIM), F32),
                 jax.ShapeDtypeStruct((B, T, POOL_W), F32),
                 jax.ShapeDtypeStruct((B, T, 2 * D_MODEL), BF16)]
    out_specs = [_row_spec(tm, QK_W), k_block(n_prev + 1), v_block(n_prev + 1),
                 _row_spec(tm, POOL_W), _row_spec(tm, 2 * D_MODEL)]
    scratch = []
    if seq_pool:
        out_shape.append(jax.ShapeDtypeStruct((B, T, POOL_W), BF16))
        out_specs.append(_row_spec(tm, POOL_W))
        scratch.append(pltpu.VMEM((POOL_HALO, POOL_W), F32))
    return pl.pallas_call(
        functools.partial(_in_proj_kernel, tm=tm, seq_pool=seq_pool, n_prev=n_prev),
        grid=(B, nt),
        in_specs=[_row_spec(tm, D_MODEL), _mod_spec(mod, tm, 0), _mod_spec(mod, tm, 1),
                  _const_spec(gmix), _const_spec(w_bf), _const_spec(wkt_bf),
                  _const_spec(qg), _const_spec(kg), _const_spec(bd)]
                 + [k_block(1)] * n_prev + [v_block(1)] * n_prev,
        out_specs=out_specs,
        out_shape=out_shape,
        scratch_shapes=scratch,
        compiler_params=_params(2),
        name="in_proj",
    )(x, mod, mod, gmix, w_bf, wkt_bf, qg, kg, bd, *prev_k, *prev_v)


def _diff_lambda(lp, layer):
    a = jnp.sum(lp[0:1] * lp[1:2], axis=-1, keepdims=True)
    b = jnp.sum(lp[2:3] * lp[3:4], axis=-1, keepdims=True)
    return jnp.exp(a) - jnp.exp(b) + _lambda_init(layer)


def _attn_kernel(qi_ref, kj_ref, q_ref, kt_ref, v_ref, bias_ref, rb_ref, lp_ref, subln_ref,
                 o_ref, m_ref, l_ref, acc_ref, s_ref, p_ref, alpha_ref, *, t, layer):
    s_idx = pl.program_id(1)
    i = qi_ref[s_idx]
    j = kj_ref[s_idx]

    @pl.when(j == 0)
    def _():
        m_ref[...] = jnp.full_like(m_ref, -jnp.inf)
        l_ref[...] = jnp.zeros_like(l_ref)
        acc_ref[...] = jnp.zeros_like(acc_ref)

    lane = lax.broadcasted_iota(jnp.int32, (1, V_DIM), 1)
    map0 = jnp.where(lane < HEAD_DIM, 1.0, 0.0).astype(BF16)
    map1 = jnp.where(lane >= HEAD_DIM, 1.0, 0.0).astype(BF16)
    far = (i - j) >= 2
    for h in range(N_HEADS):
        hs = slice(h * V_DIM, (h + 1) * V_DIM)
        q = q_ref[0, :, hs]
        kb = kt_ref[0, 0, hs, :].astype(BF16)
        vb = jnp.concatenate([v_ref[0, 0, pl.ds(h, t, stride=N_HEADS), :].astype(BF16),
                              jnp.ones((t, V_DIM), BF16)], axis=1)
        far_bias = rb_ref[N_BUCKETS - 1, h] * LOG2E
        for c in range(2 * t // ATTN_ROWS):
            qrow = (c * ATTN_ROWS) % t
            qc = q[qrow:qrow + ATTN_ROWS] * (map0 if c * ATTN_ROWS < t else map1)
            s_ref[...] = _dot(qc, kb)
            row0 = c * ATTN_ROWS

            def row_max(k, carry, h=h, qrow=qrow, row0=row0, far_bias=far_bias):
                r0 = pl.multiple_of(k * ATTN_SUB, ATTN_SUB)
                rows = pl.ds(row0 + r0, ATTN_SUB)
                bias = jnp.where(far, far_bias, bias_ref[h, 0, pl.ds(qrow + r0, ATTN_SUB), :])
                s = s_ref[pl.ds(r0, ATTN_SUB), :] + bias
                s_ref[pl.ds(r0, ATTN_SUB), :] = s
                m_prev = m_ref[h, rows, :]
                m_new = jnp.maximum(m_prev, jnp.max(s, axis=-1, keepdims=True))
                alpha_ref[pl.ds(r0, ATTN_SUB), :] = jnp.broadcast_to(jnp.exp2(m_prev - m_new),
                                                                     (ATTN_SUB, V_DIM))
                m_ref[h, rows, :] = m_new
                return carry

            def row_exp(k, carry, h=h, row0=row0):
                r0 = pl.multiple_of(k * ATTN_SUB, ATTN_SUB)
                m = m_ref[h, pl.ds(row0 + r0, ATTN_SUB), :]
                p_ref[pl.ds(r0, ATTN_SUB), :] = jnp.exp2(s_ref[pl.ds(r0, ATTN_SUB), :] - m).astype(BF16)
                return carry

            lax.fori_loop(0, ATTN_ROWS // ATTN_SUB, row_max, 0, unroll=ATTN_UNROLL)
            lax.fori_loop(0, ATTN_ROWS // ATTN_SUB, row_exp, 0, unroll=ATTN_UNROLL)
            blk = slice(row0, row0 + ATTN_ROWS)
            pv = _dot(p_ref[...], vb)
            alpha = alpha_ref[...]
            acc_ref[h, blk, :] = alpha * acc_ref[h, blk, :] + pv[:, :V_DIM]
            l_ref[h, blk, :] = alpha * l_ref[h, blk, :] + pv[:, V_DIM:]

    @pl.when(j == i)
    def _():
        lam = _diff_lambda(lp_ref[...], layer)
        for h in range(N_HEADS):
            o2 = acc_ref[h] / l_ref[h]
            o = o2[:t] - lam * o2[t:]
            o = _rms(o) * subln_ref[...] * (1.0 - _lambda_init(layer))
            o_ref[0, :, h * V_DIM:(h + 1) * V_DIM] = o.astype(BF16)


def _attn_prompt(q, kt, v, bias_tiles, rel_bias, lp, subln, *, layer, t):
    B, T, _ = q.shape
    kv_layer = kt.shape[0] - 1
    n = T // t
    pairs = [(i, j) for i in range(n) for j in range(i + 1)]
    qi = jnp.asarray([p[0] for p in pairs], jnp.int32)
    kj = jnp.asarray([p[1] for p in pairs], jnp.int32)
    grid_spec = pltpu.PrefetchScalarGridSpec(
        num_scalar_prefetch=2,
        grid=(B, len(pairs)),
        in_specs=[pl.BlockSpec((1, t, QK_W), lambda b, s, qi, kj: (b, qi[s], 0)),
                  pl.BlockSpec((1, 1, QK_W, t), lambda b, s, qi, kj: (kv_layer, b, 0, kj[s])),
                  pl.BlockSpec((1, 1, t * N_HEADS, V_DIM), lambda b, s, qi, kj: (kv_layer, b, kj[s], 0)),
                  pl.BlockSpec((N_HEADS, 1, t, t),
                               lambda b, s, qi, kj: (0, jnp.minimum(qi[s] - kj[s], 1), 0, 0)),
                  pl.BlockSpec(memory_space=pltpu.SMEM),
                  pl.BlockSpec(lp.shape, lambda *_: (0, 0)),
                  pl.BlockSpec(subln.shape, lambda *_: (0, 0))],
        out_specs=pl.BlockSpec((1, t, ATTN_W), lambda b, s, qi, kj: (b, qi[s], 0)),
        scratch_shapes=[pltpu.VMEM((N_HEADS, 2 * t, 1), F32),
                        pltpu.VMEM((N_HEADS, 2 * t, V_DIM), F32),
                        pltpu.VMEM((N_HEADS, 2 * t, V_DIM), F32),
                        pltpu.VMEM((ATTN_ROWS, t), F32),
                        pltpu.VMEM((ATTN_ROWS, t), BF16),
                        pltpu.VMEM((ATTN_ROWS, V_DIM), F32)])
    return pl.pallas_call(
        functools.partial(_attn_kernel, t=t, layer=layer),
        grid_spec=grid_spec,
        out_shape=jax.ShapeDtypeStruct((B, T, ATTN_W), BF16),
        compiler_params=_params(2),
        name="attn_prompt",
    )(qi, kj, q, kt, v, bias_tiles, rel_bias, lp, subln)


def _attn_sample_kernel(pt_ref, q_ref, kn_ref, vn_ref, *refs, n_pages, page, layer):
    k_refs = refs[:n_pages]
    v_refs = refs[n_pages:2 * n_pages]
    bias_ref, lp_ref, subln_ref, o_ref = refs[2 * n_pages:]
    rows = 2 * N_HEADS
    row = lax.broadcasted_iota(jnp.int32, (rows, QK_W), 0)
    seg = lax.broadcasted_iota(jnp.int32, (rows, QK_W), 1) // HEAD_DIM
    q = jnp.broadcast_to(q_ref[0].astype(F32), (rows, QK_W))
    qbd_f = jnp.where(seg == (row % N_HEADS) * 2 + row // N_HEADS, q, 0.0)
    qbd = qbd_f.astype(BF16)

    s = jnp.concatenate([_dot(qbd, k_refs[p][0, 0].astype(BF16)) for p in range(n_pages)], axis=1)
    past = n_pages * page
    bias = bias_ref[...]
    bias = jnp.concatenate([bias, bias], axis=0)
    s = s + bias[:, :past]
    s_new = jnp.sum(qbd_f * kn_ref[0], axis=-1, keepdims=True) + bias[:, past:past + 1]
    m = jnp.maximum(jnp.max(s, axis=-1, keepdims=True), s_new)
    p = jnp.exp2(s - m)
    p_new = jnp.exp2(s_new - m)
    denom = jnp.sum(p, axis=-1, keepdims=True) + p_new
    p = p / denom
    p_new = p_new / denom

    lam = _diff_lambda(lp_ref[...], layer)
    first = lax.broadcasted_iota(jnp.int32, (rows, 1), 0) < N_HEADS
    a = jnp.where(first, p - lam * pltpu.roll(p, N_HEADS, 0), 0.0).astype(BF16)
    a_new = p_new[:N_HEADS] - lam * p_new[N_HEADS:]

    out = jnp.zeros((rows, ATTN_W), F32)
    for pg in range(n_pages):
        vp = jnp.concatenate([v_refs[pg][0, 0, pl.ds(h, page, stride=N_HEADS), :]
                              for h in range(N_HEADS)], axis=1)
        out = out + _dot(a[:, pg * page:(pg + 1) * page], vp.astype(BF16))
    out = out[:N_HEADS] + a_new * vn_ref[0]
    rowh = lax.broadcasted_iota(jnp.int32, (N_HEADS, ATTN_W), 0)
    head = lax.broadcasted_iota(jnp.int32, (N_HEADS, ATTN_W), 1) // V_DIM
    om = jnp.where(rowh == head, out, 0.0)
    ms = jnp.sum(om * om, axis=-1, keepdims=True) / V_DIM
    on = om * lax.rsqrt(ms + EPS)
    o = jnp.sum(on, axis=0, keepdims=True) * subln_ref[...] * (1.0 - _lambda_init(layer))
    o_ref[0] = o.astype(BF16)


def _attn_sample(q, k_new, v_new, cache_kt, cache_vr, page_table, bias_rows, lp, subln4, *, layer):
    DB = q.shape[0]
    n_pages = page_table.shape[1]
    page = cache_kt.shape[3]

    def page_spec(p):
        return pl.BlockSpec((1, 1, QK_W, page), lambda b, pt: (layer, pt[b, p], 0, 0))

    def vpage_spec(p):
        return pl.BlockSpec((1, 1, page * N_HEADS, V_DIM), lambda b, pt: (layer, pt[b, p], 0, 0))

    vec = pl.BlockSpec((1, 1, QK_W), lambda b, pt: (b, 0, 0))
    grid_spec = pltpu.PrefetchScalarGridSpec(
        num_scalar_prefetch=1,
        grid=(DB,),
        in_specs=([vec, vec, vec] + [page_spec(p) for p in range(n_pages)]
                  + [vpage_spec(p) for p in range(n_pages)]
                  + [pl.BlockSpec(bias_rows.shape, lambda *_: (0, 0)),
                     pl.BlockSpec(lp.shape, lambda *_: (0, 0)),
                     pl.BlockSpec(subln4.shape, lambda *_: (0, 0))]),
        out_specs=vec)
    return pl.pallas_call(
        functools.partial(_attn_sample_kernel, n_pages=n_pages, page=page, layer=layer),
        grid_spec=grid_spec,
        out_shape=jax.ShapeDtypeStruct((DB, 1, ATTN_W), BF16),
        compiler_params=_params(1),
        name="attn_sample",
    )(page_table, q, k_new, v_new, *([cache_kt] * n_pages), *([cache_vr] * n_pages),
      bias_rows, lp, subln4)


def _pool_sample_kernel(state_ref, u_ref, d_ref):
    u = u_ref[...]
    outs = []
    for gi, w in enumerate(POOL_WINDOWS):
        ch = slice(gi * POOL_GROUP_W, (gi + 1) * POOL_GROUP_W)
        s = u[:, ch]
        for r in range(POOL_BUF - (w - 1), POOL_BUF):
            s = s + state_ref[r, :, ch]
        outs.append(s / float(w) - u[:, ch])
    d_ref[...] = jnp.concatenate(outs, axis=-1).astype(BF16)


def _pool_sample(state_t, u):
    return pl.pallas_call(
        _pool_sample_kernel,
        out_shape=jax.ShapeDtypeStruct(u.shape, BF16),
        name="pool_sample",
    )(state_t, u)


def _mix_kernel(o_ref, d_ref, g_ref, x_ref, gt_ref, sh_ref, sc_ref, wp_ref, ps_ref, wb_ref, wo_ref,
                gffn_ref, *rest, router):
    d = d_ref[0]
    y = jnp.concatenate([_dot(d[:, g * POOL_GROUP_W:(g + 1) * POOL_GROUP_W], wp_ref[g])
                         for g in range(len(POOL_WINDOWS))], axis=-1)
    y = (y * ps_ref[...]).astype(BF16)
    pa = _dot(o_ref[0], wb_ref[:ATTN_W])
    pb = _dot(y, wb_ref[ATTN_W:])
    g = g_ref[0]
    merged = g[:, :D_MODEL].astype(F32) * pa + g[:, D_MODEL:].astype(F32) * pb
    xn = x_ref[0] + gt_ref[0] * _dot(merged.astype(BF16), wo_ref[...])
    h2 = _rms(xn) * gffn_ref[...]
    h2 = h2 * (1.0 + sc_ref[0]) + sh_ref[0]
    if not router:
        xn_ref, h2_ref = rest
        xn_ref[0] = xn
        h2_ref[0] = h2.astype(h2_ref.dtype)
        return

    wr_hi_ref, wr_lo_ref, xn_ref, h2_ref, ri_ref, rp_ref = rest
    xn_ref[0] = xn
    _slab_store(h2_ref, (0,), h2)
    hi = h2.astype(BF16)
    lo = (h2 - hi.astype(F32)).astype(BF16)
    logits = _dot(hi, wr_hi_ref[...]) + _dot(lo, wr_hi_ref[...]) + _dot(hi, wr_lo_ref[...])
    lane = lax.broadcasted_iota(jnp.int32, logits.shape, 1)
    lg = jnp.where(lane < N_EXPERTS, logits, -jnp.inf)
    m1 = jnp.max(lg, axis=-1, keepdims=True)
    i1 = jnp.min(jnp.where(lg == m1, lane, LANES), axis=-1, keepdims=True)
    lg = jnp.where(lane == i1, -jnp.inf, lg)
    m2 = jnp.max(lg, axis=-1, keepdims=True)
    i2 = jnp.min(jnp.where(lg == m2, lane, LANES), axis=-1, keepdims=True)
    e = jnp.exp(m2 - m1)
    ri_ref[0] = jnp.concatenate([i1, i2], axis=-1)
    rp_ref[0] = jnp.concatenate([1.0 / (1.0 + e), e / (1.0 + e)], axis=-1)


def _mix(o, d, g, x, mod, wp_bf, pscale, wb_bf, wo_bf, gffn, router_w, *, tm):
    B, T, _ = x.shape
    router = router_w is not None
    ins = [o, d, g, x, mod, mod, mod, wp_bf, pscale, wb_bf, wo_bf, gffn]
    in_specs = [_row_spec(tm, ATTN_W), _row_spec(tm, POOL_W), _row_spec(tm, 2 * D_MODEL),
                _row_spec(tm, D_MODEL), _mod_spec(mod, tm, 2), _mod_spec(mod, tm, 3),
                _mod_spec(mod, tm, 4)] + [_const_spec(a) for a in ins[7:]]
    out_shape = [jax.ShapeDtypeStruct((B, T, D_MODEL), F32), jax.ShapeDtypeStruct((B, T, D_MODEL), BF16)]
    out_specs = [_row_spec(tm, D_MODEL), _row_spec(tm, D_MODEL)]
    if router:
        out_shape[1] = jax.ShapeDtypeStruct((B, T * ROW_SLAB, LANES), F32)
        out_specs[1] = _row_spec(tm * ROW_SLAB, LANES)
        ins += list(router_w)
        in_specs += [_const_spec(a) for a in router_w]
        out_shape += [jax.ShapeDtypeStruct((B, T, TOP_K), jnp.int32),
                      jax.ShapeDtypeStruct((B, T, TOP_K), F32)]
        out_specs += [_row_spec(tm, TOP_K), _row_spec(tm, TOP_K)]
    return pl.pallas_call(
        functools.partial(_mix_kernel, router=router),
        grid=(B, T // tm),
        in_specs=in_specs,
        out_specs=out_specs,
        out_shape=out_shape,
        compiler_params=_params(2),
        name="mix",
    )(*ins)


def _ffn_kernel(h_ref, x_ref, gt_ref, wg_ref, wu_ref, wd_ref, y_ref, acc_ref):
    f = pl.program_id(2)

    @pl.when(f == 0)
    def _():
        acc_ref[...] = jnp.zeros_like(acc_ref)

    h = h_ref[0]
    a = (_silu(_dot(h, wg_ref[...])) * _dot(h, wu_ref[...])).astype(BF16)
    acc_ref[...] += _dot(a, wd_ref[...])

    @pl.when(f == pl.num_programs(2) - 1)
    def _():
        y_ref[0] = x_ref[0] + gt_ref[0] * acc_ref[...]


def _ffn_dense(h2, x, mod, wg_bf, wu_bf, wd_bf, *, tm):
    B, T, _ = x.shape
    d_ff = wg_bf.shape[1]
    tf = d_ff // 2
    return pl.pallas_call(
        _ffn_kernel,
        grid=(B, T // tm, d_ff // tf),
        in_specs=[_row_spec(tm, D_MODEL), _row_spec(tm, D_MODEL), _mod_spec(mod, tm, 5),
                  pl.BlockSpec((D_MODEL, tf), lambda b, i, f: (0, f)),
                  pl.BlockSpec((D_MODEL, tf), lambda b, i, f: (0, f)),
                  pl.BlockSpec((tf, D_MODEL), lambda b, i, f: (f, 0))],
        out_specs=_row_spec(tm, D_MODEL),
        out_shape=jax.ShapeDtypeStruct((B, T, D_MODEL), F32),
        scratch_shapes=[pltpu.VMEM((tm, D_MODEL), F32)],
        compiler_params=_params(3),
        name="ffn_dense",
    )(h2, x, mod, wg_bf, wu_bf, wd_bf)


ROW_SLAB = D_MODEL // LANES


def _slab_copy(src_hbm, src_row, dst, dst_row, sem):
    src = src_hbm.at[pl.ds(pl.multiple_of(src_row * ROW_SLAB, ROW_SLAB), ROW_SLAB), :]
    return pltpu.make_async_copy(src, dst.at[pl.ds(pl.multiple_of(dst_row * ROW_SLAB, ROW_SLAB), ROW_SLAB), :], sem)


def _slab_store(ref, idx, x):
    rows = x.shape[0]
    for s in range(ROW_SLAB):
        ref[idx + (pl.ds(s, rows, stride=ROW_SLAB), slice(None))] = x[:, s * LANES:(s + 1) * LANES]


def _slab_load(ref, idx, rows, start=0):
    return jnp.concatenate([ref[idx + (pl.ds(start + s, rows, stride=ROW_SLAB), slice(None))]
                            for s in range(ROW_SLAB)], axis=1)


def _moe_ffn_kernel(te_ref, nu_ref, src_ref, src_next_ref, h_hbm, wg_ref, wu_ref, wd_ref, y_ref,
                    xbuf, xb_ref, acc_ref, sem, *, tm):
    i = pl.program_id(0)
    f = pl.program_id(1)
    nu = nu_ref[0]
    slot = i % 2

    def issue(idx_ref, dst):
        def body(r, carry):
            _slab_copy(h_hbm, idx_ref[0, 0, r], xbuf.at[dst], r, sem.at[dst]).start()
            return carry
        lax.fori_loop(0, tm, body, 0, unroll=8)

    def drain(dst):
        def body(r, carry):
            _slab_copy(h_hbm, 0, xbuf.at[dst], 0, sem.at[dst]).wait()
            return carry
        lax.fori_loop(0, tm, body, 0, unroll=8)

    @pl.when(i < nu)
    def _():
        @pl.when(f == 0)
        def _():
            @pl.when(i == 0)
            def _():
                issue(src_ref, 0)

            drain(slot)

            @pl.when(i + 1 < nu)
            def _():
                issue(src_next_ref, 1 - slot)

            acc_ref[...] = jnp.zeros_like(acc_ref)
            xb_ref[...] = _slab_load(xbuf, (slot,), tm).astype(BF16)

        xb = xb_ref[...]
        a = (_silu(_dot(xb, wg_ref[0].astype(BF16))) * _dot(xb, wu_ref[0].astype(BF16))).astype(BF16)
        acc_ref[...] += _dot(a, wd_ref[0].astype(BF16))

        @pl.when(f == pl.num_programs(1) - 1)
        def _():
            _slab_store(y_ref, (), acc_ref[...])

    @pl.when(i >= nu)
    def _():
        y_ref[...] = jnp.zeros_like(y_ref)


def _moe_ffn(h_rows, src, tile_expert, n_used, wg, wu, wd, *, tm, tf):
    n_tiles = src.shape[0]
    d_ff = wg.shape[2]
    assert d_ff % tf == 0, (d_ff, tf)
    nf = d_ff // tf

    def tile(i, nu):
        return jnp.minimum(i, nu[0] - 1)

    def ff(i, f, nu):
        return jnp.where(i < nu[0], f, nf - 1)

    grid_spec = pltpu.PrefetchScalarGridSpec(
        num_scalar_prefetch=2,
        grid=(n_tiles, nf),
        in_specs=[pl.BlockSpec((1, 1, tm), lambda i, f, te, nu: (i, 0, 0), memory_space=pltpu.SMEM),
                  pl.BlockSpec((1, 1, tm), lambda i, f, te, nu: (jnp.minimum(i + 1, n_tiles - 1), 0, 0),
                               memory_space=pltpu.SMEM),
                  pl.BlockSpec(memory_space=pl.ANY),
                  pl.BlockSpec((1, D_MODEL, tf), lambda i, f, te, nu: (te[tile(i, nu)], 0, ff(i, f, nu))),
                  pl.BlockSpec((1, D_MODEL, tf), lambda i, f, te, nu: (te[tile(i, nu)], 0, ff(i, f, nu))),
                  pl.BlockSpec((1, tf, D_MODEL), lambda i, f, te, nu: (te[tile(i, nu)], ff(i, f, nu), 0))],
        out_specs=pl.BlockSpec((tm * ROW_SLAB, LANES), lambda i, f, te, nu: (i, 0)),
        scratch_shapes=[pltpu.VMEM((2, tm * ROW_SLAB, LANES), F32), pltpu.VMEM((tm, D_MODEL), BF16),
                        pltpu.VMEM((tm, D_MODEL), F32),
                        pltpu.SemaphoreType.DMA((2,))])
    return pl.pallas_call(
        functools.partial(_moe_ffn_kernel, tm=tm),
        grid_spec=grid_spec,
        out_shape=jax.ShapeDtypeStruct((n_tiles * tm * ROW_SLAB, LANES), F32),
        compiler_params=_params(2),
        name="moe_ffn",
    )(tile_expert, n_used, src, src, h_rows, wg, wu, wd)


def _combine_kernel(pos_ref, pos_next_ref, y_hbm, x_ref, gt_ref, p_ref, o_ref, buf, sem, *, tc):
    g = pl.program_id(0) * pl.num_programs(1) + pl.program_id(1)
    n_steps = pl.num_programs(0) * pl.num_programs(1)
    slot = g % 2

    def issue(idx_ref, dst):
        def body(t, carry):
            for k in range(TOP_K):
                _slab_copy(y_hbm, idx_ref[0, 0, TOP_K * t + k], buf.at[dst, k], t, sem.at[dst]).start()
            return carry
        lax.fori_loop(0, tc, body, 0, unroll=4)

    @pl.when(g == 0)
    def _():
        issue(pos_ref, 0)

    def drain(t, carry):
        for k in range(TOP_K):
            _slab_copy(y_hbm, 0, buf.at[slot, 0], 0, sem.at[slot]).wait()
        return carry

    lax.fori_loop(0, tc, drain, 0, unroll=4)

    @pl.when(g + 1 < n_steps)
    def _():
        issue(pos_next_ref, 1 - slot)

    p = p_ref[0]
    f = p[:, 0:1] * _slab_load(buf, (slot, 0), tc) + p[:, 1:2] * _slab_load(buf, (slot, 1), tc)
    o_ref[0] = x_ref[0] + gt_ref[0] * f


def _combine(y, pos, probs, x, mod, *, tc):
    B, T, _ = x.shape
    nt = T // tc
    last = B * nt - 1
    pos_chunks = pos.reshape(B * nt, 1, TOP_K * tc)
    return pl.pallas_call(
        functools.partial(_combine_kernel, tc=tc),
        grid=(B, nt),
        in_specs=[pl.BlockSpec((1, 1, TOP_K * tc), lambda b, i: (b * nt + i, 0, 0),
                               memory_space=pltpu.SMEM),
                  pl.BlockSpec((1, 1, TOP_K * tc), lambda b, i: (jnp.minimum(b * nt + i + 1, last), 0, 0),
                               memory_space=pltpu.SMEM),
                  pl.BlockSpec(memory_space=pl.ANY),
                  _row_spec(tc, D_MODEL), _mod_spec(mod, tc, 5), _row_spec(tc, TOP_K)],
        out_specs=_row_spec(tc, D_MODEL),
        out_shape=jax.ShapeDtypeStruct(x.shape, F32),
        scratch_shapes=[pltpu.VMEM((2, TOP_K, tc * ROW_SLAB, LANES), F32), pltpu.SemaphoreType.DMA((2,))],
        compiler_params=_params(2),
        name="moe_combine",
    )(pos_chunks, pos_chunks, y, x, mod, probs)


def _route_plan(top_i, tm, n_tiles):
    e_flat = top_i.reshape(-1)
    onehot = (e_flat[:, None] == jnp.arange(N_EXPERTS, dtype=jnp.int32)[None, :]).astype(jnp.int32)
    csum = jnp.cumsum(onehot, axis=0)
    cnt = csum[-1]
    gsz = (cnt + tm - 1) // tm * tm
    gend = jnp.cumsum(gsz)
    gstart = gend - gsz
    pos = jnp.sum(onehot * (gstart[None, :] + csum - 1), axis=1).astype(jnp.int32)
    n_used = (gend[-1] // tm).astype(jnp.int32).reshape(1)
    tile_start = jnp.arange(n_tiles, dtype=jnp.int32) * tm
    tile_expert = jnp.sum((tile_start[:, None] >= gend[None, :]).astype(jnp.int32), axis=1)
    tile_expert = jnp.minimum(tile_expert, N_EXPERTS - 1).astype(jnp.int32)
    token = jnp.arange(e_flat.shape[0], dtype=jnp.int32) // TOP_K
    src = jnp.zeros((n_tiles * tm,), jnp.int32).at[pos].set(token, unique_indices=True)
    return pos.reshape(top_i.shape), src.reshape(n_tiles, 1, tm), tile_expert, n_used


def _moe(groups, wg, wu, wd):
    tm = MOE_ROW_TILE
    counts = [g[1].shape[0] * g[1].shape[1] for g in groups]
    total = sum(counts) * TOP_K
    n_tiles = (total + N_EXPERTS * (tm - 1)) // tm + 1
    top_all = jnp.concatenate([g[3].reshape(-1, TOP_K) for g in groups], axis=0)
    h_all = jnp.concatenate([g[0].reshape(-1, LANES) for g in groups], axis=0)
    pos_all, src, tile_expert, n_used = _route_plan(top_all, tm, n_tiles)
    y = _moe_ffn(h_all, src, tile_expert, n_used, wg, wu, wd, tm=tm, tf=MOE_FF_TILE)
    outs = []
    offs = 0
    for (h2, x, mod, top_i, probs), cnt in zip(groups, counts):
        pos = pos_all[offs:offs + cnt].reshape(top_i.shape)
        offs += cnt
        outs.append(_combine(y, pos, probs, x, mod, tc=min(MOE_DMA_CHUNK, x.shape[1])))
    return outs


def kernel(x_prompt, x_sample, c_prompt, c_sample, cache_k, cache_v, state_pool, page_table,
           rel_bias, w_ada, b_ada, g_mix, g_ffn, w_in, q_norm, k_norm, diff_lambda, subln,
           w_pool, pool_scale, w_branch, w_out, w_ff_gate, w_ff_up, w_ff_down, w_router,
           w_exp_gate, w_exp_up, w_exp_down):
    B, S, D = x_prompt.shape
    DB = x_sample.shape[0]
    depth = w_in.shape[0]
    n_phys, page = cache_k.shape[1], cache_k.shape[2]
    past = page_table.shape[1] * page

    mod_all = _ada_mod(jnp.concatenate([c_prompt, c_sample], axis=0), w_ada, b_ada)
    bias_tiles = _bias_tiles(rel_bias, ATTN_TILE)
    bias_rows = _bias_rows(rel_bias, past)

    cache_kt = jnp.transpose(cache_k, (0, 1, 3, 4, 5, 2)).reshape(depth, n_phys, QK_W, page)
    cache_vr = cache_v.reshape(depth, n_phys, page * N_HEADS, V_DIM)
    state_t = jnp.transpose(state_pool, (0, 2, 1, 3))

    bd = jnp.kron(jnp.eye(QK_W // HEAD_DIM, dtype=F32),
                  jnp.full((HEAD_DIM, HEAD_DIM), 1.0 / HEAD_DIM, F32)).astype(BF16)

    xp = x_prompt
    xs = x_sample.reshape(1, DB, D)
    outs = {k: [] for k in ("pp", "ks", "vs", "ps")}
    prev_kv = ((), ())
    for l in range(depth):
        mod_p = mod_all[l, :B].reshape(B, 1, 6 * D)
        mod_s = mod_all[l, B:].reshape(1, DB, 6 * D)
        w_bf = w_in[l].astype(BF16)
        wkt_bf = w_in[l][:, QK_W:2 * QK_W].T.astype(BF16)
        qg = jnp.tile(q_norm[l].reshape(1, 2 * HEAD_DIM), (1, N_HEADS)) * (HEAD_DIM ** -0.5 * LOG2E)
        kg = jnp.tile(k_norm[l], (N_HEADS, 1)).reshape(N_HEADS * 2, HEAD_DIM, 1)
        gmix = g_mix[l].reshape(1, D)
        gffn = g_ffn[l].reshape(1, D)
        lp = diff_lambda[l]
        sub1 = subln[l].reshape(1, V_DIM)
        sub4 = jnp.tile(sub1, (1, N_HEADS))
        wp_bf = w_pool[l].astype(BF16)
        pscale = pool_scale[l].reshape(1, POOL_W)
        wb_bf = w_branch[l].astype(BF16)
        wo_bf = w_out[l].astype(BF16)
        moe = l % 2 == 1
        j = l // 2
        router_w = None
        if moe:
            wr = jnp.pad(w_router[j], ((0, 0), (0, LANES - N_EXPERTS)))
            wr_hi = wr.astype(BF16)
            router_w = (wr_hi, (wr - wr_hi.astype(F32)).astype(BF16))

        stack = prev_kv if l == depth - 1 else ((), ())
        qp, ktp, vp, up, gp, dp = _in_proj(xp, mod_p, gmix, w_bf, wkt_bf, qg, kg, bd, *stack,
                                           tm=ROW_TILE, seq_pool=True)
        prev_kv = (prev_kv[0] + (ktp,), prev_kv[1] + (vp,))
        op = _attn_prompt(qp, ktp, vp, bias_tiles, rel_bias, lp, sub1, layer=l, t=ATTN_TILE)
        mix_p = _mix(op, dp, gp, xp, mod_p, wp_bf, pscale, wb_bf, wo_bf, gffn, router_w, tm=ROW_TILE)

        qs, kts, vs, us, gs = _in_proj(xs, mod_s, gmix, w_bf, wkt_bf, qg, kg, bd,
                                       tm=DB, seq_pool=False)
        ks_rows = jnp.transpose(kts[0], (0, 2, 1)).reshape(DB, 1, QK_W)
        osamp = _attn_sample(qs.reshape(DB, 1, QK_W), ks_rows, vs.reshape(DB, 1, ATTN_W),
                             cache_kt, cache_vr, page_table, bias_rows, lp, sub4, layer=l)
        ds = _pool_sample(state_t[l], us[0])
        mix_s = _mix(osamp.reshape(1, DB, ATTN_W), ds.reshape(1, DB, POOL_W), gs, xs, mod_s,
                     wp_bf, pscale, wb_bf, wo_bf, gffn, router_w, tm=DB)

        if moe:
            xp, xs = _moe([(mix_p[1], mix_p[0], mod_p, mix_p[2], mix_p[3]),
                           (mix_s[1], mix_s[0], mod_s, mix_s[2], mix_s[3])],
                          w_exp_gate[j], w_exp_up[j], w_exp_down[j])
        else:
            wg_bf = w_ff_gate[j].astype(BF16)
            wu_bf = w_ff_up[j].astype(BF16)
            wd_bf = w_ff_down[j].astype(BF16)
            xp = _ffn_dense(mix_p[1], mix_p[0], mod_p, wg_bf, wu_bf, wd_bf, tm=FFN_ROW_TILE)
            xs = _ffn_dense(mix_s[1], mix_s[0], mod_s, wg_bf, wu_bf, wd_bf, tm=DB)

        outs["pp"].append(up[:, S - POOL_BUF:, :])
        outs["ks"].append(ks_rows.reshape(DB, 1, N_HEADS, 2, HEAD_DIM))
        outs["vs"].append(vs.reshape(DB, 1, N_HEADS, V_DIM))
        outs["ps"].append(jnp.concatenate([state_pool[l][:, 1:], us.reshape(DB, 1, POOL_W)], axis=1))

    return (xp, xs.reshape(DB, 1, D),
            jnp.transpose(ktp.reshape(depth, B, N_HEADS, 2, HEAD_DIM, S), (0, 1, 5, 2, 3, 4)),
            vp.reshape(depth, B, S, N_HEADS, V_DIM), jnp.stack(outs["pp"]),
            jnp.stack(outs["ks"]), jnp.stack(outs["vs"]), jnp.stack(outs["ps"]))
```

```python
import functools
import math

import jax
import jax.numpy as jnp
from jax import lax
from jax.experimental import pallas as pl
from jax.experimental.pallas import tpu as pltpu

F32 = jnp.float32
BF16 = jnp.bfloat16

D_MODEL = 1024
N_HEADS = 4
HEAD_DIM = 64
V_DIM = 2 * HEAD_DIM
QK_W = N_HEADS * 2 * HEAD_DIM
ATTN_W = N_HEADS * V_DIM
POOL_WINDOWS = (2, 4, 8, 16)
POOL_W = D_MODEL // 2
POOL_GROUP_W = POOL_W // len(POOL_WINDOWS)
POOL_BUF = max(POOL_WINDOWS) - 1
POOL_HALO = max(POOL_WINDOWS)
IN_COLS = 2 * QK_W + ATTN_W + POOL_W + 2 * D_MODEL
N_BUCKETS = 32
MAX_DISTANCE = 128
N_EXPERTS = 8
TOP_K = 2
EPS = 1e-6
LOG2E = math.log2(math.e)
LANES = 128
V7X_VMEM_BYTES = 64 * 1024 * 1024
VMEM_LIMIT = V7X_VMEM_BYTES * 7 // 8

ATTN_TILE = 512
ATTN_ROWS = 256
ATTN_SUB = 32
ATTN_UNROLL = ATTN_ROWS // ATTN_SUB
SAMPLES_PER_STEP = 2
ROW_TILE = 512
FFN_ROW_TILE = 1024
MOE_ROW_TILE = 1024
MOE_FF_TILE = 512
MOE_DMA_CHUNK = 512

_NT = (((1,), (1,)), ((), ()))


def _lambda_init(layer):
    return 0.8 - 0.6 * math.exp(-0.3 * layer)


def _params(n_axes, vmem=VMEM_LIMIT):
    return pltpu.CompilerParams(dimension_semantics=("arbitrary",) * n_axes,
                                vmem_limit_bytes=vmem)


def _dot(a, b):
    return jnp.dot(a, b, preferred_element_type=F32)


def _rms(x):
    return x * lax.rsqrt(jnp.mean(x * x, axis=-1, keepdims=True) + EPS)


def _silu(x):
    return x * jax.nn.sigmoid(x)


def _const_spec(a):
    nd = a.ndim
    return pl.BlockSpec(a.shape, lambda *_: (0,) * nd)


def _row_spec(tm, width):
    return pl.BlockSpec((1, tm, width), lambda b, i, *_: (b, i, 0))


def _mod_spec(mod, tm, chunk):
    if mod.shape[1] == 1:
        return pl.BlockSpec((1, 1, D_MODEL), lambda b, i, *_: (b, 0, chunk))
    return pl.BlockSpec((1, tm, D_MODEL), lambda b, i, *_: (b, i, chunk))


def _ada_kernel(c_ref, w_ref, b_ref, o_ref):
    a = _silu(c_ref[...]).astype(BF16)
    o_ref[0] = _dot(a, w_ref[0].astype(BF16)) + b_ref[0]


def _ada_mod(c_all, w_ada, b_ada):
    rows = c_all.shape[0]
    depth, _, cols = w_ada.shape
    tn = cols // 4
    return pl.pallas_call(
        _ada_kernel,
        grid=(depth, cols // tn),
        in_specs=[pl.BlockSpec((rows, D_MODEL), lambda l, n: (0, 0)),
                  pl.BlockSpec((1, D_MODEL, tn), lambda l, n: (l, 0, n)),
                  pl.BlockSpec((1, 1, tn), lambda l, n: (l, 0, n))],
        out_specs=pl.BlockSpec((1, rows, tn), lambda l, n: (l, 0, n)),
        out_shape=jax.ShapeDtypeStruct((depth, rows, cols), F32),
        compiler_params=_params(2),
        name="ada_mod",
    )(c_all, w_ada, b_ada.reshape(depth, 1, cols))


def _t5_bucket(rel):
    n = jnp.maximum(rel, 0)
    max_exact = N_BUCKETS // 2
    nf = jnp.maximum(n, 1).astype(F32)
    large = max_exact + (jnp.log(nf / max_exact) / math.log(MAX_DISTANCE / max_exact)
                         * (N_BUCKETS - max_exact)).astype(jnp.int32)
    large = jnp.minimum(large, N_BUCKETS - 1)
    return jnp.where(n < max_exact, n, large)


def _bias_lookup(bucket, rb_ref, h):
    acc = jnp.zeros(bucket.shape, F32)
    for b in range(N_BUCKETS):
        acc = jnp.where(bucket == b, rb_ref[b, h], acc)
    return acc


def _bias_tile_kernel(rb_ref, o_ref, *, t):
    h = pl.program_id(0)
    d = pl.program_id(1)
    r = lax.broadcasted_iota(jnp.int32, (t, t), 0)
    c = lax.broadcasted_iota(jnp.int32, (t, t), 1)
    rel = d * t + r - c
    val = _bias_lookup(_t5_bucket(rel), rb_ref, h)
    o_ref[0, 0] = jnp.where(rel >= 0, val * LOG2E, -jnp.inf)


def _bias_tiles(rel_bias, t):
    return pl.pallas_call(
        functools.partial(_bias_tile_kernel, t=t),
        grid=(N_HEADS, 2),
        in_specs=[pl.BlockSpec(memory_space=pltpu.SMEM)],
        out_specs=pl.BlockSpec((1, 1, t, t), lambda h, d: (h, d, 0, 0)),
        out_shape=jax.ShapeDtypeStruct((N_HEADS, 2, t, t), F32),
        compiler_params=_params(2),
        name="bias_tiles",
    )(rel_bias)


def _bias_row_kernel(rb_ref, o_ref, *, past):
    width = o_ref.shape[1]
    c = lax.broadcasted_iota(jnp.int32, (1, width), 1)
    bucket = _t5_bucket(jnp.maximum(past - c, 0))
    for h in range(N_HEADS):
        o_ref[h:h + 1, :] = _bias_lookup(bucket, rb_ref, h) * LOG2E


def _bias_rows(rel_bias, past):
    return pl.pallas_call(
        functools.partial(_bias_row_kernel, past=past),
        in_specs=[pl.BlockSpec(memory_space=pltpu.SMEM)],
        out_shape=jax.ShapeDtypeStruct((N_HEADS, past + LANES), F32),
        name="bias_rows",
    )(rel_bias)


def _window_means(ext, u, pos):
    tm = u.shape[0]
    outs = []
    for gi, w in enumerate(POOL_WINDOWS):
        ch = slice(gi * POOL_GROUP_W, (gi + 1) * POOL_GROUP_W)
        s = ext[:, ch]
        span = 1
        while span < w:
            s = s[span:] + s[:-span]
            span *= 2
        off = POOL_HALO - (w - 1)
        win = s[off:off + tm]
        cnt = jnp.minimum(pos + 1, w).astype(F32)
        outs.append(win / cnt - u[:, ch])
    return jnp.concatenate(outs, axis=-1)


def _in_proj_kernel(x_ref, sh_ref, sc_ref, gmix_ref, w_ref, wkt_ref, qg_ref, kg_ref, bd_ref,
                    *rest, tm, seq_pool, n_prev):
    prev_k, prev_v = rest[:n_prev], rest[n_prev:2 * n_prev]
    q_ref, kt_ref, v_ref, u_ref, g_ref = rest[2 * n_prev:2 * n_prev + 5]
    rest = rest[2 * n_prev + 5:]
    for p in range(n_prev):
        kt_ref[p, 0] = prev_k[p][0, 0]
        v_ref[p, 0] = prev_v[p][0, 0]
    h = _rms(x_ref[0]) * gmix_ref[...]
    h = h * (1.0 + sc_ref[0]) + sh_ref[0]
    hb = h.astype(BF16)

    def proj(c0, c1):
        return _dot(hb, w_ref[:, c0:c1])

    zq = proj(0, QK_W)
    msq = _dot((zq * zq).astype(BF16), bd_ref[...])
    q_ref[0] = (zq * lax.rsqrt(msq + EPS) * qg_ref[...]).astype(BF16)

    zk = lax.dot_general(wkt_ref[...], hb, _NT, preferred_element_type=F32)
    zk = zk.reshape(N_HEADS * 2, HEAD_DIM, tm)
    msk = jnp.mean(zk * zk, axis=1, keepdims=True)
    kt_ref[n_prev, 0] = (zk * lax.rsqrt(msk + EPS) * kg_ref[...]).reshape(QK_W, tm)

    c0 = 2 * QK_W
    zv = proj(c0, c0 + ATTN_W)
    for hd in range(N_HEADS):
        v_ref[n_prev, 0, pl.ds(hd, tm, stride=N_HEADS), :] = zv[:, hd * V_DIM:(hd + 1) * V_DIM]
    c0 += ATTN_W
    u = proj(c0, c0 + POOL_W)
    u_ref[0] = u
    c0 += POOL_W
    for c in range(0, 2 * D_MODEL, 512):
        g_ref[0, :, c:c + 512] = jax.nn.sigmoid(proj(c0 + c, c0 + c + 512)).astype(BF16)

    if seq_pool:
        d_ref, carry_ref = rest
        i = pl.program_id(1)

        @pl.when(i == 0)
        def _():
            carry_ref[...] = jnp.zeros_like(carry_ref)

        ext = jnp.concatenate([carry_ref[...], u], axis=0)
        carry_ref[...] = u[tm - POOL_HALO:, :]
        pos = i * tm + lax.broadcasted_iota(jnp.int32, (tm, 1), 0)
        d_ref[0] = _window_means(ext, u, pos).astype(BF16)


def _in_proj(x, mod, gmix, w_bf, wkt_bf, qg, kg, bd, prev_k=(), prev_v=(), *, tm, seq_pool):
    B, T, _ = x.shape
    nt = T // tm
    n_prev = len(prev_k)
    k_block = lambda n: pl.BlockSpec((n, 1, QK_W, tm), lambda b, i: (0, b, 0, i))
    v_block = lambda n: pl.BlockSpec((n, 1, tm * N_HEADS, V_DIM), lambda b, i: (0, b, i, 0))
    out_shape = [jax.ShapeDtypeStruct((B, T, QK_W), BF16),
                 jax.ShapeDtypeStruct((n_prev + 1, B, QK_W, T), F32),
                 jax.ShapeDtypeStruct((n_prev + 1, B, T * N_HEADS, V_DIM), F32),
                 jax.ShapeDtypeStruct((B, T, POOL_W), F32),
                 jax.ShapeDtypeStruct((B, T, 2 * D_MODEL), BF16)]
    out_specs = [_row_spec(tm, QK_W), k_block(n_prev + 1), v_block(n_prev + 1),
                 _row_spec(tm, POOL_W), _row_spec(tm, 2 * D_MODEL)]
    scratch = []
    if seq_pool:
        out_shape.append(jax.ShapeDtypeStruct((B, T, POOL_W), BF16))
        out_specs.append(_row_spec(tm, POOL_W))
        scratch.append(pltpu.VMEM((POOL_HALO, POOL_W), F32))
    return pl.pallas_call(
        functools.partial(_in_proj_kernel, tm=tm, seq_pool=seq_pool, n_prev=n_prev),
        grid=(B, nt),
        in_specs=[_row_spec(tm, D_MODEL), _mod_spec(mod, tm, 0), _mod_spec(mod, tm, 1),
                  _const_spec(gmix), _const_spec(w_bf), _const_spec(wkt_bf),
                  _const_spec(qg), _const_spec(kg), _const_spec(bd)]
                 + [k_block(1)] * n_prev + [v_block(1)] * n_prev,
        out_specs=out_specs,
        out_shape=out_shape,
        scratch_shapes=scratch,
        compiler_params=_params(2),
        name="in_proj",
    )(x, mod, mod, gmix, w_bf, wkt_bf, qg, kg, bd, *prev_k, *prev_v)


def _diff_lambda(lp, layer):
    a = jnp.sum(lp[0:1] * lp[1:2], axis=-1, keepdims=True)
    b = jnp.sum(lp[2:3] * lp[3:4], axis=-1, keepdims=True)
    return jnp.exp(a) - jnp.exp(b) + _lambda_init(layer)


def _attn_kernel(qi_ref, kj_ref, q_ref, kt_ref, v_ref, bias_ref, rb_ref, lp_ref, subln_ref,
                 o_ref, m_ref, l_ref, acc_ref, s_ref, p_ref, alpha_ref, *, t, layer):
    s_idx = pl.program_id(1)
    i = qi_ref[s_idx]
    j = kj_ref[s_idx]

    @pl.when(j == 0)
    def _():
        m_ref[...] = jnp.full_like(m_ref, -jnp.inf)
        l_ref[...] = jnp.zeros_like(l_ref)
        acc_ref[...] = jnp.zeros_like(acc_ref)

    lane = lax.broadcasted_iota(jnp.int32, (1, V_DIM), 1)
    map0 = jnp.where(lane < HEAD_DIM, 1.0, 0.0).astype(BF16)
    map1 = jnp.where(lane >= HEAD_DIM, 1.0, 0.0).astype(BF16)
    far = (i - j) >= 2
    for h in range(N_HEADS):
        hs = slice(h * V_DIM, (h + 1) * V_DIM)
        q = q_ref[0, :, hs]
        kb = kt_ref[0, 0, hs, :].astype(BF16)
        vb = jnp.concatenate([v_ref[0, 0, pl.ds(h, t, stride=N_HEADS), :].astype(BF16),
                              jnp.ones((t, V_DIM), BF16)], axis=1)
        far_bias = rb_ref[N_BUCKETS - 1, h] * LOG2E
        for c in range(2 * t // ATTN_ROWS):
            qrow = (c * ATTN_ROWS) % t
            qc = q[qrow:qrow + ATTN_ROWS] * (map0 if c * ATTN_ROWS < t else map1)
            s_ref[...] = _dot(qc, kb)
            row0 = c * ATTN_ROWS

            def row_max(k, carry, h=h, qrow=qrow, row0=row0, far_bias=far_bias):
                r0 = pl.multiple_of(k * ATTN_SUB, ATTN_SUB)
                rows = pl.ds(row0 + r0, ATTN_SUB)
                bias = jnp.where(far, far_bias, bias_ref[h, 0, pl.ds(qrow + r0, ATTN_SUB), :])
                s = s_ref[pl.ds(r0, ATTN_SUB), :] + bias
                s_ref[pl.ds(r0, ATTN_SUB), :] = s
                m_prev = m_ref[h, rows, :]
                m_new = jnp.maximum(m_prev, jnp.max(s, axis=-1, keepdims=True))
                alpha_ref[pl.ds(r0, ATTN_SUB), :] = jnp.broadcast_to(jnp.exp2(m_prev - m_new),
                                                                     (ATTN_SUB, V_DIM))
                m_ref[h, rows, :] = m_new
                return carry

            def row_exp(k, carry, h=h, row0=row0):
                r0 = pl.multiple_of(k * ATTN_SUB, ATTN_SUB)
                m = m_ref[h, pl.ds(row0 + r0, ATTN_SUB), :]
                p_ref[pl.ds(r0, ATTN_SUB), :] = jnp.exp2(s_ref[pl.ds(r0, ATTN_SUB), :] - m).astype(BF16)
                return carry

            lax.fori_loop(0, ATTN_ROWS // ATTN_SUB, row_max, 0, unroll=ATTN_UNROLL)
            lax.fori_loop(0, ATTN_ROWS // ATTN_SUB, row_exp, 0, unroll=ATTN_UNROLL)
            blk = slice(row0, row0 + ATTN_ROWS)
            pv = _dot(p_ref[...], vb)
            alpha = alpha_ref[...]
            acc_ref[h, blk, :] = alpha * acc_ref[h, blk, :] + pv[:, :V_DIM]
            l_ref[h, blk, :] = alpha * l_ref[h, blk, :] + pv[:, V_DIM:]

    @pl.when(j == i)
    def _():
        lam = _diff_lambda(lp_ref[...], layer)
        for h in range(N_HEADS):
            o2 = acc_ref[h] / l_ref[h]
            o = o2[:t] - lam * o2[t:]
            o = _rms(o) * subln_ref[...] * (1.0 - _lambda_init(layer))
            o_ref[0, :, h * V_DIM:(h + 1) * V_DIM] = o.astype(BF16)


def _attn_prompt(q, kt, v, bias_tiles, rel_bias, lp, subln, *, layer, t):
    B, T, _ = q.shape
    kv_layer = kt.shape[0] - 1
    n = T // t
    pairs = [(i, j) for i in range(n) for j in range(i + 1)]
    qi = jnp.asarray([p[0] for p in pairs], jnp.int32)
    kj = jnp.asarray([p[1] for p in pairs], jnp.int32)
    grid_spec = pltpu.PrefetchScalarGridSpec(
        num_scalar_prefetch=2,
        grid=(B, len(pairs)),
        in_specs=[pl.BlockSpec((1, t, QK_W), lambda b, s, qi, kj: (b, qi[s], 0)),
                  pl.BlockSpec((1, 1, QK_W, t), lambda b, s, qi, kj: (kv_layer, b, 0, kj[s])),
                  pl.BlockSpec((1, 1, t * N_HEADS, V_DIM), lambda b, s, qi, kj: (kv_layer, b, kj[s], 0)),
                  pl.BlockSpec((N_HEADS, 1, t, t),
                               lambda b, s, qi, kj: (0, jnp.minimum(qi[s] - kj[s], 1), 0, 0)),
                  pl.BlockSpec(memory_space=pltpu.SMEM),
                  pl.BlockSpec(lp.shape, lambda *_: (0, 0)),
                  pl.BlockSpec(subln.shape, lambda *_: (0, 0))],
        out_specs=pl.BlockSpec((1, t, ATTN_W), lambda b, s, qi, kj: (b, qi[s], 0)),
        scratch_shapes=[pltpu.VMEM((N_HEADS, 2 * t, 1), F32),
                        pltpu.VMEM((N_HEADS, 2 * t, V_DIM), F32),
                        pltpu.VMEM((N_HEADS, 2 * t, V_DIM), F32),
                        pltpu.VMEM((ATTN_ROWS, t), F32),
                        pltpu.VMEM((ATTN_ROWS, t), BF16),
                        pltpu.VMEM((ATTN_ROWS, V_DIM), F32)])
    return pl.pallas_call(
        functools.partial(_attn_kernel, t=t, layer=layer),
        grid_spec=grid_spec,
        out_shape=jax.ShapeDtypeStruct((B, T, ATTN_W), BF16),
        compiler_params=_params(2),
        name="attn_prompt",
    )(qi, kj, q, kt, v, bias_tiles, rel_bias, lp, subln)


def _attn_sample_kernel(pt_ref, q_ref, kn_ref, vn_ref, *refs, n_pages, page, layer, group):
    k_refs = refs[:group * n_pages]
    v_refs = refs[group * n_pages:2 * group * n_pages]
    bias_ref, lp_ref, subln_ref, o_ref = refs[2 * group * n_pages:]
    for g in range(group):
        pages = slice(g * n_pages, (g + 1) * n_pages)
        o_ref[g] = _attn_sample_one(q_ref[g], kn_ref[g], vn_ref[g], k_refs[pages], v_refs[pages],
                                    bias_ref, lp_ref, subln_ref, n_pages=n_pages, page=page, layer=layer)


def _attn_sample_one(q_row, kn_row, vn_row, k_refs, v_refs, bias_ref, lp_ref, subln_ref, *, n_pages, page,
                     layer):
    rows = 2 * N_HEADS
    row = lax.broadcasted_iota(jnp.int32, (rows, QK_W), 0)
    seg = lax.broadcasted_iota(jnp.int32, (rows, QK_W), 1) // HEAD_DIM
    q = jnp.broadcast_to(q_row.astype(F32), (rows, QK_W))
    qbd_f = jnp.where(seg == (row % N_HEADS) * 2 + row // N_HEADS, q, 0.0)
    qbd = qbd_f.astype(BF16)

    s = jnp.concatenate([_dot(qbd, k_refs[p][0, 0].astype(BF16)) for p in range(n_pages)], axis=1)
    past = n_pages * page
    bias = bias_ref[...]
    bias = jnp.concatenate([bias, bias], axis=0)
    s = s + bias[:, :past]
    s_new = jnp.sum(qbd_f * kn_row, axis=-1, keepdims=True) + bias[:, past:past + 1]
    m = jnp.maximum(jnp.max(s, axis=-1, keepdims=True), s_new)
    p = jnp.exp2(s - m)
    p_new = jnp.exp2(s_new - m)
    denom = jnp.sum(p, axis=-1, keepdims=True) + p_new
    p = p / denom
    p_new = p_new / denom

    lam = _diff_lambda(lp_ref[...], layer)
    first = lax.broadcasted_iota(jnp.int32, (rows, 1), 0) < N_HEADS
    a = jnp.where(first, p - lam * pltpu.roll(p, N_HEADS, 0), 0.0).astype(BF16)
    a_new = p_new[:N_HEADS] - lam * p_new[N_HEADS:]

    out = jnp.zeros((rows, ATTN_W), F32)
    for pg in range(n_pages):
        vp = jnp.concatenate([v_refs[pg][0, 0, pl.ds(h, page, stride=N_HEADS), :]
                              for h in range(N_HEADS)], axis=1)
        out = out + _dot(a[:, pg * page:(pg + 1) * page], vp.astype(BF16))
    out = out[:N_HEADS] + a_new * vn_row
    rowh = lax.broadcasted_iota(jnp.int32, (N_HEADS, ATTN_W), 0)
    head = lax.broadcasted_iota(jnp.int32, (N_HEADS, ATTN_W), 1) // V_DIM
    om = jnp.where(rowh == head, out, 0.0)
    ms = jnp.sum(om * om, axis=-1, keepdims=True) / V_DIM
    on = om * lax.rsqrt(ms + EPS)
    o = jnp.sum(on, axis=0, keepdims=True) * subln_ref[...] * (1.0 - _lambda_init(layer))
    return o.astype(BF16)


def _attn_sample(q, k_new, v_new, cache_kt, cache_vr, page_table, bias_rows, lp, subln4, *, layer):
    DB = q.shape[0]
    n_pages = page_table.shape[1]
    page = cache_kt.shape[3]
    group = SAMPLES_PER_STEP
    assert DB % group == 0, (DB, group)
    slots = [(g, p) for g in range(group) for p in range(n_pages)]

    def page_spec(g, p):
        return pl.BlockSpec((1, 1, QK_W, page), lambda b, pt: (layer, pt[b * group + g, p], 0, 0))

    def vpage_spec(g, p):
        return pl.BlockSpec((1, 1, page * N_HEADS, V_DIM), lambda b, pt: (layer, pt[b * group + g, p], 0, 0))

    vec = pl.BlockSpec((group, 1, QK_W), lambda b, pt: (b, 0, 0))
    grid_spec = pltpu.PrefetchScalarGridSpec(
        num_scalar_prefetch=1,
        grid=(DB // group,),
        in_specs=([vec, vec, vec] + [page_spec(g, p) for g, p in slots]
                  + [vpage_spec(g, p) for g, p in slots]
                  + [pl.BlockSpec(bias_rows.shape, lambda *_: (0, 0)),
                     pl.BlockSpec(lp.shape, lambda *_: (0, 0)),
                     pl.BlockSpec(subln4.shape, lambda *_: (0, 0))]),
        out_specs=vec)
    return pl.pallas_call(
        functools.partial(_attn_sample_kernel, n_pages=n_pages, page=page, layer=layer, group=group),
        grid_spec=grid_spec,
        out_shape=jax.ShapeDtypeStruct((DB, 1, ATTN_W), BF16),
        compiler_params=_params(1),
        name="attn_sample",
    )(page_table, q, k_new, v_new, *([cache_kt] * len(slots)), *([cache_vr] * len(slots)),
      bias_rows, lp, subln4)


def _pool_sample_kernel(state_ref, u_ref, d_ref):
    u = u_ref[...]
    outs = []
    for gi, w in enumerate(POOL_WINDOWS):
        ch = slice(gi * POOL_GROUP_W, (gi + 1) * POOL_GROUP_W)
        s = u[:, ch]
        for r in range(POOL_BUF - (w - 1), POOL_BUF):
            s = s + state_ref[r, :, ch]
        outs.append(s / float(w) - u[:, ch])
    d_ref[...] = jnp.concatenate(outs, axis=-1).astype(BF16)


def _pool_sample(state_t, u):
    return pl.pallas_call(
        _pool_sample_kernel,
        out_shape=jax.ShapeDtypeStruct(u.shape, BF16),
        name="pool_sample",
    )(state_t, u)


def _mix_kernel(o_ref, d_ref, g_ref, x_ref, gt_ref, sh_ref, sc_ref, wp_ref, ps_ref, wb_ref, wo_ref,
                gffn_ref, *rest, router):
    d = d_ref[0]
    y = jnp.concatenate([_dot(d[:, g * POOL_GROUP_W:(g + 1) * POOL_GROUP_W], wp_ref[g])
                         for g in range(len(POOL_WINDOWS))], axis=-1)
    y = (y * ps_ref[...]).astype(BF16)
    pa = _dot(o_ref[0], wb_ref[:ATTN_W])
    pb = _dot(y, wb_ref[ATTN_W:])
    g = g_ref[0]
    merged = g[:, :D_MODEL].astype(F32) * pa + g[:, D_MODEL:].astype(F32) * pb
    xn = x_ref[0] + gt_ref[0] * _dot(merged.astype(BF16), wo_ref[...])
    h2 = _rms(xn) * gffn_ref[...]
    h2 = h2 * (1.0 + sc_ref[0]) + sh_ref[0]
    if not router:
        xn_ref, h2_ref = rest
        xn_ref[0] = xn
        h2_ref[0] = h2.astype(h2_ref.dtype)
        return

    wr_hi_ref, wr_lo_ref, xn_ref, h2_ref, ri_ref, rp_ref = rest
    xn_ref[0] = xn
    _slab_store(h2_ref, (0,), h2)
    hi = h2.astype(BF16)
    lo = (h2 - hi.astype(F32)).astype(BF16)
    logits = _dot(hi, wr_hi_ref[...]) + _dot(lo, wr_hi_ref[...]) + _dot(hi, wr_lo_ref[...])
    lane = lax.broadcasted_iota(jnp.int32, logits.shape, 1)
    lg = jnp.where(lane < N_EXPERTS, logits, -jnp.inf)
    m1 = jnp.max(lg, axis=-1, keepdims=True)
    i1 = jnp.min(jnp.where(lg == m1, lane, LANES), axis=-1, keepdims=True)
    lg = jnp.where(lane == i1, -jnp.inf, lg)
    m2 = jnp.max(lg, axis=-1, keepdims=True)
    i2 = jnp.min(jnp.where(lg == m2, lane, LANES), axis=-1, keepdims=True)
    e = jnp.exp(m2 - m1)
    ri_ref[0] = jnp.concatenate([i1, i2], axis=-1)
    rp_ref[0] = jnp.concatenate([1.0 / (1.0 + e), e / (1.0 + e)], axis=-1)


def _mix(o, d, g, x, mod, wp_bf, pscale, wb_bf, wo_bf, gffn, router_w, *, tm):
    B, T, _ = x.shape
    router = router_w is not None
    ins = [o, d, g, x, mod, mod, mod, wp_bf, pscale, wb_bf, wo_bf, gffn]
    in_specs = [_row_spec(tm, ATTN_W), _row_spec(tm, POOL_W), _row_spec(tm, 2 * D_MODEL),
                _row_spec(tm, D_MODEL), _mod_spec(mod, tm, 2), _mod_spec(mod, tm, 3),
                _mod_spec(mod, tm, 4)] + [_const_spec(a) for a in ins[7:]]
    out_shape = [jax.ShapeDtypeStruct((B, T, D_MODEL), F32), jax.ShapeDtypeStruct((B, T, D_MODEL), BF16)]
    out_specs = [_row_spec(tm, D_MODEL), _row_spec(tm, D_MODEL)]
    if router:
        out_shape[1] = jax.ShapeDtypeStruct((B, T * ROW_SLAB, LANES), F32)
        out_specs[1] = _row_spec(tm * ROW_SLAB, LANES)
        ins += list(router_w)
        in_specs += [_const_spec(a) for a in router_w]
        out_shape += [jax.ShapeDtypeStruct((B, T, TOP_K), jnp.int32),
                      jax.ShapeDtypeStruct((B, T, TOP_K), F32)]
        out_specs += [_row_spec(tm, TOP_K), _row_spec(tm, TOP_K)]
    return pl.pallas_call(
        functools.partial(_mix_kernel, router=router),
        grid=(B, T // tm),
        in_specs=in_specs,
        out_specs=out_specs,
        out_shape=out_shape,
        compiler_params=_params(2),
        name="mix",
    )(*ins)


def _ffn_kernel(h_ref, x_ref, gt_ref, wg_ref, wu_ref, wd_ref, y_ref, acc_ref):
    f = pl.program_id(2)

    @pl.when(f == 0)
    def _():
        acc_ref[...] = jnp.zeros_like(acc_ref)

    h = h_ref[0]
    a = (_silu(_dot(h, wg_ref[...])) * _dot(h, wu_ref[...])).astype(BF16)
    acc_ref[...] += _dot(a, wd_ref[...])

    @pl.when(f == pl.num_programs(2) - 1)
    def _():
        y_ref[0] = x_ref[0] + gt_ref[0] * acc_ref[...]


def _ffn_dense(h2, x, mod, wg_bf, wu_bf, wd_bf, *, tm):
    B, T, _ = x.shape
    d_ff = wg_bf.shape[1]
    tf = d_ff // 2
    return pl.pallas_call(
        _ffn_kernel,
        grid=(B, T // tm, d_ff // tf),
        in_specs=[_row_spec(tm, D_MODEL), _row_spec(tm, D_MODEL), _mod_spec(mod, tm, 5),
                  pl.BlockSpec((D_MODEL, tf), lambda b, i, f: (0, f)),
                  pl.BlockSpec((D_MODEL, tf), lambda b, i, f: (0, f)),
                  pl.BlockSpec((tf, D_MODEL), lambda b, i, f: (f, 0))],
        out_specs=_row_spec(tm, D_MODEL),
        out_shape=jax.ShapeDtypeStruct((B, T, D_MODEL), F32),
        scratch_shapes=[pltpu.VMEM((tm, D_MODEL), F32)],
        compiler_params=_params(3),
        name="ffn_dense",
    )(h2, x, mod, wg_bf, wu_bf, wd_bf)


ROW_SLAB = D_MODEL // LANES


def _slab_copy(src_hbm, src_row, dst, dst_row, sem):
    src = src_hbm.at[pl.ds(pl.multiple_of(src_row * ROW_SLAB, ROW_SLAB), ROW_SLAB), :]
    return pltpu.make_async_copy(src, dst.at[pl.ds(pl.multiple_of(dst_row * ROW_SLAB, ROW_SLAB), ROW_SLAB), :], sem)


def _slab_store(ref, idx, x):
    rows = x.shape[0]
    for s in range(ROW_SLAB):
        ref[idx + (pl.ds(s, rows, stride=ROW_SLAB), slice(None))] = x[:, s * LANES:(s + 1) * LANES]


def _slab_load(ref, idx, rows, start=0):
    return jnp.concatenate([ref[idx + (pl.ds(start + s, rows, stride=ROW_SLAB), slice(None))]
                            for s in range(ROW_SLAB)], axis=1)


def _moe_ffn_kernel(te_ref, nu_ref, src_ref, src_next_ref, h_hbm, wg_ref, wu_ref, wd_ref, y_ref,
                    xbuf, xb_ref, acc_ref, sem, *, tm):
    i = pl.program_id(0)
    f = pl.program_id(1)
    nu = nu_ref[0]
    slot = i % 2

    def issue(idx_ref, dst):
        def body(r, carry):
            _slab_copy(h_hbm, idx_ref[0, 0, r], xbuf.at[dst], r, sem.at[dst]).start()
            return carry
        lax.fori_loop(0, tm, body, 0, unroll=8)

    def drain(dst):
        def body(r, carry):
            _slab_copy(h_hbm, 0, xbuf.at[dst], 0, sem.at[dst]).wait()
            return carry
        lax.fori_loop(0, tm, body, 0, unroll=8)

    @pl.when(i < nu)
    def _():
        @pl.when(f == 0)
        def _():
            @pl.when(i == 0)
            def _():
                issue(src_ref, 0)

            drain(slot)

            @pl.when(i + 1 < nu)
            def _():
                issue(src_next_ref, 1 - slot)

            acc_ref[...] = jnp.zeros_like(acc_ref)
            xb_ref[...] = _slab_load(xbuf, (slot,), tm).astype(BF16)

        xb = xb_ref[...]
        a = (_silu(_dot(xb, wg_ref[0].astype(BF16))) * _dot(xb, wu_ref[0].astype(BF16))).astype(BF16)
        acc_ref[...] += _dot(a, wd_ref[0].astype(BF16))

        @pl.when(f == pl.num_programs(1) - 1)
        def _():
            _slab_store(y_ref, (), acc_ref[...])

    @pl.when(i >= nu)
    def _():
        y_ref[...] = jnp.zeros_like(y_ref)


def _moe_ffn(h_rows, src, tile_expert, n_used, wg, wu, wd, *, tm, tf):
    n_tiles = src.shape[0]
    d_ff = wg.shape[2]
    assert d_ff % tf == 0, (d_ff, tf)
    nf = d_ff // tf

    def tile(i, nu):
        return jnp.minimum(i, nu[0] - 1)

    def ff(i, f, nu):
        return jnp.where(i < nu[0], f, nf - 1)

    grid_spec = pltpu.PrefetchScalarGridSpec(
        num_scalar_prefetch=2,
        grid=(n_tiles, nf),
        in_specs=[pl.BlockSpec((1, 1, tm), lambda i, f, te, nu: (i, 0, 0), memory_space=pltpu.SMEM),
                  pl.BlockSpec((1, 1, tm), lambda i, f, te, nu: (jnp.minimum(i + 1, n_tiles - 1), 0, 0),
                               memory_space=pltpu.SMEM),
                  pl.BlockSpec(memory_space=pl.ANY),
                  pl.BlockSpec((1, D_MODEL, tf), lambda i, f, te, nu: (te[tile(i, nu)], 0, ff(i, f, nu))),
                  pl.BlockSpec((1, D_MODEL, tf), lambda i, f, te, nu: (te[tile(i, nu)], 0, ff(i, f, nu))),
                  pl.BlockSpec((1, tf, D_MODEL), lambda i, f, te, nu: (te[tile(i, nu)], ff(i, f, nu), 0))],
        out_specs=pl.BlockSpec((tm * ROW_SLAB, LANES), lambda i, f, te, nu: (i, 0)),
        scratch_shapes=[pltpu.VMEM((2, tm * ROW_SLAB, LANES), F32), pltpu.VMEM((tm, D_MODEL), BF16),
                        pltpu.VMEM((tm, D_MODEL), F32),
                        pltpu.SemaphoreType.DMA((2,))])
    return pl.pallas_call(
        functools.partial(_moe_ffn_kernel, tm=tm),
        grid_spec=grid_spec,
        out_shape=jax.ShapeDtypeStruct((n_tiles * tm * ROW_SLAB, LANES), F32),
        compiler_params=_params(2),
        name="moe_ffn",
    )(tile_expert, n_used, src, src, h_rows, wg, wu, wd)


def _combine_kernel(pos_ref, pos_next_ref, y_hbm, x_ref, gt_ref, p_ref, o_ref, buf, sem, *, tc):
    g = pl.program_id(0) * pl.num_programs(1) + pl.program_id(1)
    n_steps = pl.num_programs(0) * pl.num_programs(1)
    slot = g % 2

    def issue(idx_ref, dst):
        def body(t, carry):
            for k in range(TOP_K):
                _slab_copy(y_hbm, idx_ref[0, 0, TOP_K * t + k], buf.at[dst, k], t, sem.at[dst]).start()
            return carry
        lax.fori_loop(0, tc, body, 0, unroll=4)

    @pl.when(g == 0)
    def _():
        issue(pos_ref, 0)

    def drain(t, carry):
        for k in range(TOP_K):
            _slab_copy(y_hbm, 0, buf.at[slot, 0], 0, sem.at[slot]).wait()
        return carry

    lax.fori_loop(0, tc, drain, 0, unroll=4)

    @pl.when(g + 1 < n_steps)
    def _():
        issue(pos_next_ref, 1 - slot)

    p = p_ref[0]
    f = p[:, 0:1] * _slab_load(buf, (slot, 0), tc) + p[:, 1:2] * _slab_load(buf, (slot, 1), tc)
    o_ref[0] = x_ref[0] + gt_ref[0] * f


def _combine(y, pos, probs, x, mod, *, tc):
    B, T, _ = x.shape
    nt = T // tc
    last = B * nt - 1
    pos_chunks = pos.reshape(B * nt, 1, TOP_K * tc)
    return pl.pallas_call(
        functools.partial(_combine_kernel, tc=tc),
        grid=(B, nt),
        in_specs=[pl.BlockSpec((1, 1, TOP_K * tc), lambda b, i: (b * nt + i, 0, 0),
                               memory_space=pltpu.SMEM),
                  pl.BlockSpec((1, 1, TOP_K * tc), lambda b, i: (jnp.minimum(b * nt + i + 1, last), 0, 0),
                               memory_space=pltpu.SMEM),
                  pl.BlockSpec(memory_space=pl.ANY),
                  _row_spec(tc, D_MODEL), _mod_spec(mod, tc, 5), _row_spec(tc, TOP_K)],
        out_specs=_row_spec(tc, D_MODEL),
        out_shape=jax.ShapeDtypeStruct(x.shape, F32),
        scratch_shapes=[pltpu.VMEM((2, TOP_K, tc * ROW_SLAB, LANES), F32), pltpu.SemaphoreType.DMA((2,))],
        compiler_params=_params(2),
        name="moe_combine",
    )(pos_chunks, pos_chunks, y, x, mod, probs)


def _route_plan(top_i, tm, n_tiles):
    e_flat = top_i.reshape(-1)
    onehot = (e_flat[:, None] == jnp.arange(N_EXPERTS, dtype=jnp.int32)[None, :]).astype(jnp.int32)
    csum = jnp.cumsum(onehot, axis=0)
    cnt = csum[-1]
    gsz = (cnt + tm - 1) // tm * tm
    gend = jnp.cumsum(gsz)
    gstart = gend - gsz
    pos = jnp.sum(onehot * (gstart[None, :] + csum - 1), axis=1).astype(jnp.int32)
    n_used = (gend[-1] // tm).astype(jnp.int32).reshape(1)
    tile_start = jnp.arange(n_tiles, dtype=jnp.int32) * tm
    tile_expert = jnp.sum((tile_start[:, None] >= gend[None, :]).astype(jnp.int32), axis=1)
    tile_expert = jnp.minimum(tile_expert, N_EXPERTS - 1).astype(jnp.int32)
    token = jnp.arange(e_flat.shape[0], dtype=jnp.int32) // TOP_K
    src = jnp.zeros((n_tiles * tm,), jnp.int32).at[pos].set(token, unique_indices=True)
    return pos.reshape(top_i.shape), src.reshape(n_tiles, 1, tm), tile_expert, n_used


def _moe(groups, wg, wu, wd):
    tm = MOE_ROW_TILE
    counts = [g[1].shape[0] * g[1].shape[1] for g in groups]
    total = sum(counts) * TOP_K
    n_tiles = (total + N_EXPERTS * (tm - 1)) // tm + 1
    top_all = jnp.concatenate([g[3].reshape(-1, TOP_K) for g in groups], axis=0)
    h_all = jnp.concatenate([g[0].reshape(-1, LANES) for g in groups], axis=0)
    pos_all, src, tile_expert, n_used = _route_plan(top_all, tm, n_tiles)
    y = _moe_ffn(h_all, src, tile_expert, n_used, wg, wu, wd, tm=tm, tf=MOE_FF_TILE)
    outs = []
    offs = 0
    for (h2, x, mod, top_i, probs), cnt in zip(groups, counts):
        pos = pos_all[offs:offs + cnt].reshape(top_i.shape)
        offs += cnt
        outs.append(_combine(y, pos, probs, x, mod, tc=min(MOE_DMA_CHUNK, x.shape[1])))
    return outs


def kernel(x_prompt, x_sample, c_prompt, c_sample, cache_k, cache_v, state_pool, page_table,
           rel_bias, w_ada, b_ada, g_mix, g_ffn, w_in, q_norm, k_norm, diff_lambda, subln,
           w_pool, pool_scale, w_branch, w_out, w_ff_gate, w_ff_up, w_ff_down, w_router,
           w_exp_gate, w_exp_up, w_exp_down):
    B, S, D = x_prompt.shape
    DB = x_sample.shape[0]
    depth = w_in.shape[0]
    n_phys, page = cache_k.shape[1], cache_k.shape[2]
    past = page_table.shape[1] * page

    mod_all = _ada_mod(jnp.concatenate([c_prompt, c_sample], axis=0), w_ada, b_ada)
    bias_tiles = _bias_tiles(rel_bias, ATTN_TILE)
    bias_rows = _bias_rows(rel_bias, past)

    cache_kt = jnp.transpose(cache_k, (0, 1, 3, 4, 5, 2)).reshape(depth, n_phys, QK_W, page)
    cache_vr = cache_v.reshape(depth, n_phys, page * N_HEADS, V_DIM)
    state_t = jnp.transpose(state_pool, (0, 2, 1, 3))

    bd = jnp.kron(jnp.eye(QK_W // HEAD_DIM, dtype=F32),
                  jnp.full((HEAD_DIM, HEAD_DIM), 1.0 / HEAD_DIM, F32)).astype(BF16)

    xp = x_prompt
    xs = x_sample.reshape(1, DB, D)
    outs = {k: [] for k in ("pp", "ks", "vs", "ps")}
    prev_kv = ((), ())
    for l in range(depth):
        mod_p = mod_all[l, :B].reshape(B, 1, 6 * D)
        mod_s = mod_all[l, B:].reshape(1, DB, 6 * D)
        w_bf = w_in[l].astype(BF16)
        wkt_bf = w_in[l][:, QK_W:2 * QK_W].T.astype(BF16)
        qg = jnp.tile(q_norm[l].reshape(1, 2 * HEAD_DIM), (1, N_HEADS)) * (HEAD_DIM ** -0.5 * LOG2E)
        kg = jnp.tile(k_norm[l], (N_HEADS, 1)).reshape(N_HEADS * 2, HEAD_DIM, 1)
        gmix = g_mix[l].reshape(1, D)
        gffn = g_ffn[l].reshape(1, D)
        lp = diff_lambda[l]
        sub1 = subln[l].reshape(1, V_DIM)
        sub4 = jnp.tile(sub1, (1, N_HEADS))
        wp_bf = w_pool[l].astype(BF16)
        pscale = pool_scale[l].reshape(1, POOL_W)
        wb_bf = w_branch[l].astype(BF16)
        wo_bf = w_out[l].astype(BF16)
        moe = l % 2 == 1
        j = l // 2
        router_w = None
        if moe:
            wr = jnp.pad(w_router[j], ((0, 0), (0, LANES - N_EXPERTS)))
            wr_hi = wr.astype(BF16)
            router_w = (wr_hi, (wr - wr_hi.astype(F32)).astype(BF16))

        stack = prev_kv if l == depth - 1 else ((), ())
        qp, ktp, vp, up, gp, dp = _in_proj(xp, mod_p, gmix, w_bf, wkt_bf, qg, kg, bd, *stack,
                                           tm=ROW_TILE, seq_pool=True)
        prev_kv = (prev_kv[0] + (ktp,), prev_kv[1] + (vp,))
        op = _attn_prompt(qp, ktp, vp, bias_tiles, rel_bias, lp, sub1, layer=l, t=ATTN_TILE)
        mix_p = _mix(op, dp, gp, xp, mod_p, wp_bf, pscale, wb_bf, wo_bf, gffn, router_w, tm=ROW_TILE)

        qs, kts, vs, us, gs = _in_proj(xs, mod_s, gmix, w_bf, wkt_bf, qg, kg, bd,
                                       tm=DB, seq_pool=False)
        ks_rows = jnp.transpose(kts[0], (0, 2, 1)).reshape(DB, 1, QK_W)
        osamp = _attn_sample(qs.reshape(DB, 1, QK_W), ks_rows, vs.reshape(DB, 1, ATTN_W),
                             cache_kt, cache_vr, page_table, bias_rows, lp, sub4, layer=l)
        ds = _pool_sample(state_t[l], us[0])
        mix_s = _mix(osamp.reshape(1, DB, ATTN_W), ds.reshape(1, DB, POOL_W), gs, xs, mod_s,
                     wp_bf, pscale, wb_bf, wo_bf, gffn, router_w, tm=DB)

        if moe:
            xp, xs = _moe([(mix_p[1], mix_p[0], mod_p, mix_p[2], mix_p[3]),
                           (mix_s[1], mix_s[0], mod_s, mix_s[2], mix_s[3])],
                          w_exp_gate[j], w_exp_up[j], w_exp_down[j])
        else:
            wg_bf = w_ff_gate[j].astype(BF16)
            wu_bf = w_ff_up[j].astype(BF16)
            wd_bf = w_ff_down[j].astype(BF16)
            xp = _ffn_dense(mix_p[1], mix_p[0], mod_p, wg_bf, wu_bf, wd_bf, tm=FFN_ROW_TILE)
            xs = _ffn_dense(mix_s[1], mix_s[0], mod_s, wg_bf, wu_bf, wd_bf, tm=DB)

        outs["pp"].append(up[:, S - POOL_BUF:, :])
        outs["ks"].append(ks_rows.reshape(DB, 1, N_HEADS, 2, HEAD_DIM))
        outs["vs"].append(vs.reshape(DB, 1, N_HEADS, V_DIM))
        outs["ps"].append(jnp.concatenate([state_pool[l][:, 1:], us.reshape(DB, 1, POOL_W)], axis=1))

    return (xp, xs.reshape(DB, 1, D),
            jnp.transpose(ktp.reshape(depth, B, N_HEADS, 2, HEAD_DIM, S), (0, 1, 5, 2, 3, 4)),
            vp.reshape(depth, B, S, N_HEADS, V_DIM), jnp.stack(outs["pp"]),
            jnp.stack(outs["ks"]), jnp.stack(outs["vs"]), jnp.stack(outs["ps"]))
```

```python
import functools
import math

import jax
import jax.numpy as jnp
from jax import lax
from jax.experimental import pallas as pl
from jax.experimental.pallas import tpu as pltpu

F32 = jnp.float32
BF16 = jnp.bfloat16

D_MODEL = 1024
N_HEADS = 4
HEAD_DIM = 64
V_DIM = 2 * HEAD_DIM
QK_W = N_HEADS * 2 * HEAD_DIM
ATTN_W = N_HEADS * V_DIM
POOL_WINDOWS = (2, 4, 8, 16)
POOL_W = D_MODEL // 2
POOL_GROUP_W = POOL_W // len(POOL_WINDOWS)
POOL_BUF = max(POOL_WINDOWS) - 1
POOL_HALO = max(POOL_WINDOWS)
IN_COLS = 2 * QK_W + ATTN_W + POOL_W + 2 * D_MODEL
N_BUCKETS = 32
MAX_DISTANCE = 128
N_EXPERTS = 8
TOP_K = 2
EPS = 1e-6
LOG2E = math.log2(math.e)
LANES = 128
V7X_VMEM_BYTES = 64 * 1024 * 1024
VMEM_LIMIT = V7X_VMEM_BYTES * 7 // 8

ATTN_TILE = 512
ATTN_ROWS = 256
ATTN_SUB = 32
ATTN_UNROLL = ATTN_ROWS // ATTN_SUB
SAMPLES_PER_STEP = 2
ROW_TILE = 512
FFN_ROW_TILE = 1024
MOE_ROW_TILE = 1024
MOE_FF_TILE = 512
MOE_DMA_CHUNK = 1024

_NT = (((1,), (1,)), ((), ()))


def _lambda_init(layer):
    return 0.8 - 0.6 * math.exp(-0.3 * layer)


def _params(n_axes, vmem=VMEM_LIMIT):
    return pltpu.CompilerParams(dimension_semantics=("arbitrary",) * n_axes,
                                vmem_limit_bytes=vmem)


def _dot(a, b):
    return jnp.dot(a, b, preferred_element_type=F32)


def _rms(x):
    return x * lax.rsqrt(jnp.mean(x * x, axis=-1, keepdims=True) + EPS)


def _silu(x):
    return x * jax.nn.sigmoid(x)


def _const_spec(a):
    nd = a.ndim
    return pl.BlockSpec(a.shape, lambda *_: (0,) * nd)


def _row_spec(tm, width):
    return pl.BlockSpec((1, tm, width), lambda b, i, *_: (b, i, 0))


def _mod_spec(mod, tm, chunk):
    if mod.shape[1] == 1:
        return pl.BlockSpec((1, 1, D_MODEL), lambda b, i, *_: (b, 0, chunk))
    return pl.BlockSpec((1, tm, D_MODEL), lambda b, i, *_: (b, i, chunk))


def _ada_kernel(c_ref, w_ref, b_ref, o_ref):
    a = _silu(c_ref[...]).astype(BF16)
    o_ref[0] = _dot(a, w_ref[0].astype(BF16)) + b_ref[0]


def _ada_mod(c_all, w_ada, b_ada):
    rows = c_all.shape[0]
    depth, _, cols = w_ada.shape
    tn = cols // 4
    return pl.pallas_call(
        _ada_kernel,
        grid=(depth, cols // tn),
        in_specs=[pl.BlockSpec((rows, D_MODEL), lambda l, n: (0, 0)),
                  pl.BlockSpec((1, D_MODEL, tn), lambda l, n: (l, 0, n)),
                  pl.BlockSpec((1, 1, tn), lambda l, n: (l, 0, n))],
        out_specs=pl.BlockSpec((1, rows, tn), lambda l, n: (l, 0, n)),
        out_shape=jax.ShapeDtypeStruct((depth, rows, cols), F32),
        compiler_params=_params(2),
        name="ada_mod",
    )(c_all, w_ada, b_ada.reshape(depth, 1, cols))


def _t5_bucket(rel):
    n = jnp.maximum(rel, 0)
    max_exact = N_BUCKETS // 2
    nf = jnp.maximum(n, 1).astype(F32)
    large = max_exact + (jnp.log(nf / max_exact) / math.log(MAX_DISTANCE / max_exact)
                         * (N_BUCKETS - max_exact)).astype(jnp.int32)
    large = jnp.minimum(large, N_BUCKETS - 1)
    return jnp.where(n < max_exact, n, large)


def _bias_lookup(bucket, rb_ref, h):
    acc = jnp.zeros(bucket.shape, F32)
    for b in range(N_BUCKETS):
        acc = jnp.where(bucket == b, rb_ref[b, h], acc)
    return acc


def _bias_tile_kernel(rb_ref, o_ref, *, t):
    h = pl.program_id(0)
    d = pl.program_id(1)
    r = lax.broadcasted_iota(jnp.int32, (t, t), 0)
    c = lax.broadcasted_iota(jnp.int32, (t, t), 1)
    rel = d * t + r - c
    val = _bias_lookup(_t5_bucket(rel), rb_ref, h)
    o_ref[0, 0] = jnp.where(rel >= 0, val * LOG2E, -jnp.inf)


def _bias_tiles(rel_bias, t):
    return pl.pallas_call(
        functools.partial(_bias_tile_kernel, t=t),
        grid=(N_HEADS, 2),
        in_specs=[pl.BlockSpec(memory_space=pltpu.SMEM)],
        out_specs=pl.BlockSpec((1, 1, t, t), lambda h, d: (h, d, 0, 0)),
        out_shape=jax.ShapeDtypeStruct((N_HEADS, 2, t, t), F32),
        compiler_params=_params(2),
        name="bias_tiles",
    )(rel_bias)


def _bias_row_kernel(rb_ref, o_ref, *, past):
    width = o_ref.shape[1]
    c = lax.broadcasted_iota(jnp.int32, (1, width), 1)
    bucket = _t5_bucket(jnp.maximum(past - c, 0))
    for h in range(N_HEADS):
        o_ref[h:h + 1, :] = _bias_lookup(bucket, rb_ref, h) * LOG2E


def _bias_rows(rel_bias, past):
    return pl.pallas_call(
        functools.partial(_bias_row_kernel, past=past),
        in_specs=[pl.BlockSpec(memory_space=pltpu.SMEM)],
        out_shape=jax.ShapeDtypeStruct((N_HEADS, past + LANES), F32),
        name="bias_rows",
    )(rel_bias)


def _window_means(ext, u, pos):
    tm = u.shape[0]
    outs = []
    for gi, w in enumerate(POOL_WINDOWS):
        ch = slice(gi * POOL_GROUP_W, (gi + 1) * POOL_GROUP_W)
        s = ext[:, ch]
        span = 1
        while span < w:
            s = s[span:] + s[:-span]
            span *= 2
        off = POOL_HALO - (w - 1)
        win = s[off:off + tm]
        cnt = jnp.minimum(pos + 1, w).astype(F32)
        outs.append(win / cnt - u[:, ch])
    return jnp.concatenate(outs, axis=-1)


def _in_proj_kernel(x_ref, sh_ref, sc_ref, gmix_ref, w_ref, wkt_ref, qg_ref, kg_ref, bd_ref,
                    *rest, tm, seq_pool, n_prev):
    prev_k, prev_v = rest[:n_prev], rest[n_prev:2 * n_prev]
    q_ref, kt_ref, v_ref, u_ref, g_ref = rest[2 * n_prev:2 * n_prev + 5]
    rest = rest[2 * n_prev + 5:]
    for p in range(n_prev):
        kt_ref[p, 0] = prev_k[p][0, 0]
        v_ref[p, 0] = prev_v[p][0, 0]
    h = _rms(x_ref[0]) * gmix_ref[...]
    h = h * (1.0 + sc_ref[0]) + sh_ref[0]
    hb = h.astype(BF16)

    def proj(c0, c1):
        return _dot(hb, w_ref[:, c0:c1])

    zq = proj(0, QK_W)
    msq = _dot((zq * zq).astype(BF16), bd_ref[...])
    q_ref[0] = (zq * lax.rsqrt(msq + EPS) * qg_ref[...]).astype(BF16)

    zk = lax.dot_general(wkt_ref[...], hb, _NT, preferred_element_type=F32)
    zk = zk.reshape(N_HEADS * 2, HEAD_DIM, tm)
    msk = jnp.mean(zk * zk, axis=1, keepdims=True)
    kt_ref[n_prev, 0] = (zk * lax.rsqrt(msk + EPS) * kg_ref[...]).reshape(QK_W, tm)

    c0 = 2 * QK_W
    zv = proj(c0, c0 + ATTN_W)
    for hd in range(N_HEADS):
        v_ref[n_prev, 0, pl.ds(hd, tm, stride=N_HEADS), :] = zv[:, hd * V_DIM:(hd + 1) * V_DIM]
    c0 += ATTN_W
    u = proj(c0, c0 + POOL_W)
    u_ref[0] = u
    c0 += POOL_W
    for c in range(0, 2 * D_MODEL, 512):
        g_ref[0, :, c:c + 512] = jax.nn.sigmoid(proj(c0 + c, c0 + c + 512)).astype(BF16)

    if seq_pool:
        d_ref, carry_ref = rest
        i = pl.program_id(1)

        @pl.when(i == 0)
        def _():
            carry_ref[...] = jnp.zeros_like(carry_ref)

        ext = jnp.concatenate([carry_ref[...], u], axis=0)
        carry_ref[...] = u[tm - POOL_HALO:, :]
        pos = i * tm + lax.broadcasted_iota(jnp.int32, (tm, 1), 0)
        d_ref[0] = _window_means(ext, u, pos).astype(BF16)


def _in_proj(x, mod, gmix, w_bf, wkt_bf, qg, kg, bd, prev_k=(), prev_v=(), *, tm, seq_pool):
    B, T, _ = x.shape
    nt = T // tm
    n_prev = len(prev_k)
    k_block = lambda n: pl.BlockSpec((n, 1, QK_W, tm), lambda b, i: (0, b, 0, i))
    v_block = lambda n: pl.BlockSpec((n, 1, tm * N_HEADS, V_DIM), lambda b, i: (0, b, i, 0))
    out_shape = [jax.ShapeDtypeStruct((B, T, QK_W), BF16),
                 jax.ShapeDtypeStruct((n_prev + 1, B, QK_W, T), F32),
                 jax.ShapeDtypeStruct((n_prev + 1, B, T * N_HEADS, V_DIM), F32),
                 jax.ShapeDtypeStruct((B, T, POOL_W), F32),
                 jax.ShapeDtypeStruct((B, T, 2 * D_MODEL), BF16)]
    out_specs = [_row_spec(tm, QK_W), k_block(n_prev + 1), v_block(n_prev + 1),
                 _row_spec(tm, POOL_W), _row_spec(tm, 2 * D_MODEL)]
    scratch = []
    if seq_pool:
        out_shape.append(jax.ShapeDtypeStruct((B, T, POOL_W), BF16))
        out_specs.append(_row_spec(tm, POOL_W))
        scratch.append(pltpu.VMEM((POOL_HALO, POOL_W), F32))
    return pl.pallas_call(
        functools.partial(_in_proj_kernel, tm=tm, seq_pool=seq_pool, n_prev=n_prev),
        grid=(B, nt),
        in_specs=[_row_spec(tm, D_MODEL), _mod_spec(mod, tm, 0), _mod_spec(mod, tm, 1),
                  _const_spec(gmix), _const_spec(w_bf), _const_spec(wkt_bf),
                  _const_spec(qg), _const_spec(kg), _const_spec(bd)]
                 + [k_block(1)] * n_prev + [v_block(1)] * n_prev,
        out_specs=out_specs,
        out_shape=out_shape,
        scratch_shapes=scratch,
        compiler_params=_params(2),
        name="in_proj",
    )(x, mod, mod, gmix, w_bf, wkt_bf, qg, kg, bd, *prev_k, *prev_v)


def _diff_lambda(lp, layer):
    a = jnp.sum(lp[0:1] * lp[1:2], axis=-1, keepdims=True)
    b = jnp.sum(lp[2:3] * lp[3:4], axis=-1, keepdims=True)
    return jnp.exp(a) - jnp.exp(b) + _lambda_init(layer)


def _attn_kernel(qi_ref, kj_ref, q_ref, kt_ref, v_ref, bias_ref, rb_ref, lp_ref, subln_ref,
                 o_ref, m_ref, l_ref, acc_ref, s_ref, p_ref, alpha_ref, *, t, layer):
    s_idx = pl.program_id(1)
    i = qi_ref[s_idx]
    j = kj_ref[s_idx]

    @pl.when(j == 0)
    def _():
        m_ref[...] = jnp.full_like(m_ref, -jnp.inf)
        l_ref[...] = jnp.zeros_like(l_ref)
        acc_ref[...] = jnp.zeros_like(acc_ref)

    lane = lax.broadcasted_iota(jnp.int32, (1, V_DIM), 1)
    map0 = jnp.where(lane < HEAD_DIM, 1.0, 0.0).astype(BF16)
    map1 = jnp.where(lane >= HEAD_DIM, 1.0, 0.0).astype(BF16)
    far = (i - j) >= 2
    for h in range(N_HEADS):
        hs = slice(h * V_DIM, (h + 1) * V_DIM)
        q = q_ref[0, :, hs]
        kb = kt_ref[0, 0, hs, :].astype(BF16)
        vb = jnp.concatenate([v_ref[0, 0, pl.ds(h, t, stride=N_HEADS), :].astype(BF16),
                              jnp.ones((t, V_DIM), BF16)], axis=1)
        far_bias = rb_ref[N_BUCKETS - 1, h] * LOG2E
        for c in range(2 * t // ATTN_ROWS):
            qrow = (c * ATTN_ROWS) % t
            qc = q[qrow:qrow + ATTN_ROWS] * (map0 if c * ATTN_ROWS < t else map1)
            s_ref[...] = _dot(qc, kb)
            row0 = c * ATTN_ROWS

            def row_max(k, carry, h=h, qrow=qrow, row0=row0, far_bias=far_bias):
                r0 = pl.multiple_of(k * ATTN_SUB, ATTN_SUB)
                rows = pl.ds(row0 + r0, ATTN_SUB)
                bias = jnp.where(far, far_bias, bias_ref[h, 0, pl.ds(qrow + r0, ATTN_SUB), :])
                s = s_ref[pl.ds(r0, ATTN_SUB), :] + bias
                s_ref[pl.ds(r0, ATTN_SUB), :] = s
                m_prev = m_ref[h, rows, :]
                m_new = jnp.maximum(m_prev, jnp.max(s, axis=-1, keepdims=True))
                alpha_ref[pl.ds(r0, ATTN_SUB), :] = jnp.broadcast_to(jnp.exp2(m_prev - m_new),
                                                                     (ATTN_SUB, V_DIM))
                m_ref[h, rows, :] = m_new
                return carry

            def row_exp(k, carry, h=h, row0=row0):
                r0 = pl.multiple_of(k * ATTN_SUB, ATTN_SUB)
                m = m_ref[h, pl.ds(row0 + r0, ATTN_SUB), :]
                p_ref[pl.ds(r0, ATTN_SUB), :] = jnp.exp2(s_ref[pl.ds(r0, ATTN_SUB), :] - m).astype(BF16)
                return carry

            lax.fori_loop(0, ATTN_ROWS // ATTN_SUB, row_max, 0, unroll=ATTN_UNROLL)
            lax.fori_loop(0, ATTN_ROWS // ATTN_SUB, row_exp, 0, unroll=ATTN_UNROLL)
            blk = slice(row0, row0 + ATTN_ROWS)
            pv = _dot(p_ref[...], vb)
            alpha = alpha_ref[...]
            acc_ref[h, blk, :] = alpha * acc_ref[h, blk, :] + pv[:, :V_DIM]
            l_ref[h, blk, :] = alpha * l_ref[h, blk, :] + pv[:, V_DIM:]

    @pl.when(j == i)
    def _():
        lam = _diff_lambda(lp_ref[...], layer)
        for h in range(N_HEADS):
            o2 = acc_ref[h] / l_ref[h]
            o = o2[:t] - lam * o2[t:]
            o = _rms(o) * subln_ref[...] * (1.0 - _lambda_init(layer))
            o_ref[0, :, h * V_DIM:(h + 1) * V_DIM] = o.astype(BF16)


def _attn_prompt(q, kt, v, bias_tiles, rel_bias, lp, subln, *, layer, t):
    B, T, _ = q.shape
    kv_layer = kt.shape[0] - 1
    n = T // t
    pairs = [(i, j) for i in range(n) for j in range(i + 1)]
    qi = jnp.asarray([p[0] for p in pairs], jnp.int32)
    kj = jnp.asarray([p[1] for p in pairs], jnp.int32)
    grid_spec = pltpu.PrefetchScalarGridSpec(
        num_scalar_prefetch=2,
        grid=(B, len(pairs)),
        in_specs=[pl.BlockSpec((1, t, QK_W), lambda b, s, qi, kj: (b, qi[s], 0)),
                  pl.BlockSpec((1, 1, QK_W, t), lambda b, s, qi, kj: (kv_layer, b, 0, kj[s])),
                  pl.BlockSpec((1, 1, t * N_HEADS, V_DIM), lambda b, s, qi, kj: (kv_layer, b, kj[s], 0)),
                  pl.BlockSpec((N_HEADS, 1, t, t),
                               lambda b, s, qi, kj: (0, jnp.minimum(qi[s] - kj[s], 1), 0, 0)),
                  pl.BlockSpec(memory_space=pltpu.SMEM),
                  pl.BlockSpec(lp.shape, lambda *_: (0, 0)),
                  pl.BlockSpec(subln.shape, lambda *_: (0, 0))],
        out_specs=pl.BlockSpec((1, t, ATTN_W), lambda b, s, qi, kj: (b, qi[s], 0)),
        scratch_shapes=[pltpu.VMEM((N_HEADS, 2 * t, 1), F32),
                        pltpu.VMEM((N_HEADS, 2 * t, V_DIM), F32),
                        pltpu.VMEM((N_HEADS, 2 * t, V_DIM), F32),
                        pltpu.VMEM((ATTN_ROWS, t), F32),
                        pltpu.VMEM((ATTN_ROWS, t), BF16),
                        pltpu.VMEM((ATTN_ROWS, V_DIM), F32)])
    return pl.pallas_call(
        functools.partial(_attn_kernel, t=t, layer=layer),
        grid_spec=grid_spec,
        out_shape=jax.ShapeDtypeStruct((B, T, ATTN_W), BF16),
        compiler_params=_params(2),
        name="attn_prompt",
    )(qi, kj, q, kt, v, bias_tiles, rel_bias, lp, subln)


def _attn_sample_kernel(pt_ref, q_ref, kn_ref, vn_ref, *refs, n_pages, page, layer, group):
    k_refs = refs[:group * n_pages]
    v_refs = refs[group * n_pages:2 * group * n_pages]
    bias_ref, lp_ref, subln_ref, o_ref = refs[2 * group * n_pages:]
    for g in range(group):
        pages = slice(g * n_pages, (g + 1) * n_pages)
        o_ref[g] = _attn_sample_one(q_ref[g], kn_ref[g], vn_ref[g], k_refs[pages], v_refs[pages],
                                    bias_ref, lp_ref, subln_ref, n_pages=n_pages, page=page, layer=layer)


def _attn_sample_one(q_row, kn_row, vn_row, k_refs, v_refs, bias_ref, lp_ref, subln_ref, *, n_pages, page,
                     layer):
    rows = 2 * N_HEADS
    row = lax.broadcasted_iota(jnp.int32, (rows, QK_W), 0)
    seg = lax.broadcasted_iota(jnp.int32, (rows, QK_W), 1) // HEAD_DIM
    q = jnp.broadcast_to(q_row.astype(F32), (rows, QK_W))
    qbd_f = jnp.where(seg == (row % N_HEADS) * 2 + row // N_HEADS, q, 0.0)
    qbd = qbd_f.astype(BF16)

    s = jnp.concatenate([_dot(qbd, k_refs[p][0, 0].astype(BF16)) for p in range(n_pages)], axis=1)
    past = n_pages * page
    bias = bias_ref[...]
    bias = jnp.concatenate([bias, bias], axis=0)
    s = s + bias[:, :past]
    s_new = jnp.sum(qbd_f * kn_row, axis=-1, keepdims=True) + bias[:, past:past + 1]
    m = jnp.maximum(jnp.max(s, axis=-1, keepdims=True), s_new)
    p = jnp.exp2(s - m)
    p_new = jnp.exp2(s_new - m)
    denom = jnp.sum(p, axis=-1, keepdims=True) + p_new
    p = p / denom
    p_new = p_new / denom

    lam = _diff_lambda(lp_ref[...], layer)
    first = lax.broadcasted_iota(jnp.int32, (rows, 1), 0) < N_HEADS
    a = jnp.where(first, p - lam * pltpu.roll(p, N_HEADS, 0), 0.0).astype(BF16)
    a_new = p_new[:N_HEADS] - lam * p_new[N_HEADS:]

    out = jnp.zeros((rows, ATTN_W), F32)
    for pg in range(n_pages):
        vp = jnp.concatenate([v_refs[pg][0, 0, pl.ds(h, page, stride=N_HEADS), :]
                              for h in range(N_HEADS)], axis=1)
        out = out + _dot(a[:, pg * page:(pg + 1) * page], vp.astype(BF16))
    out = out[:N_HEADS] + a_new * vn_row
    rowh = lax.broadcasted_iota(jnp.int32, (N_HEADS, ATTN_W), 0)
    head = lax.broadcasted_iota(jnp.int32, (N_HEADS, ATTN_W), 1) // V_DIM
    om = jnp.where(rowh == head, out, 0.0)
    ms = jnp.sum(om * om, axis=-1, keepdims=True) / V_DIM
    on = om * lax.rsqrt(ms + EPS)
    o = jnp.sum(on, axis=0, keepdims=True) * subln_ref[...] * (1.0 - _lambda_init(layer))
    return o.astype(BF16)


def _attn_sample(q, k_new, v_new, cache_kt, cache_vr, page_table, bias_rows, lp, subln4, *, layer):
    DB = q.shape[0]
    n_pages = page_table.shape[1]
    page = cache_kt.shape[3]
    group = SAMPLES_PER_STEP
    assert DB % group == 0, (DB, group)
    slots = [(g, p) for g in range(group) for p in range(n_pages)]

    def page_spec(g, p):
        return pl.BlockSpec((1, 1, QK_W, page), lambda b, pt: (layer, pt[b * group + g, p], 0, 0))

    def vpage_spec(g, p):
        return pl.BlockSpec((1, 1, page * N_HEADS, V_DIM), lambda b, pt: (layer, pt[b * group + g, p], 0, 0))

    vec = pl.BlockSpec((group, 1, QK_W), lambda b, pt: (b, 0, 0))
    grid_spec = pltpu.PrefetchScalarGridSpec(
        num_scalar_prefetch=1,
        grid=(DB // group,),
        in_specs=([vec, vec, vec] + [page_spec(g, p) for g, p in slots]
                  + [vpage_spec(g, p) for g, p in slots]
                  + [pl.BlockSpec(bias_rows.shape, lambda *_: (0, 0)),
                     pl.BlockSpec(lp.shape, lambda *_: (0, 0)),
                     pl.BlockSpec(subln4.shape, lambda *_: (0, 0))]),
        out_specs=vec)
    return pl.pallas_call(
        functools.partial(_attn_sample_kernel, n_pages=n_pages, page=page, layer=layer, group=group),
        grid_spec=grid_spec,
        out_shape=jax.ShapeDtypeStruct((DB, 1, ATTN_W), BF16),
        compiler_params=_params(1),
        name="attn_sample",
    )(page_table, q, k_new, v_new, *([cache_kt] * len(slots)), *([cache_vr] * len(slots)),
      bias_rows, lp, subln4)


def _pool_sample_kernel(state_ref, u_ref, d_ref):
    u = u_ref[...]
    outs = []
    for gi, w in enumerate(POOL_WINDOWS):
        ch = slice(gi * POOL_GROUP_W, (gi + 1) * POOL_GROUP_W)
        s = u[:, ch]
        for r in range(POOL_BUF - (w - 1), POOL_BUF):
            s = s + state_ref[r, :, ch]
        outs.append(s / float(w) - u[:, ch])
    d_ref[...] = jnp.concatenate(outs, axis=-1).astype(BF16)


def _pool_sample(state_t, u):
    return pl.pallas_call(
        _pool_sample_kernel,
        out_shape=jax.ShapeDtypeStruct(u.shape, BF16),
        name="pool_sample",
    )(state_t, u)


def _mix_kernel(o_ref, d_ref, g_ref, x_ref, gt_ref, sh_ref, sc_ref, wp_ref, ps_ref, wb_ref, wo_ref,
                gffn_ref, *rest, router):
    d = d_ref[0]
    y = jnp.concatenate([_dot(d[:, g * POOL_GROUP_W:(g + 1) * POOL_GROUP_W], wp_ref[g])
                         for g in range(len(POOL_WINDOWS))], axis=-1)
    y = (y * ps_ref[...]).astype(BF16)
    pa = _dot(o_ref[0], wb_ref[:ATTN_W])
    pb = _dot(y, wb_ref[ATTN_W:])
    g = g_ref[0]
    merged = g[:, :D_MODEL].astype(F32) * pa + g[:, D_MODEL:].astype(F32) * pb
    xn = x_ref[0] + gt_ref[0] * _dot(merged.astype(BF16), wo_ref[...])
    h2 = _rms(xn) * gffn_ref[...]
    h2 = h2 * (1.0 + sc_ref[0]) + sh_ref[0]
    if not router:
        xn_ref, h2_ref = rest
        xn_ref[0] = xn
        h2_ref[0] = h2.astype(h2_ref.dtype)
        return

    wr_hi_ref, wr_lo_ref, xn_ref, h2_ref, ri_ref, rp_ref = rest
    xn_ref[0] = xn
    _slab_store(h2_ref, (0,), h2)
    hi = h2.astype(BF16)
    lo = (h2 - hi.astype(F32)).astype(BF16)
    logits = _dot(hi, wr_hi_ref[...]) + _dot(lo, wr_hi_ref[...]) + _dot(hi, wr_lo_ref[...])
    lane = lax.broadcasted_iota(jnp.int32, logits.shape, 1)
    lg = jnp.where(lane < N_EXPERTS, logits, -jnp.inf)
    m1 = jnp.max(lg, axis=-1, keepdims=True)
    i1 = jnp.min(jnp.where(lg == m1, lane, LANES), axis=-1, keepdims=True)
    lg = jnp.where(lane == i1, -jnp.inf, lg)
    m2 = jnp.max(lg, axis=-1, keepdims=True)
    i2 = jnp.min(jnp.where(lg == m2, lane, LANES), axis=-1, keepdims=True)
    e = jnp.exp(m2 - m1)
    ri_ref[0] = jnp.concatenate([i1, i2], axis=-1)
    rp_ref[0] = jnp.concatenate([1.0 / (1.0 + e), e / (1.0 + e)], axis=-1)


def _mix(o, d, g, x, mod, wp_bf, pscale, wb_bf, wo_bf, gffn, router_w, *, tm):
    B, T, _ = x.shape
    router = router_w is not None
    ins = [o, d, g, x, mod, mod, mod, wp_bf, pscale, wb_bf, wo_bf, gffn]
    in_specs = [_row_spec(tm, ATTN_W), _row_spec(tm, POOL_W), _row_spec(tm, 2 * D_MODEL),
                _row_spec(tm, D_MODEL), _mod_spec(mod, tm, 2), _mod_spec(mod, tm, 3),
                _mod_spec(mod, tm, 4)] + [_const_spec(a) for a in ins[7:]]
    out_shape = [jax.ShapeDtypeStruct((B, T, D_MODEL), F32), jax.ShapeDtypeStruct((B, T, D_MODEL), BF16)]
    out_specs = [_row_spec(tm, D_MODEL), _row_spec(tm, D_MODEL)]
    if router:
        out_shape[1] = jax.ShapeDtypeStruct((B, T * ROW_SLAB, LANES), F32)
        out_specs[1] = _row_spec(tm * ROW_SLAB, LANES)
        ins += list(router_w)
        in_specs += [_const_spec(a) for a in router_w]
        out_shape += [jax.ShapeDtypeStruct((B, T, TOP_K), jnp.int32),
                      jax.ShapeDtypeStruct((B, T, TOP_K), F32)]
        out_specs += [_row_spec(tm, TOP_K), _row_spec(tm, TOP_K)]
    return pl.pallas_call(
        functools.partial(_mix_kernel, router=router),
        grid=(B, T // tm),
        in_specs=in_specs,
        out_specs=out_specs,
        out_shape=out_shape,
        compiler_params=_params(2),
        name="mix",
    )(*ins)


def _ffn_kernel(h_ref, x_ref, gt_ref, wg_ref, wu_ref, wd_ref, y_ref, acc_ref):
    f = pl.program_id(2)

    @pl.when(f == 0)
    def _():
        acc_ref[...] = jnp.zeros_like(acc_ref)

    h = h_ref[0]
    a = (_silu(_dot(h, wg_ref[...])) * _dot(h, wu_ref[...])).astype(BF16)
    acc_ref[...] += _dot(a, wd_ref[...])

    @pl.when(f == pl.num_programs(2) - 1)
    def _():
        y_ref[0] = x_ref[0] + gt_ref[0] * acc_ref[...]


def _ffn_dense(h2, x, mod, wg_bf, wu_bf, wd_bf, *, tm):
    B, T, _ = x.shape
    d_ff = wg_bf.shape[1]
    tf = d_ff // 2
    return pl.pallas_call(
        _ffn_kernel,
        grid=(B, T // tm, d_ff // tf),
        in_specs=[_row_spec(tm, D_MODEL), _row_spec(tm, D_MODEL), _mod_spec(mod, tm, 5),
                  pl.BlockSpec((D_MODEL, tf), lambda b, i, f: (0, f)),
                  pl.BlockSpec((D_MODEL, tf), lambda b, i, f: (0, f)),
                  pl.BlockSpec((tf, D_MODEL), lambda b, i, f: (f, 0))],
        out_specs=_row_spec(tm, D_MODEL),
        out_shape=jax.ShapeDtypeStruct((B, T, D_MODEL), F32),
        scratch_shapes=[pltpu.VMEM((tm, D_MODEL), F32)],
        compiler_params=_params(3),
        name="ffn_dense",
    )(h2, x, mod, wg_bf, wu_bf, wd_bf)


ROW_SLAB = D_MODEL // LANES


def _slab_copy(src_hbm, src_row, dst, dst_row, sem):
    src = src_hbm.at[pl.ds(pl.multiple_of(src_row * ROW_SLAB, ROW_SLAB), ROW_SLAB), :]
    return pltpu.make_async_copy(src, dst.at[pl.ds(pl.multiple_of(dst_row * ROW_SLAB, ROW_SLAB), ROW_SLAB), :], sem)


def _slab_store(ref, idx, x):
    rows = x.shape[0]
    for s in range(ROW_SLAB):
        ref[idx + (pl.ds(s, rows, stride=ROW_SLAB), slice(None))] = x[:, s * LANES:(s + 1) * LANES]


def _slab_load(ref, idx, rows, start=0):
    return jnp.concatenate([ref[idx + (pl.ds(start + s, rows, stride=ROW_SLAB), slice(None))]
                            for s in range(ROW_SLAB)], axis=1)


def _moe_ffn_kernel(te_ref, nu_ref, src_ref, src_next_ref, h_hbm, wg_ref, wu_ref, wd_ref, y_ref,
                    xbuf, xb_ref, acc_ref, sem, *, tm):
    i = pl.program_id(0)
    f = pl.program_id(1)
    nu = nu_ref[0]
    slot = i % 2

    def issue(idx_ref, dst):
        def body(r, carry):
            _slab_copy(h_hbm, idx_ref[0, 0, r], xbuf.at[dst], r, sem.at[dst]).start()
            return carry
        lax.fori_loop(0, tm, body, 0, unroll=8)

    def drain(dst):
        def body(r, carry):
            _slab_copy(h_hbm, 0, xbuf.at[dst], 0, sem.at[dst]).wait()
            return carry
        lax.fori_loop(0, tm, body, 0, unroll=8)

    @pl.when(i < nu)
    def _():
        @pl.when(f == 0)
        def _():
            @pl.when(i == 0)
            def _():
                issue(src_ref, 0)

            drain(slot)

            @pl.when(i + 1 < nu)
            def _():
                issue(src_next_ref, 1 - slot)

            acc_ref[...] = jnp.zeros_like(acc_ref)
            xb_ref[...] = _slab_load(xbuf, (slot,), tm).astype(BF16)

        xb = xb_ref[...]
        a = (_silu(_dot(xb, wg_ref[0].astype(BF16))) * _dot(xb, wu_ref[0].astype(BF16))).astype(BF16)
        acc_ref[...] += _dot(a, wd_ref[0].astype(BF16))

        @pl.when(f == pl.num_programs(1) - 1)
        def _():
            _slab_store(y_ref, (), acc_ref[...])

    @pl.when(i >= nu)
    def _():
        y_ref[...] = jnp.zeros_like(y_ref)


def _moe_ffn(h_rows, src, tile_expert, n_used, wg, wu, wd, *, tm, tf):
    n_tiles = src.shape[0]
    d_ff = wg.shape[2]
    assert d_ff % tf == 0, (d_ff, tf)
    nf = d_ff // tf

    def tile(i, nu):
        return jnp.minimum(i, nu[0] - 1)

    def ff(i, f, nu):
        return jnp.where(i < nu[0], f, nf - 1)

    grid_spec = pltpu.PrefetchScalarGridSpec(
        num_scalar_prefetch=2,
        grid=(n_tiles, nf),
        in_specs=[pl.BlockSpec((1, 1, tm), lambda i, f, te, nu: (i, 0, 0), memory_space=pltpu.SMEM),
                  pl.BlockSpec((1, 1, tm), lambda i, f, te, nu: (jnp.minimum(i + 1, n_tiles - 1), 0, 0),
                               memory_space=pltpu.SMEM),
                  pl.BlockSpec(memory_space=pl.ANY),
                  pl.BlockSpec((1, D_MODEL, tf), lambda i, f, te, nu: (te[tile(i, nu)], 0, ff(i, f, nu))),
                  pl.BlockSpec((1, D_MODEL, tf), lambda i, f, te, nu: (te[tile(i, nu)], 0, ff(i, f, nu))),
                  pl.BlockSpec((1, tf, D_MODEL), lambda i, f, te, nu: (te[tile(i, nu)], ff(i, f, nu), 0))],
        out_specs=pl.BlockSpec((tm * ROW_SLAB, LANES), lambda i, f, te, nu: (i, 0)),
        scratch_shapes=[pltpu.VMEM((2, tm * ROW_SLAB, LANES), F32), pltpu.VMEM((tm, D_MODEL), BF16),
                        pltpu.VMEM((tm, D_MODEL), F32),
                        pltpu.SemaphoreType.DMA((2,))])
    return pl.pallas_call(
        functools.partial(_moe_ffn_kernel, tm=tm),
        grid_spec=grid_spec,
        out_shape=jax.ShapeDtypeStruct((n_tiles * tm * ROW_SLAB, LANES), F32),
        compiler_params=_params(2),
        name="moe_ffn",
    )(tile_expert, n_used, src, src, h_rows, wg, wu, wd)


def _combine_kernel(pos_ref, pos_next_ref, y_hbm, x_ref, gt_ref, p_ref, o_ref, buf, sem, *, tc):
    g = pl.program_id(0) * pl.num_programs(1) + pl.program_id(1)
    n_steps = pl.num_programs(0) * pl.num_programs(1)
    slot = g % 2

    def issue(idx_ref, dst):
        def body(t, carry):
            for k in range(TOP_K):
                _slab_copy(y_hbm, idx_ref[0, 0, TOP_K * t + k], buf.at[dst, k], t, sem.at[dst]).start()
            return carry
        lax.fori_loop(0, tc, body, 0, unroll=4)

    @pl.when(g == 0)
    def _():
        issue(pos_ref, 0)

    def drain(t, carry):
        for k in range(TOP_K):
            _slab_copy(y_hbm, 0, buf.at[slot, 0], 0, sem.at[slot]).wait()
        return carry

    lax.fori_loop(0, tc, drain, 0, unroll=4)

    @pl.when(g + 1 < n_steps)
    def _():
        issue(pos_next_ref, 1 - slot)

    p = p_ref[0]
    f = p[:, 0:1] * _slab_load(buf, (slot, 0), tc) + p[:, 1:2] * _slab_load(buf, (slot, 1), tc)
    o_ref[0] = x_ref[0] + gt_ref[0] * f


def _combine(y, pos, probs, x, mod, *, tc):
    B, T, _ = x.shape
    nt = T // tc
    last = B * nt - 1
    pos_chunks = pos.reshape(B * nt, 1, TOP_K * tc)
    return pl.pallas_call(
        functools.partial(_combine_kernel, tc=tc),
        grid=(B, nt),
        in_specs=[pl.BlockSpec((1, 1, TOP_K * tc), lambda b, i: (b * nt + i, 0, 0),
                               memory_space=pltpu.SMEM),
                  pl.BlockSpec((1, 1, TOP_K * tc), lambda b, i: (jnp.minimum(b * nt + i + 1, last), 0, 0),
                               memory_space=pltpu.SMEM),
                  pl.BlockSpec(memory_space=pl.ANY),
                  _row_spec(tc, D_MODEL), _mod_spec(mod, tc, 5), _row_spec(tc, TOP_K)],
        out_specs=_row_spec(tc, D_MODEL),
        out_shape=jax.ShapeDtypeStruct(x.shape, F32),
        scratch_shapes=[pltpu.VMEM((2, TOP_K, tc * ROW_SLAB, LANES), F32), pltpu.SemaphoreType.DMA((2,))],
        compiler_params=_params(2),
        name="moe_combine",
    )(pos_chunks, pos_chunks, y, x, mod, probs)


def _route_plan(top_i, tm, n_tiles):
    e_flat = top_i.reshape(-1)
    onehot = (e_flat[:, None] == jnp.arange(N_EXPERTS, dtype=jnp.int32)[None, :]).astype(jnp.int32)
    csum = jnp.cumsum(onehot, axis=0)
    cnt = csum[-1]
    gsz = (cnt + tm - 1) // tm * tm
    gend = jnp.cumsum(gsz)
    gstart = gend - gsz
    pos = jnp.sum(onehot * (gstart[None, :] + csum - 1), axis=1).astype(jnp.int32)
    n_used = (gend[-1] // tm).astype(jnp.int32).reshape(1)
    tile_start = jnp.arange(n_tiles, dtype=jnp.int32) * tm
    tile_expert = jnp.sum((tile_start[:, None] >= gend[None, :]).astype(jnp.int32), axis=1)
    tile_expert = jnp.minimum(tile_expert, N_EXPERTS - 1).astype(jnp.int32)
    token = jnp.arange(e_flat.shape[0], dtype=jnp.int32) // TOP_K
    src = jnp.zeros((n_tiles * tm,), jnp.int32).at[pos].set(token, unique_indices=True)
    return pos.reshape(top_i.shape), src.reshape(n_tiles, 1, tm), tile_expert, n_used


def _moe(groups, wg, wu, wd):
    tm = MOE_ROW_TILE
    counts = [g[1].shape[0] * g[1].shape[1] for g in groups]
    total = sum(counts) * TOP_K
    n_tiles = (total + N_EXPERTS * (tm - 1)) // tm + 1
    top_all = jnp.concatenate([g[3].reshape(-1, TOP_K) for g in groups], axis=0)
    h_all = jnp.concatenate([g[0].reshape(-1, LANES) for g in groups], axis=0)
    pos_all, src, tile_expert, n_used = _route_plan(top_all, tm, n_tiles)
    y = _moe_ffn(h_all, src, tile_expert, n_used, wg, wu, wd, tm=tm, tf=MOE_FF_TILE)
    outs = []
    offs = 0
    for (h2, x, mod, top_i, probs), cnt in zip(groups, counts):
        pos = pos_all[offs:offs + cnt].reshape(top_i.shape)
        offs += cnt
        outs.append(_combine(y, pos, probs, x, mod, tc=min(MOE_DMA_CHUNK, x.shape[1])))
    return outs


def kernel(x_prompt, x_sample, c_prompt, c_sample, cache_k, cache_v, state_pool, page_table,
           rel_bias, w_ada, b_ada, g_mix, g_ffn, w_in, q_norm, k_norm, diff_lambda, subln,
           w_pool, pool_scale, w_branch, w_out, w_ff_gate, w_ff_up, w_ff_down, w_router,
           w_exp_gate, w_exp_up, w_exp_down):
    B, S, D = x_prompt.shape
    DB = x_sample.shape[0]
    depth = w_in.shape[0]
    n_phys, page = cache_k.shape[1], cache_k.shape[2]
    past = page_table.shape[1] * page

    mod_all = _ada_mod(jnp.concatenate([c_prompt, c_sample], axis=0), w_ada, b_ada)
    bias_tiles = _bias_tiles(rel_bias, ATTN_TILE)
    bias_rows = _bias_rows(rel_bias, past)

    cache_kt = jnp.transpose(cache_k, (0, 1, 3, 4, 5, 2)).reshape(depth, n_phys, QK_W, page)
    cache_vr = cache_v.reshape(depth, n_phys, page * N_HEADS, V_DIM)
    state_t = jnp.transpose(state_pool, (0, 2, 1, 3))

    bd = jnp.kron(jnp.eye(QK_W // HEAD_DIM, dtype=F32),
                  jnp.full((HEAD_DIM, HEAD_DIM), 1.0 / HEAD_DIM, F32)).astype(BF16)

    xp = x_prompt
    xs = x_sample.reshape(1, DB, D)
    outs = {k: [] for k in ("pp", "ks", "vs", "ps")}
    prev_kv = ((), ())
    for l in range(depth):
        mod_p = mod_all[l, :B].reshape(B, 1, 6 * D)
        mod_s = mod_all[l, B:].reshape(1, DB, 6 * D)
        w_bf = w_in[l].astype(BF16)
        wkt_bf = w_in[l][:, QK_W:2 * QK_W].T.astype(BF16)
        qg = jnp.tile(q_norm[l].reshape(1, 2 * HEAD_DIM), (1, N_HEADS)) * (HEAD_DIM ** -0.5 * LOG2E)
        kg = jnp.tile(k_norm[l], (N_HEADS, 1)).reshape(N_HEADS * 2, HEAD_DIM, 1)
        gmix = g_mix[l].reshape(1, D)
        gffn = g_ffn[l].reshape(1, D)
        lp = diff_lambda[l]
        sub1 = subln[l].reshape(1, V_DIM)
        sub4 = jnp.tile(sub1, (1, N_HEADS))
        wp_bf = w_pool[l].astype(BF16)
        pscale = pool_scale[l].reshape(1, POOL_W)
        wb_bf = w_branch[l].astype(BF16)
        wo_bf = w_out[l].astype(BF16)
        moe = l % 2 == 1
        j = l // 2
        router_w = None
        if moe:
            wr = jnp.pad(w_router[j], ((0, 0), (0, LANES - N_EXPERTS)))
            wr_hi = wr.astype(BF16)
            router_w = (wr_hi, (wr - wr_hi.astype(F32)).astype(BF16))

        stack = prev_kv if l == depth - 1 else ((), ())
        qp, ktp, vp, up, gp, dp = _in_proj(xp, mod_p, gmix, w_bf, wkt_bf, qg, kg, bd, *stack,
                                           tm=ROW_TILE, seq_pool=True)
        prev_kv = (prev_kv[0] + (ktp,), prev_kv[1] + (vp,))
        op = _attn_prompt(qp, ktp, vp, bias_tiles, rel_bias, lp, sub1, layer=l, t=ATTN_TILE)
        mix_p = _mix(op, dp, gp, xp, mod_p, wp_bf, pscale, wb_bf, wo_bf, gffn, router_w, tm=ROW_TILE)

        qs, kts, vs, us, gs = _in_proj(xs, mod_s, gmix, w_bf, wkt_bf, qg, kg, bd,
                                       tm=DB, seq_pool=False)
        ks_rows = jnp.transpose(kts[0], (0, 2, 1)).reshape(DB, 1, QK_W)
        osamp = _attn_sample(qs.reshape(DB, 1, QK_W), ks_rows, vs.reshape(DB, 1, ATTN_W),
                             cache_kt, cache_vr, page_table, bias_rows, lp, sub4, layer=l)
        ds = _pool_sample(state_t[l], us[0])
        mix_s = _mix(osamp.reshape(1, DB, ATTN_W), ds.reshape(1, DB, POOL_W), gs, xs, mod_s,
                     wp_bf, pscale, wb_bf, wo_bf, gffn, router_w, tm=DB)

        if moe:
            xp, xs = _moe([(mix_p[1], mix_p[0], mod_p, mix_p[2], mix_p[3]),
                           (mix_s[1], mix_s[0], mod_s, mix_s[2], mix_s[3])],
                          w_exp_gate[j], w_exp_up[j], w_exp_down[j])
        else:
            wg_bf = w_ff_gate[j].astype(BF16)
            wu_bf = w_ff_up[j].astype(BF16)
            wd_bf = w_ff_down[j].astype(BF16)
            xp = _ffn_dense(mix_p[1], mix_p[0], mod_p, wg_bf, wu_bf, wd_bf, tm=FFN_ROW_TILE)
            xs = _ffn_dense(mix_s[1], mix_s[0], mod_s, wg_bf, wu_bf, wd_bf, tm=DB)

        outs["pp"].append(up[:, S - POOL_BUF:, :])
        outs["ks"].append(ks_rows.reshape(DB, 1, N_HEADS, 2, HEAD_DIM))
        outs["vs"].append(vs.reshape(DB, 1, N_HEADS, V_DIM))
        outs["ps"].append(jnp.concatenate([state_pool[l][:, 1:], us.reshape(DB, 1, POOL_W)], axis=1))

    return (xp, xs.reshape(DB, 1, D),
            jnp.transpose(ktp.reshape(depth, B, N_HEADS, 2, HEAD_DIM, S), (0, 1, 5, 2, 3, 4)),
            vp.reshape(depth, B, S, N_HEADS, V_DIM), jnp.stack(outs["pp"]),
            jnp.stack(outs["ks"]), jnp.stack(outs["vs"]), jnp.stack(outs["ps"]))
```
